```python
import math
import jax, jax.numpy as jnp
from jax import lax
import numpy as np

D_MODEL = 1024
BATCH = 8
SEQ = 2048
DEPTH = 1
DEC_BATCH = 32
DEC_SEQ = 8
PAST_LEN = 16384
PAGE_SIZE = 128

N_META = 16
MLA_HEADS = 8
MLA_NOPE = 64
MLA_ROPE = 32
MLA_V = 64
MLA_KV_LORA = 256
MLA_Q_LORA = 768
MLA_SCALE = (MLA_NOPE + MLA_ROPE) ** -0.5
DSA_HEADS = 8
DSA_HEAD_DIM = 64
DSA_SCALE = DSA_HEAD_DIM ** -0.5
IDX_HEADS = 8
IDX_DIM = 64
TOPK_MAX = 256
D_FF = 2816
CONV_W = 3
ROPE_THETA = 10000.0
EPS = 1e-6
Q_BLOCK = 128
NEG_INF = -1e30
MIX_SIZES = (MLA_Q_LORA, MLA_KV_LORA, MLA_ROPE,
             DSA_HEADS * DSA_HEAD_DIM, DSA_HEADS * DSA_HEAD_DIM, DSA_HEADS * DSA_HEAD_DIM,
             IDX_HEADS * IDX_DIM, IDX_DIM, IDX_HEADS, D_MODEL, D_MODEL)
D_IN = (MLA_Q_LORA + MLA_KV_LORA + MLA_ROPE + 3 * DSA_HEADS * DSA_HEAD_DIM
        + IDX_HEADS * IDX_DIM + IDX_DIM + IDX_HEADS + 2 * D_MODEL)

kernel_name = 'mla_dsa_gated_hybrid_step'


def rms_norm(x, g):
    xf = x.astype(jnp.float32)
    y = xf * lax.rsqrt(jnp.mean(xf * xf, axis=-1, keepdims=True) + EPS)
    return (y * g.astype(jnp.float32)).astype(x.dtype)


def rope(x, pos):
    half = x.shape[-1] // 2
    inv_freq = 1.0 / (ROPE_THETA ** (jnp.arange(half, dtype=jnp.float32) / half))
    ang = pos.astype(jnp.float32)[:, None] * inv_freq[None, :]
    cos = jnp.cos(ang)[:, None, :]
    sin = jnp.sin(ang)[:, None, :]
    xf = x.astype(jnp.float32)
    x1, x2 = xf[..., :half], xf[..., half:]
    return jnp.concatenate([x1 * cos - x2 * sin, x2 * cos + x1 * sin], axis=-1).astype(x.dtype)


def project_mixers(h, pos, lw):
    B, L, _ = h.shape
    z = jnp.einsum('bld,de->ble', h, lw['w_in'])
    cuts = [int(c) for c in np.cumsum(MIX_SIZES)[:-1]]
    c_q, c_kv, k_pe, q_b, k_b, v_b, q_i, k_i, w_i, gate_a, gate_b = jnp.split(z, cuts, axis=-1)
    q = jnp.einsum('blc,chd->blhd', rms_norm(c_q, lw['g_q']), lw['w_uq'])
    q_lat = jnp.einsum('blhn,chn->blhc', q[..., :MLA_NOPE], lw['w_uk'])
    q_pe = rope(q[..., MLA_NOPE:], pos)
    c_kv = rms_norm(c_kv, lw['g_kv'])
    k_pe = rope(k_pe[:, :, None, :], pos)[:, :, 0]
    q_b = rope(q_b.reshape(B, L, DSA_HEADS, DSA_HEAD_DIM), pos)
    k_b = rope(k_b.reshape(B, L, DSA_HEADS, DSA_HEAD_DIM), pos)
    v_b = v_b.reshape(B, L, DSA_HEADS, DSA_HEAD_DIM)
    q_i = rope(q_i.reshape(B, L, IDX_HEADS, IDX_DIM), pos)
    k_i = rope(k_i[:, :, None, :], pos)[:, :, 0]
    w_i = w_i * (IDX_HEADS ** -0.5)
    return (q_lat, q_pe, q_b, q_i, w_i), (c_kv, k_pe, k_b, v_b, k_i), (gate_a, gate_b)


def mla_attend(q_lat, q_pe, q_pos, c_kv, k_pe, w_uv):
    s = jnp.einsum('bqhc,bkc->bhqk', q_lat, c_kv) + jnp.einsum('bqhr,bkr->bhqk', q_pe, k_pe)
    s = s.astype(jnp.float32) * MLA_SCALE
    k_pos = jnp.arange(c_kv.shape[1])
    s = jnp.where(k_pos[None, None, None, :] <= q_pos[None, None, :, None], s, NEG_INF)
    p = jax.nn.softmax(s, axis=-1).astype(c_kv.dtype)
    o_lat = jnp.einsum('bhqk,bkc->bqhc', p, c_kv)
    o = jnp.einsum('bqhc,chv->bqhv', o_lat, w_uv)
    return o.reshape(o.shape[0], o.shape[1], MLA_HEADS * MLA_V)


def indexer_topk(q_i, w_i, q_pos, k_i, topk):
    s = jnp.einsum('bqhd,bkd->bqhk', q_i, k_i).astype(jnp.float32) * (IDX_DIM ** -0.5)
    score = jnp.einsum('bqhk,bqh->bqk', jax.nn.relu(s), w_i.astype(jnp.float32))
    k_pos = jnp.arange(k_i.shape[1])
    score = jnp.where(k_pos[None, None, :] <= q_pos[None, :, None], score, NEG_INF)
    _, idx = lax.top_k(score, topk)
    return idx


def dsa_attend(q, q_pos, k_sel, v_sel, idx):
    s = jnp.einsum('bqhd,bqshd->bhqs', q, k_sel).astype(jnp.float32) * DSA_SCALE
    valid = idx <= q_pos[None, :, None]
    s = jnp.where(valid[:, None], s, NEG_INF)
    p = jax.nn.softmax(s, axis=-1).astype(v_sel.dtype)
    o = jnp.einsum('bhqs,bqshd->bqhd', p, v_sel)
    return o.reshape(o.shape[0], o.shape[1], DSA_HEADS * DSA_HEAD_DIM)


def gather_rows(rows, idx):
    return jax.vmap(lambda rb, ib: rb[ib])(rows, idx)


def gather_selected(pool, page_table, new_rows, idx, past_len):
    B = idx.shape[0]
    in_past = idx < past_len
    ip = jnp.minimum(idx, past_len - 1)
    phys = jnp.take_along_axis(page_table, (ip // PAGE_SIZE).reshape(B, -1), axis=1).reshape(idx.shape)
    from_pool = pool[phys, ip % PAGE_SIZE]
    inew = jnp.clip(idx - past_len, 0, new_rows.shape[1] - 1)
    from_new = gather_rows(new_rows, inew)
    return jnp.where(in_past[..., None, None], from_pool, from_new)


def merge_branches(x, mla_o, dsa_o, gate_a, gate_b, lw):
    o = (jax.nn.sigmoid(gate_a) * jnp.einsum('blv,vd->bld', mla_o, lw['w_br_a'])
         + jax.nn.sigmoid(gate_b) * jnp.einsum('blv,vd->bld', dsa_o, lw['w_br_b']))
    return x + jnp.einsum('bld,de->ble', o, lw['w_o'])


def conv_ffn(h, conv_state, lw):
    L = h.shape[1]
    g = jnp.einsum('bld,df->blf', h, lw['w_ffn_g'])
    u = jnp.einsum('bld,df->blf', h, lw['w_ffn_u'])
    gp = jnp.concatenate([conv_state.astype(g.dtype), g], axis=1)
    gc = lw['ffn_conv_b'] + lw['ffn_conv_w'][0] * gp[:, 0:L]
    for j in range(1, CONV_W):
        gc = gc + lw['ffn_conv_w'][j] * gp[:, j:j + L]
    out = jnp.einsum('blf,fd->bld', jax.nn.silu(gc) * u, lw['w_ffn_d'])
    return out, gp[:, L:]


def prompt_layer(x, lw):
    B, L, _ = x.shape
    pos = jnp.arange(L)
    h = rms_norm(x, lw['norm1_g'])
    (q_lat, q_pe, q_b, q_i, w_i), (c_kv, k_pe, k_b, v_b, k_i), (gate_a, gate_b) = project_mixers(h, pos, lw)
    topk = min(TOPK_MAX, L // 4)
    nb = -(-L // Q_BLOCK)
    lpad = nb * Q_BLOCK

    def to_blocks(a):
        a = jnp.pad(a, [(0, 0), (0, lpad - L)] + [(0, 0)] * (a.ndim - 2))
        return jnp.moveaxis(a.reshape((B, nb, Q_BLOCK) + a.shape[2:]), 1, 0)

    def from_blocks(a):
        return jnp.moveaxis(a, 0, 1).reshape(B, lpad, a.shape[-1])[:, :L]

    def block(args):
        ql, qp, qb, qi, wi, qpos = args
        mla_o = mla_attend(ql, qp, qpos, c_kv, k_pe, lw['w_uv'])
        idx = indexer_topk(qi, wi, qpos, k_i, topk)
        dsa_o = dsa_attend(qb, qpos, gather_rows(k_b, idx), gather_rows(v_b, idx), idx)
        return mla_o, dsa_o

    q_pos_blocks = jnp.arange(lpad).reshape(nb, Q_BLOCK)
    mla_o, dsa_o = lax.map(block, (to_blocks(q_lat), to_blocks(q_pe), to_blocks(q_b),
                                   to_blocks(q_i), to_blocks(w_i), q_pos_blocks))
    x = merge_branches(x, from_blocks(mla_o), from_blocks(dsa_o), gate_a, gate_b, lw)
    zero_state = jnp.zeros((B, CONV_W - 1, D_FF), x.dtype)
    f, conv_new = conv_ffn(rms_norm(x, lw['norm2_g']), zero_state, lw)
    x = x + f
    return x, (c_kv, k_pe, k_b, v_b, k_i, conv_new)


def sample_layer(x, pool_ckv, pool_kpe, pool_k, pool_v, pool_ik, conv_state, page_table, lw):
    B, T, _ = x.shape
    past_len = page_table.shape[1] * PAGE_SIZE
    pos = past_len + jnp.arange(T)
    h = rms_norm(x, lw['norm1_g'])
    (q_lat, q_pe, q_b, q_i, w_i), (c_kv, k_pe, k_b, v_b, k_i), (gate_a, gate_b) = project_mixers(h, pos, lw)

    def paged_rows(pool):
        g = pool[page_table]
        return g.reshape((B, past_len) + pool.shape[2:])

    ckv_all = jnp.concatenate([paged_rows(pool_ckv).astype(c_kv.dtype), c_kv], axis=1)
    kpe_all = jnp.concatenate([paged_rows(pool_kpe).astype(k_pe.dtype), k_pe], axis=1)
    ik_all = jnp.concatenate([paged_rows(pool_ik).astype(k_i.dtype), k_i], axis=1)
    mla_o = mla_attend(q_lat, q_pe, pos, ckv_all, kpe_all, lw['w_uv'])
    topk = min(TOPK_MAX, (past_len + T) // 4)
    idx = indexer_topk(q_i, w_i, pos, ik_all, topk)
    k_sel = gather_selected(pool_k.astype(k_b.dtype), page_table, k_b, idx, past_len)
    v_sel = gather_selected(pool_v.astype(v_b.dtype), page_table, v_b, idx, past_len)
    dsa_o = dsa_attend(q_b, pos, k_sel, v_sel, idx)
    x = merge_branches(x, mla_o, dsa_o, gate_a, gate_b, lw)
    f, conv_new = conv_ffn(rms_norm(x, lw['norm2_g']), conv_state, lw)
    x = x + f
    return x, (c_kv, k_pe, k_b, v_b, k_i, conv_new)


def setup_inputs(seed: int = 0) -> dict:
    key = jax.random.key(seed)
    ks = jax.random.split(key, 32)
    n_pages = PAST_LEN // PAGE_SIZE
    n_used = DEC_BATCH * n_pages
    n_pool = n_used + n_used // 4
    nrm = jax.random.normal
    f32 = jnp.float32
    page_table = jax.random.permutation(ks[0], n_pool)[:n_used].reshape(DEC_BATCH, n_pages).astype(jnp.int32)
    return {
        'x_prompt': nrm(ks[1], (BATCH, SEQ, D_MODEL), f32),
        'x_sample': nrm(ks[2], (DEC_BATCH, DEC_SEQ, D_MODEL), f32),
        'cache_mla_ckv': nrm(ks[3], (DEPTH, n_pool, PAGE_SIZE, MLA_KV_LORA), f32),
        'cache_mla_kpe': nrm(ks[4], (DEPTH, n_pool, PAGE_SIZE, MLA_ROPE), f32),
        'cache_dsa_k': nrm(ks[5], (DEPTH, n_pool, PAGE_SIZE, DSA_HEADS, DSA_HEAD_DIM), f32),
        'cache_dsa_v': nrm(ks[6], (DEPTH, n_pool, PAGE_SIZE, DSA_HEADS, DSA_HEAD_DIM), f32),
        'cache_idx_k': nrm(ks[7], (DEPTH, n_pool, PAGE_SIZE, IDX_DIM), f32),
        'state_ffn_conv': nrm(ks[8], (DEPTH, DEC_BATCH, CONV_W - 1, D_FF), f32),
        'page_table': page_table,
        'meta_tokens': nrm(ks[9], (N_META, D_MODEL), f32),
        'norm1_g': 1.0 + 0.02 * nrm(ks[10], (DEPTH, D_MODEL), f32),
        'w_in': nrm(ks[11], (DEPTH, D_MODEL, D_IN), f32) * D_MODEL ** -0.5,
        'g_q': 1.0 + 0.02 * nrm(ks[12], (DEPTH, MLA_Q_LORA), f32),
        'g_kv': 1.0 + 0.02 * nrm(ks[13], (DEPTH, MLA_KV_LORA), f32),
        'w_uq': nrm(ks[14], (DEPTH, MLA_Q_LORA, MLA_HEADS, MLA_NOPE + MLA_ROPE), f32) * MLA_Q_LORA ** -0.5,
        'w_uk': nrm(ks[15], (DEPTH, MLA_KV_LORA, MLA_HEADS, MLA_NOPE), f32) * MLA_KV_LORA ** -0.5,
        'w_uv': nrm(ks[16], (DEPTH, MLA_KV_LORA, MLA_HEADS, MLA_V), f32) * MLA_KV_LORA ** -0.5,
        'w_br_a': nrm(ks[17], (DEPTH, MLA_HEADS * MLA_V, D_MODEL), f32) * (MLA_HEADS * MLA_V) ** -0.5,
        'w_br_b': nrm(ks[18], (DEPTH, DSA_HEADS * DSA_HEAD_DIM, D_MODEL), f32) * (DSA_HEADS * DSA_HEAD_DIM) ** -0.5,
        'w_o': nrm(ks[19], (DEPTH, D_MODEL, D_MODEL), f32) * D_MODEL ** -0.5,
        'norm2_g': 1.0 + 0.02 * nrm(ks[20], (DEPTH, D_MODEL), f32),
        'w_ffn_g': nrm(ks[21], (DEPTH, D_MODEL, D_FF), f32) * D_MODEL ** -0.5,
        'w_ffn_u': nrm(ks[22], (DEPTH, D_MODEL, D_FF), f32) * D_MODEL ** -0.5,
        'ffn_conv_w': nrm(ks[23], (DEPTH, CONV_W, D_FF), f32) * CONV_W ** -0.5,
        'ffn_conv_b': 0.02 * nrm(ks[24], (DEPTH, D_FF), f32),
        'w_ffn_d': nrm(ks[25], (DEPTH, D_FF, D_MODEL), f32) * D_FF ** -0.5,
        'final_g': 1.0 + 0.02 * nrm(ks[26], (D_MODEL,), f32),
    }


def reference(x_prompt, x_sample, cache_mla_ckv, cache_mla_kpe, cache_dsa_k, cache_dsa_v, cache_idx_k,
              state_ffn_conv, page_table, meta_tokens, norm1_g, w_in, g_q, g_kv, w_uq, w_uk, w_uv,
              w_br_a, w_br_b, w_o, norm2_g, w_ffn_g, w_ffn_u, ffn_conv_w, ffn_conv_b, w_ffn_d, final_g):
    B = x_prompt.shape[0]
    meta = jnp.broadcast_to(meta_tokens[None].astype(x_prompt.dtype), (B, N_META, x_prompt.shape[-1]))
    xp = jnp.concatenate([meta, x_prompt], axis=1)
    xs = x_sample
    new_p = []
    new_s = []
    for l in range(DEPTH):
        lw = {'norm1_g': norm1_g[l], 'w_in': w_in[l], 'g_q': g_q[l], 'g_kv': g_kv[l], 'w_uq': w_uq[l],
              'w_uk': w_uk[l], 'w_uv': w_uv[l], 'w_br_a': w_br_a[l], 'w_br_b': w_br_b[l], 'w_o': w_o[l],
              'norm2_g': norm2_g[l], 'w_ffn_g': w_ffn_g[l], 'w_ffn_u': w_ffn_u[l],
              'ffn_conv_w': ffn_conv_w[l], 'ffn_conv_b': ffn_conv_b[l], 'w_ffn_d': w_ffn_d[l]}
        xp, rows_p = prompt_layer(xp, lw)
        xs, rows_s = sample_layer(xs, cache_mla_ckv[l], cache_mla_kpe[l], cache_dsa_k[l], cache_dsa_v[l],
                                  cache_idx_k[l], state_ffn_conv[l], page_table, lw)
        new_p.append(rows_p)
        new_s.append(rows_s)
    y_prompt = rms_norm(xp, final_g)[:, N_META:]
    y_sample = rms_norm(xs, final_g)
    new_ckv_p = jnp.stack([r[0] for r in new_p], axis=0)
    new_kpe_p = jnp.stack([r[1] for r in new_p], axis=0)
    new_k_p = jnp.stack([r[2] for r in new_p], axis=0)
    new_v_p = jnp.stack([r[3] for r in new_p], axis=0)
    new_ik_p = jnp.stack([r[4] for r in new_p], axis=0)
    new_conv_p = jnp.stack([r[5] for r in new_p], axis=0)
    new_ckv_s = jnp.stack([r[0] for r in new_s], axis=0)
    new_kpe_s = jnp.stack([r[1] for r in new_s], axis=0)
    new_k_s = jnp.stack([r[2] for r in new_s], axis=0)
    new_v_s = jnp.stack([r[3] for r in new_s], axis=0)
    new_ik_s = jnp.stack([r[4] for r in new_s], axis=0)
    new_conv_s = jnp.stack([r[5] for r in new_s], axis=0)
    return (y_prompt, y_sample, new_ckv_p, new_kpe_p, new_k_p, new_v_p, new_ik_p, new_conv_p,
            new_ckv_s, new_kpe_s, new_k_s, new_v_s, new_ik_s, new_conv_s)
```

```python
import functools

import numpy as np
import jax
import jax.numpy as jnp
from jax import lax
from jax.experimental import pallas as pl
from jax.experimental.pallas import tpu as pltpu

N_META = 16
MLA_HEADS = 8
MLA_NOPE = 64
MLA_ROPE = 32
MLA_V = 64
MLA_KV_LORA = 256
MLA_Q_LORA = 768
MLA_SCALE = (MLA_NOPE + MLA_ROPE) ** -0.5
DSA_HEADS = 8
DSA_HEAD_DIM = 64
DSA_SCALE = DSA_HEAD_DIM ** -0.5
IDX_HEADS = 8
IDX_DIM = 64
IDX_SCALE = IDX_DIM ** -0.5
TOPK_MAX = 256
CONV_W = 3
ROPE_THETA = 10000.0
EPS = 1e-6
NEG_INF = -1e30
PAGE_SIZE = 128

LANES = 128
SUBLANES = 8
HD = 512
VMEM_LIMIT = 56 * 1024 * 1024

F32 = jnp.float32
BF16 = jnp.bfloat16

_C_Q, _C_KV, _Q_B, _K_B, _V_B, _Q_I, _G_A, _G_B, _SMALL, _D_IN_P = (
    0, 768, 1024, 1536, 2048, 2560, 3072, 4096, 5120, 5248)
_L_KI, _L_KPE, _L_WI = 0, 64, 96


def _dot(a, b):
    return jnp.dot(a, b, preferred_element_type=F32)


def _dot_t(a, b):
    return lax.dot_general(a, b, (((1,), (1,)), ((), ())), preferred_element_type=F32)


def _rms(x, g):
    ms = jnp.mean(x * x, axis=-1, keepdims=True)
    return x * lax.rsqrt(ms + EPS) * g


def _rope_tables(pos):
    pos = np.asarray(pos, np.float64)[:, None]
    lane = np.arange(LANES)
    out = []
    for width in (64, 32):
        half = width // 2
        m = lane % width
        ang = pos * (ROPE_THETA ** (-(m % half) / half))[None, :]
        c, s = np.cos(ang), np.sin(ang)
        out += [c, np.where(m >= half, s, 0.0), np.where(m < half, -s, 0.0)]
    return [jnp.asarray(t, F32) for t in out]


def _rope_blk(x, c, sa, sb, half):
    return x * c + pltpu.roll(x, half, 1) * sa + pltpu.roll(x, LANES - half, 1) * sb


def _proj_kernel(sample, x_ref, c64_ref, sa64_ref, sb64_ref, c32_ref, sa32_ref, sb32_ref,
                 g1_ref, gq_ref, gkv_ref, win_ref, wuq_ref, wkv_ref, *outs):
    if sample:
        (ckv_o, kb_o, vb_o, small_o, qlat_o, qp_o, qb_o, qi_o, gate_o) = outs
    else:
        (ckv_o, kb_o, vb_o, small_o, qn_o, qp_o, qb_o, qi_o, gate_o,
         kn_o, vm_o, kbb_o, vbb_o, ki2_o, kpe4_o) = outs
    hb = _rms(x_ref[0], g1_ref[...]).astype(BF16)
    c64, sa64, sb64 = c64_ref[...], sa64_ref[...], sb64_ref[...]
    c32, sa32, sb32 = c32_ref[...], sa32_ref[...], sb32_ref[...]

    def proj(a, b):
        return _dot(hb, win_ref[:, a:b])

    cq = _rms(proj(_C_Q, _C_KV), gq_ref[...]).astype(BF16)
    q = _dot(cq, wuq_ref[...])
    qn = (q[:, :HD] * MLA_SCALE).astype(BF16)
    for blk in range(2):
        qpe = q[:, HD + blk * LANES:HD + (blk + 1) * LANES]
        qp_o[0, :, blk * LANES:(blk + 1) * LANES] = (
            _rope_blk(qpe, c32, sa32, sb32, 16) * MLA_SCALE).astype(BF16)

    ckv = _rms(proj(_C_KV, _Q_B), gkv_ref[...])
    ckv_o[0] = ckv
    if sample:
        qlat_o[0] = _dot(qn, wkv_ref[...]).astype(BF16)
    else:
        qn_o[0] = qn
        kv = _dot(ckv.astype(BF16), wkv_ref[...])
        kn_o[0] = kv[:, :HD].astype(BF16)
        vm_o[0] = kv[:, HD:].astype(BF16)

    zq = proj(_Q_B, _K_B)
    zk = proj(_K_B, _V_B)
    zi = proj(_Q_I, _G_A)
    for blk in range(HD // LANES):
        sl = slice(blk * LANES, (blk + 1) * LANES)
        qb_o[0, :, sl] = (_rope_blk(zq[:, sl], c64, sa64, sb64, 32) * DSA_SCALE).astype(BF16)
        qi_o[0, :, sl] = (_rope_blk(zi[:, sl], c64, sa64, sb64, 32) * IDX_SCALE).astype(BF16)
        kr = _rope_blk(zk[:, sl], c64, sa64, sb64, 32)
        kb_o[0, :, sl] = kr
        if not sample:
            kbb_o[0, :, sl] = kr.astype(BF16)
    zv = proj(_V_B, _Q_I)
    vb_o[0] = zv
    if not sample:
        vbb_o[0] = zv.astype(BF16)

    gate_o[0] = jax.nn.sigmoid(proj(_G_A, _SMALL)).astype(BF16)

    zs = proj(_SMALL, _D_IN_P)
    lane = lax.broadcasted_iota(jnp.int32, (1, LANES), 1)
    m_ki = (lane < _L_KPE).astype(F32)
    m_kpe = ((lane >= _L_KPE) & (lane < _L_WI)).astype(F32)
    m_wi = ((lane >= _L_WI) & (lane < _L_WI + IDX_HEADS)).astype(F32)
    small = (zs * (c64 * m_ki + c32 * m_kpe + (IDX_HEADS ** -0.5) * m_wi)
             + pltpu.roll(zs, 32, 1) * (sa64 * m_ki) + pltpu.roll(zs, 96, 1) * (sb64 * m_ki)
             + pltpu.roll(zs, 16, 1) * (sa32 * m_kpe) + pltpu.roll(zs, 112, 1) * (sb32 * m_kpe))
    small_o[0] = small
    if not sample:
        ki = small * m_ki
        ki2_o[0] = (ki + pltpu.roll(ki, 64, 1)).astype(BF16)
        kp = pltpu.roll(small * m_kpe, 64, 1)
        kpe4_o[0] = (kp + pltpu.roll(kp, 32, 1) + pltpu.roll(kp, 64, 1)
                     + pltpu.roll(kp, 96, 1)).astype(BF16)


def _const_spec(shape):
    nd = len(shape)
    return pl.BlockSpec(shape, lambda *_: (0,) * nd, pipeline_mode=pl.Buffered(1))


def _proj_call(x, tables, g1, gq, gkv, win, wuq, wkv, tm, sample):
    nb, lp, d = x.shape
    grid = (nb, lp // tm)
    row = lambda w: pl.BlockSpec((1, tm, w), lambda b, j: (b, j, 0))
    tab = pl.BlockSpec((tm, LANES), lambda b, j: (j, 0))
    in_specs = ([row(d)] + [tab] * 6
                + [_const_spec(a.shape) for a in (g1, gq, gkv, win, wuq, wkv)])
    f32_w = [MLA_KV_LORA, HD, HD, LANES]
    if sample:
        bf_w = [MLA_HEADS * MLA_KV_LORA, 2 * LANES, HD, HD, 2048]
    else:
        bf_w = [HD, 2 * LANES, HD, HD, 2048, HD, HD, HD, HD, LANES, LANES]
    out_shape = ([jax.ShapeDtypeStruct((nb, lp, w), F32) for w in f32_w]
                 + [jax.ShapeDtypeStruct((nb, lp, w), BF16) for w in bf_w])
    out_specs = [row(w) for w in f32_w + bf_w]
    return pl.pallas_call(
        functools.partial(_proj_kernel, sample),
        grid=grid, in_specs=in_specs, out_specs=out_specs, out_shape=out_shape,
        compiler_params=pltpu.CompilerParams(
            dimension_semantics=("arbitrary", "arbitrary"), vmem_limit_bytes=VMEM_LIMIT),
        name="proj_sample" if sample else "proj_prompt",
    )(x, *tables, g1, gq, gkv, win, wuq, wkv)


_MAX_BISECT = 320


def _chunk(ref, c, tk):
    return ref[:, pl.ds(pl.multiple_of(c * tk, tk), tk)]


def _lane_fold(v, op):
    part = v[:, :LANES]
    for i in range(1, v.shape[1] // LANES):
        part = op(part, v[:, i * LANES:(i + 1) * LANES])
    return part


def _count(sc_ref, nck, tk, pred):
    rows = sc_ref.shape[0]

    def body(c, acc):
        v = jnp.where(pred(_chunk(sc_ref, c, tk), c), 1.0, 0.0)
        return acc + _lane_fold(v, jnp.add)

    acc = lax.fori_loop(0, nck, body, jnp.zeros((rows, LANES), F32))
    return jnp.sum(acc, axis=1, keepdims=True)


def _select_threshold(sc_ref, nck, tk, k_eff, n_valid):
    rows = sc_ref.shape[0]
    big = -NEG_INF

    def mm_body(c, carry):
        mn, mx = carry
        x = _chunk(sc_ref, c, tk)
        xv = jnp.where(x > 0.5 * NEG_INF, x, big)
        return (jnp.minimum(mn, _lane_fold(xv, jnp.minimum)),
                jnp.maximum(mx, _lane_fold(x, jnp.maximum)))

    mn, mx = lax.fori_loop(0, nck, mm_body, (jnp.full((rows, LANES), big, F32),
                                             jnp.full((rows, LANES), NEG_INF, F32)))
    lo0 = jnp.min(mn, axis=1, keepdims=True)
    hi0 = jnp.max(mx, axis=1, keepdims=True)

    def count_ge(thr):
        return _count(sc_ref, nck, tk, lambda x, c: x >= thr)

    def all_rows(fin):
        return jnp.min(jnp.where(fin, 1.0, 0.0)) > 0.5

    def cond(st):
        it, _, _, _, done = st
        return jnp.logical_and(it < _MAX_BISECT, jnp.logical_not(done))

    def body(st):
        it, lo, hi, c_lo, _ = st
        mid = 0.5 * lo + 0.5 * hi
        c = count_ge(mid)
        ge = c >= k_eff
        lo2 = jnp.where(ge, mid, lo)
        hi2 = jnp.where(ge, hi, mid)
        c_lo2 = jnp.where(ge, c, c_lo)
        fin = (c_lo2 == k_eff) | (mid <= lo) | (mid >= hi)
        return it + 1, lo2, hi2, c_lo2, all_rows(fin)

    fin0 = (n_valid == k_eff) | (lo0 >= hi0)
    _, lo, hi, _, _ = lax.while_loop(
        cond, body, (jnp.int32(0), lo0, hi0, n_valid, all_rows(fin0)))

    c_hi = count_ge(hi)
    t = jnp.where(c_hi >= k_eff, hi, lo)
    c_gt = _count(sc_ref, nck, tk, lambda x, c: x > t)
    c_get = count_ge(t)
    need = k_eff - c_gt
    tie_rows = (c_get - c_gt) > need

    @pl.when(jnp.max(jnp.where(tie_rows, 1.0, 0.0)) > 0.5)
    def _():
        lane = lax.broadcasted_iota(jnp.int32, (1, tk), 1).astype(F32)
        width = sc_ref.shape[1]

        def tie_count(m):
            return _count(sc_ref, nck, tk,
                          lambda x, c: (x == t) & ((c * tk).astype(F32) + lane <= m))

        def ibody(_, st):
            lo_i, hi_i = st
            mid = jnp.floor(0.5 * (lo_i + hi_i))
            ok = tie_count(mid) >= need
            return jnp.where(ok, lo_i, mid), jnp.where(ok, mid, hi_i)

        steps = int(np.ceil(np.log2(width))) + 1
        _, m_idx = lax.fori_loop(0, steps, ibody,
                                 (jnp.full((rows, 1), -1.0, F32),
                                  jnp.full((rows, 1), float(width - 1), F32)))

        def fix(c, carry):
            x = _chunk(sc_ref, c, tk)
            drop = (x == t) & ((c * tk).astype(F32) + lane > m_idx) & tie_rows
            sc_ref[:, pl.ds(pl.multiple_of(c * tk, tk), tk)] = jnp.where(drop, NEG_INF, x)
            return carry

        lax.fori_loop(0, nck, fix, 0)

    return t


def _softmax_step(carry, s, v):
    m, l, acc = carry
    m_new = jnp.maximum(m, jnp.max(s, axis=1, keepdims=True))
    alpha = jnp.exp(m - m_new)
    p = jnp.exp(s - m_new)
    l_new = alpha * l + jnp.sum(p, axis=1, keepdims=True)
    acc_new = alpha * acc + _dot(p.astype(BF16), v)
    return m_new, l_new, acc_new


def _attn_kernel(tq, tk, n_keep, topk,
                 qn_ref, qp_ref, qb_ref, qi_ref, small_ref,
                 kn_ref, vm_ref, kpe4_ref, kb_ref, vb_ref, ki2_ref,
                 mla_ref, dsa_ref, sc_ref):
    j = pl.program_id(1)

    @pl.when(j >= n_keep)
    def _():
        mla_ref[...] = jnp.zeros_like(mla_ref)
        dsa_ref[...] = jnp.zeros_like(dsa_ref)

    @pl.when(j < n_keep)
    def _():
        nck = (j * tq + tq - 1) // tk + 1
        last = nck - 1
        lane = lax.broadcasted_iota(jnp.int32, (1, LANES), 1)
        q_pos = j * tq + lax.broadcasted_iota(jnp.int32, (tq, 1), 0)
        k_lane = lax.broadcasted_iota(jnp.int32, (1, tk), 1)
        zero_b = jnp.zeros((tq, LANES), BF16)

        def causal(c):
            return (c * tk + k_lane) <= q_pos

        def init():
            return (jnp.full((tq, 1), NEG_INF, F32), jnp.zeros((tq, 1), F32),
                    jnp.zeros((tq, LANES), F32))

        def ksl(c):
            return pl.ds(pl.multiple_of(c * tk, tk), tk)

        def finish(carry, sub):
            _, l, acc = carry
            return acc / l, sub

        def merge_pair(o0, o1):
            return jnp.where(lane < 64, o0, o1).astype(BF16)

        for pair in range(MLA_HEADS // 2):
            psl = slice(pair * LANES, (pair + 1) * LANES)
            outs = []
            for sub in range(2):
                h = 2 * pair + sub
                grp, gsub = divmod(h, 4)
                qm = jnp.where((lane >= 64) == bool(sub), qn_ref[0, :, psl], zero_b)
                qpm = jnp.where((lane // 32) == gsub,
                                qp_ref[0, :, grp * LANES:(grp + 1) * LANES], zero_b)

                def scores(c, qm=qm, qpm=qpm, psl=psl):
                    return (_dot_t(qm, kn_ref[0, ksl(c), psl])
                            + _dot_t(qpm, kpe4_ref[0, ksl(c), :]))

                def step(c, carry, scores=scores, psl=psl):
                    return _softmax_step(carry, scores(c), vm_ref[0, ksl(c), psl])

                carry = lax.fori_loop(0, last, step, init())
                s = jnp.where(causal(last), scores(last), NEG_INF)
                carry = _softmax_step(carry, s, vm_ref[0, ksl(last), psl])
                outs.append(finish(carry, sub)[0])
            mla_ref[0, :, psl] = merge_pair(*outs)

        w_all = small_ref[0]
        w_b = [jnp.broadcast_to(w_all[:, _L_WI + h:_L_WI + h + 1], (tq, tk))
               for h in range(IDX_HEADS)]

        def idx_scores(c):
            kc = ki2_ref[0, ksl(c), :]
            acc = jnp.zeros((tq, tk), F32)
            for h in range(IDX_HEADS):
                pair, sub = divmod(h, 2)
                qm = jnp.where((lane >= 64) == bool(sub),
                               qi_ref[0, :, pair * LANES:(pair + 1) * LANES], zero_b)
                acc = acc + jnp.maximum(_dot_t(qm, kc), 0.0) * w_b[h]
            return acc

        def idx_step(c, carry):
            sc_ref[:, ksl(c)] = idx_scores(c)
            return carry

        lax.fori_loop(0, last, idx_step, 0)
        sc_ref[:, ksl(last)] = jnp.where(causal(last), idx_scores(last), NEG_INF)

        n_valid = (q_pos + 1).astype(F32)
        k_eff = jnp.minimum(n_valid, float(topk))
        t = _select_threshold(sc_ref, nck, tk, k_eff, n_valid)

        for pair in range(DSA_HEADS // 2):
            psl = slice(pair * LANES, (pair + 1) * LANES)
            outs = []
            for sub in range(2):
                qm = jnp.where((lane >= 64) == bool(sub), qb_ref[0, :, psl], zero_b)

                def step(c, carry, qm=qm, psl=psl):
                    s = _dot_t(qm, kb_ref[0, ksl(c), psl])
                    s = jnp.where(sc_ref[:, ksl(c)] >= t, s, NEG_INF)
                    return _softmax_step(carry, s, vb_ref[0, ksl(c), psl])

                carry = lax.fori_loop(0, nck, step, init())
                outs.append(finish(carry, sub)[0])
            dsa_ref[0, :, psl] = merge_pair(*outs)


def _attn_call(qn, qp, qb, qi, small, kn, vm, kpe4, kbb, vbb, ki2, l_valid, tq, tk):
    nb, lp, _ = qn.shape
    n_keep = -(-l_valid // tq)
    topk = min(TOPK_MAX, l_valid // 4)
    qrow = lambda w: pl.BlockSpec((1, tq, w), lambda b, j: (b, j, 0))
    krow = lambda w: pl.BlockSpec((1, lp, w), lambda b, j: (b, 0, 0))
    in_specs = [qrow(HD), qrow(2 * LANES), qrow(HD), qrow(HD), qrow(LANES),
                krow(HD), krow(HD), krow(LANES), krow(HD), krow(HD), krow(LANES)]
    return pl.pallas_call(
        functools.partial(_attn_kernel, tq, tk, n_keep, topk),
        grid=(nb, lp // tq), in_specs=in_specs,
        out_specs=[qrow(HD), qrow(HD)],
        out_shape=[jax.ShapeDtypeStruct((nb, lp, HD), BF16)] * 2,
        scratch_shapes=[pltpu.VMEM((tq, lp), F32)],
        compiler_params=pltpu.CompilerParams(
            dimension_semantics=("arbitrary", "arbitrary"), vmem_limit_bytes=VMEM_LIMIT),
        name="attn_prompt",
    )(qn, qp, qb, qi, small, kn, vm, kpe4, kbb, vbb, ki2)


def _merge_kernel(x_ref, mla_ref, dsa_ref, gate_ref, wa_ref, wb_ref, wo_ref, g2_ref,
                  x2_ref, h2_ref):
    d = x_ref.shape[-1]
    a = _dot(mla_ref[0], wa_ref[...])
    b = _dot(dsa_ref[0], wb_ref[...])
    g = gate_ref[0]
    o = g[:, :d].astype(F32) * a + g[:, d:].astype(F32) * b
    x2 = x_ref[0] + _dot(o.astype(BF16), wo_ref[...])
    x2_ref[0] = x2
    h2_ref[0] = _rms(x2, g2_ref[...]).astype(BF16)


def _merge_call(x, mla, dsa, gate, wa, wb, wo, g2, tm, name):
    nb, lp, d = x.shape
    row = lambda w: pl.BlockSpec((1, tm, w), lambda b, j: (b, j, 0))
    return pl.pallas_call(
        _merge_kernel, grid=(nb, lp // tm),
        in_specs=[row(d), row(HD), row(HD), row(2 * d)]
        + [_const_spec(a.shape) for a in (wa, wb, wo, g2)],
        out_specs=[row(d), row(d)],
        out_shape=[jax.ShapeDtypeStruct((nb, lp, d), F32), jax.ShapeDtypeStruct((nb, lp, d), BF16)],
        compiler_params=pltpu.CompilerParams(
            dimension_semantics=("arbitrary", "arbitrary"), vmem_limit_bytes=VMEM_LIMIT),
        name=name,
    )(x, mla, dsa, gate, wa, wb, wo, g2)


_FF_CHUNK = 256


def _ffn_kernel(seq_rows, tail_tile, tail_off, h_ref, x_ref, s1_ref, s2_ref,
                wg_ref, wu_ref, wd_ref, cw_ref, cb_ref, gf_ref, y_ref, tail_ref, prev_ref):
    j = pl.program_id(1)
    tm = h_ref.shape[1]
    d_ff = wg_ref.shape[1]
    h = h_ref[0]
    row = lax.broadcasted_iota(jnp.int32, (tm, 1), 0)
    if seq_rows is None:
        @pl.when(j == 0)
        def _():
            prev_ref[...] = jnp.zeros_like(prev_ref)
        first1, first2 = row < 1, row < 2
    else:
        first1, first2 = (row % seq_rows) < 1, (row % seq_rows) < 2

    acc = jnp.zeros((tm, x_ref.shape[-1]), F32)
    for c0 in range(0, d_ff, _FF_CHUNK):
        sl = slice(c0, c0 + _FF_CHUNK)
        g = _dot(h, wg_ref[:, sl])
        u = _dot(h, wu_ref[:, sl])
        if seq_rows is None:
            p = prev_ref[:, sl]
            hist1 = jnp.broadcast_to(p[SUBLANES - 1:SUBLANES], g.shape)
            hist2 = jnp.where(row < 1, jnp.broadcast_to(p[SUBLANES - 2:SUBLANES - 1], g.shape),
                              hist1)
            prev_ref[:, sl] = g[tm - SUBLANES:]
        else:
            hist1, hist2 = s1_ref[0, :, sl], s2_ref[0, :, sl]
        g1 = jnp.where(first1, hist1, pltpu.roll(g, 1, 0))
        g2 = jnp.where(first2, hist2, pltpu.roll(g, 2, 0))
        cw = cw_ref[:, sl]
        gc = cb_ref[:, sl] + cw[0:1] * g2 + cw[1:2] * g1 + cw[2:3] * g
        act = (gc * jax.nn.sigmoid(gc) * u).astype(BF16)
        acc = acc + _dot(act, wd_ref[sl, :])
        if seq_rows is None:
            @pl.when(j == tail_tile)
            def _(g=g, sl=sl):
                tail_ref[0, :, sl] = g[tail_off:tail_off + SUBLANES]
        else:
            tail_ref[0, :, sl] = g
    y_ref[0] = _rms(x_ref[0] + acc, gf_ref[...])


def _ffn_call(h2, x2, s1, s2, wg, wu, wd, cw, cb, gf, tm, l_valid, sample, name):
    nb, lp, d = x2.shape
    d_ff = wg.shape[1]
    row = lambda w: pl.BlockSpec((1, tm, w), lambda b, j: (b, j, 0))
    if sample:
        seq_rows, tail_tile, tail_off = l_valid, 0, 0
        tail_shape, tail_spec = (nb, lp, d_ff), row(d_ff)
        s_spec = row(d_ff)
    else:
        seq_rows = None
        tail_tile, tail_off = divmod(l_valid - SUBLANES, tm)
        tail_shape = (nb, SUBLANES, d_ff)
        tail_spec = pl.BlockSpec((1, SUBLANES, d_ff), lambda b, j: (b, 0, 0))
        s_spec = pl.BlockSpec((1, SUBLANES, d_ff), lambda b, j: (0, 0, 0))
    return pl.pallas_call(
        functools.partial(_ffn_kernel, seq_rows, tail_tile, tail_off),
        grid=(nb, lp // tm),
        in_specs=[row(d), row(d), s_spec, s_spec]
        + [_const_spec(a.shape) for a in (wg, wu, wd, cw, cb, gf)],
        out_specs=[row(d), tail_spec],
        out_shape=[jax.ShapeDtypeStruct((nb, lp, d), F32), jax.ShapeDtypeStruct(tail_shape, F32)],
        scratch_shapes=[pltpu.VMEM((SUBLANES, d_ff), F32)],
        compiler_params=pltpu.CompilerParams(
            dimension_semantics=("arbitrary", "arbitrary"), vmem_limit_bytes=VMEM_LIMIT),
        name=name,
    )(h2, x2, s1, s2, wg, wu, wd, cw, cb, gf)


def _head_rows_mask(rows_per_head, n_heads, width_per_head):
    r = lax.broadcasted_iota(jnp.int32, (n_heads * rows_per_head, 1), 0) // rows_per_head
    c = lax.broadcasted_iota(jnp.int32, (1, n_heads * width_per_head), 1) // width_per_head
    return r == c


def _diag_heads(full, t, n_heads, width):
    col_head = lax.broadcasted_iota(jnp.int32, (1, n_heads * width), 1) // width
    out = jnp.zeros((t, n_heads * width), F32)
    for h in range(n_heads):
        out = jnp.where(col_head == h, full[h * t:(h + 1) * t], out)
    return out


def _smla_kernel(npp, t_new, pt_ref, qabs_ref, qpe_ref, qi_ref, wrow_ref,
                 ckvn_ref, kpen_ref, ikn_ref, wuv_ref, *rest):
    ckv_pages = rest[:npp]
    kpe_pages = rest[npp:2 * npp]
    ik_pages = rest[2 * npp:3 * npp]
    mla_ref, scp_ref, scn_ref, m_ref, l_ref, acc_ref = rest[3 * npp:]
    c = pl.program_id(1)
    rows = qabs_ref.shape[1]
    qabs, qpe, qi, wrow = qabs_ref[0], qpe_ref[0], qi_ref[0], wrow_ref[0]

    @pl.when(c == 0)
    def _():
        m_ref[...] = jnp.full_like(m_ref, NEG_INF)
        l_ref[...] = jnp.zeros_like(l_ref)
        acc_ref[...] = jnp.zeros_like(acc_ref)

    def update(s, vals):
        m = m_ref[...]
        m_new = jnp.maximum(m, jnp.max(s, axis=1, keepdims=True))
        alpha = jnp.exp(m - m_new)
        p = jnp.exp(s - m_new)
        l_ref[...] = alpha * l_ref[...] + jnp.sum(p, axis=1, keepdims=True)
        acc_ref[...] = alpha * acc_ref[...] + _dot(p.astype(BF16), vals)
        m_ref[...] = m_new

    def idx_score(ik):
        s = jnp.maximum(_dot_t(qi, ik), 0.0) * wrow[:, :1]
        out = s[:t_new]
        for h in range(1, IDX_HEADS):
            out = out + s[h * t_new:(h + 1) * t_new]
        return out

    for i in range(npp):
        ck = ckv_pages[i][...].astype(BF16)
        s = _dot_t(qabs, ck) + _dot_t(qpe, kpe_pages[i][...].astype(BF16))
        update(s, ck)
        scp_ref[0, :, i * PAGE_SIZE:(i + 1) * PAGE_SIZE] = idx_score(ik_pages[i][...].astype(BF16))

    @pl.when(c == pl.num_programs(1) - 1)
    def _():
        ck = ckvn_ref[0]
        tok = lax.broadcasted_iota(jnp.int32, (rows, 1), 0) % t_new
        key = lax.broadcasted_iota(jnp.int32, (1, PAGE_SIZE), 1)
        vis = key <= tok
        s = _dot_t(qabs, ck) + _dot_t(qpe, kpen_ref[0])
        update(jnp.where(vis, s, NEG_INF), ck)
        scn_ref[0] = jnp.where(vis[:t_new], idx_score(ikn_ref[0]), NEG_INF)
        o_lat = (acc_ref[...] / l_ref[...]).astype(BF16)
        mla_ref[0] = _diag_heads(_dot(o_lat, wuv_ref[...]), t_new, MLA_HEADS, MLA_V).astype(BF16)


def _sdsa_kernel(npp, t_new, topk, past_len, sel_tk, pt_ref, qbd_ref, scp_ref, scn_ref, kbn_ref,
                 vbn_ref, *rest):
    k_pages = rest[:npp]
    v_pages = rest[npp:2 * npp]
    dsa_ref, sc_ref, t_ref, m_ref, l_ref, acc_ref = rest[2 * npp:]
    c = pl.program_id(1)
    rows = qbd_ref.shape[1]
    qbd = qbd_ref[0]
    step_keys = npp * PAGE_SIZE

    @pl.when(c == 0)
    def _():
        m_ref[...] = jnp.full_like(m_ref, NEG_INF)
        l_ref[...] = jnp.zeros_like(l_ref)
        acc_ref[...] = jnp.zeros_like(acc_ref)
        sc_ref[:, :past_len] = scp_ref[0]
        sc_ref[:, past_len:past_len + PAGE_SIZE] = scn_ref[0]
        if sel_tk > PAGE_SIZE:
            sc_ref[:, past_len + PAGE_SIZE:] = jnp.full((t_new, sel_tk - PAGE_SIZE), NEG_INF, F32)
        tok = lax.broadcasted_iota(jnp.int32, (t_new, 1), 0)
        n_valid = (past_len + 1 + tok).astype(F32)
        k_eff = jnp.minimum(n_valid, float(topk))
        t_ref[...] = _select_threshold(sc_ref, (past_len + sel_tk) // sel_tk, sel_tk,
                                       k_eff, n_valid)

    t = t_ref[...]

    def update(s, sel, vals):
        sel_rows = jnp.concatenate([sel] * DSA_HEADS, axis=0)
        s = jnp.where(sel_rows, s, NEG_INF)
        m = m_ref[...]
        m_new = jnp.maximum(m, jnp.max(s, axis=1, keepdims=True))
        alpha = jnp.exp(m - m_new)
        p = jnp.exp(s - m_new)
        l_ref[...] = alpha * l_ref[...] + jnp.sum(p, axis=1, keepdims=True)
        acc_ref[...] = alpha * acc_ref[...] + _dot(p.astype(BF16), vals)
        m_ref[...] = m_new

    for i in range(npp):
        kp = k_pages[i][...].astype(BF16)
        start = pl.multiple_of(c * step_keys + i * PAGE_SIZE, PAGE_SIZE)
        sel = sc_ref[:, pl.ds(start, PAGE_SIZE)] >= t
        update(_dot_t(qbd, kp), sel, v_pages[i][...].astype(BF16))

    @pl.when(c == pl.num_programs(1) - 1)
    def _():
        sel = sc_ref[:, past_len:past_len + PAGE_SIZE] >= t
        update(_dot_t(qbd, kbn_ref[0]), sel, vbn_ref[0])
        o = acc_ref[...] / l_ref[...]
        dsa_ref[0] = _diag_heads(o, t_new, DSA_HEADS, DSA_HEAD_DIM).astype(BF16)


def _pages_per_step(n_pages, want):
    p = min(want, n_pages)
    while n_pages % p:
        p -= 1
    return p


def _page_specs(npp, width):
    return [pl.BlockSpec((None, PAGE_SIZE, width),
                         functools.partial(lambda i, b, c, pt: (pt[b, c * npp + i], 0, 0), i))
            for i in range(npp)]


def _smla_call(page_table, qabs, qpe, qi, wrow, ckvn, kpen, ikn, wuv, pool_ckv, pool_kpe, pool_ik,
               t_new):
    nb, n_pages = page_table.shape
    npp = _pages_per_step(n_pages, 16)
    rows = qabs.shape[1]
    past_len = n_pages * PAGE_SIZE
    per_b = lambda shape: pl.BlockSpec((1,) + shape, lambda b, c, pt: (b,) + (0,) * len(shape))
    in_specs = ([per_b((rows, MLA_KV_LORA)), per_b((rows, MLA_ROPE)), per_b((rows, IDX_DIM)),
                 per_b((rows, LANES)), per_b((PAGE_SIZE, MLA_KV_LORA)),
                 per_b((PAGE_SIZE, MLA_ROPE)), per_b((PAGE_SIZE, IDX_DIM)),
                 pl.BlockSpec(wuv.shape, lambda b, c, pt: (0, 0))]
                + _page_specs(npp, MLA_KV_LORA) + _page_specs(npp, MLA_ROPE)
                + _page_specs(npp, IDX_DIM))
    out_specs = [per_b((t_new, HD)),
                 pl.BlockSpec((1, t_new, npp * PAGE_SIZE), lambda b, c, pt: (b, 0, c)),
                 per_b((t_new, PAGE_SIZE))]
    out_shape = [jax.ShapeDtypeStruct((nb, t_new, HD), BF16),
                 jax.ShapeDtypeStruct((nb, t_new, past_len), F32),
                 jax.ShapeDtypeStruct((nb, t_new, PAGE_SIZE), F32)]
    grid_spec = pltpu.PrefetchScalarGridSpec(
        num_scalar_prefetch=1, grid=(nb, n_pages // npp), in_specs=in_specs, out_specs=out_specs,
        scratch_shapes=[pltpu.VMEM((rows, 1), F32), pltpu.VMEM((rows, 1), F32),
                        pltpu.VMEM((rows, MLA_KV_LORA), F32)])
    return pl.pallas_call(
        functools.partial(_smla_kernel, npp, t_new), grid_spec=grid_spec, out_shape=out_shape,
        compiler_params=pltpu.CompilerParams(
            dimension_semantics=("arbitrary", "arbitrary"), vmem_limit_bytes=VMEM_LIMIT),
        name="sample_mla",
    )(page_table, qabs, qpe, qi, wrow, ckvn, kpen, ikn, wuv,
      *([pool_ckv] * npp), *([pool_kpe] * npp), *([pool_ik] * npp))


def _sdsa_call(page_table, qbd, scp, scn, kbn, vbn, pool_k, pool_v, t_new):
    nb, n_pages = page_table.shape
    npp = _pages_per_step(n_pages, 8)
    rows = qbd.shape[1]
    past_len = n_pages * PAGE_SIZE
    topk = min(TOPK_MAX, (past_len + t_new) // 4)
    sel_tk = PAGE_SIZE
    while sel_tk < 2048 and past_len % (2 * sel_tk) == 0:
        sel_tk *= 2
    per_b = lambda shape: pl.BlockSpec((1,) + shape, lambda b, c, pt: (b,) + (0,) * len(shape))
    in_specs = ([per_b((rows, HD)), per_b((t_new, past_len)), per_b((t_new, PAGE_SIZE)),
                 per_b((PAGE_SIZE, HD)), per_b((PAGE_SIZE, HD))]
                + _page_specs(npp, HD) + _page_specs(npp, HD))
    grid_spec = pltpu.PrefetchScalarGridSpec(
        num_scalar_prefetch=1, grid=(nb, n_pages // npp), in_specs=in_specs,
        out_specs=[per_b((t_new, HD))],
        scratch_shapes=[pltpu.VMEM((t_new, past_len + sel_tk), F32),
                        pltpu.VMEM((t_new, 1), F32),
                        pltpu.VMEM((rows, 1), F32), pltpu.VMEM((rows, 1), F32),
                        pltpu.VMEM((rows, HD), F32)])
    return pl.pallas_call(
        functools.partial(_sdsa_kernel, npp, t_new, topk, past_len, sel_tk), grid_spec=grid_spec,
        out_shape=[jax.ShapeDtypeStruct((nb, t_new, HD), BF16)],
        compiler_params=pltpu.CompilerParams(
            dimension_semantics=("arbitrary", "arbitrary"), vmem_limit_bytes=VMEM_LIMIT),
        name="sample_dsa",
    )(page_table, qbd, scp, scn, kbn, vbn, *([pool_k] * npp), *([pool_v] * npp))[0]


def _pick_tile(n, candidates):
    for c in candidates:
        if n % c == 0:
            return c
    return n


def _pad_rows(a, rows):
    return jnp.pad(a, ((0, 0), (0, rows - a.shape[1]), (0, 0)))


def _head_major(a, nb, t, heads):
    w = a.shape[-1] // heads
    return a.reshape(nb, t, heads, w).transpose(0, 2, 1, 3).reshape(nb, heads * t, w)


def kernel(x_prompt, x_sample, cache_mla_ckv, cache_mla_kpe, cache_dsa_k, cache_dsa_v, cache_idx_k,
           state_ffn_conv, page_table, meta_tokens, norm1_g, w_in, g_q, g_kv, w_uq, w_uk, w_uv,
           w_br_a, w_br_b, w_o, norm2_g, w_ffn_g, w_ffn_u, ffn_conv_w, ffn_conv_b, w_ffn_d, final_g):
    depth = w_in.shape[0]
    assert depth == 1
    nb, seq, d = x_prompt.shape
    nsb, t_new, _ = x_sample.shape
    n_pages = page_table.shape[1]
    past_len = n_pages * PAGE_SIZE
    l_valid = N_META + seq
    tq, tk = 128, 256
    lp = -(-l_valid // tk) * tk
    assert l_valid % SUBLANES == 0 and t_new % SUBLANES == 0 and t_new <= PAGE_SIZE
    d_ff = w_ffn_g.shape[-1]
    l = 0

    wi = w_in[l]
    cuts = np.cumsum([MLA_Q_LORA, MLA_KV_LORA, MLA_ROPE, HD, HD, HD, HD, IDX_DIM, IDX_HEADS, d, d])
    c_q, c_kv, k_pe, q_b, k_b, v_b, q_i, k_i, w_i, g_a, g_b = jnp.split(wi, cuts[:-1], axis=1)
    pad = jnp.zeros((d, LANES - IDX_DIM - MLA_ROPE - IDX_HEADS), wi.dtype)
    win = jnp.concatenate([c_q, c_kv, q_b, k_b, v_b, q_i, g_a, g_b, k_i, k_pe, w_i, pad],
                          axis=1).astype(BF16)
    wuq = jnp.concatenate([w_uq[l][:, :, :MLA_NOPE].reshape(MLA_Q_LORA, -1),
                           w_uq[l][:, :, MLA_NOPE:].reshape(MLA_Q_LORA, -1)], axis=1).astype(BF16)
    wuk2 = w_uk[l].reshape(MLA_KV_LORA, HD)
    wuv2 = w_uv[l].reshape(MLA_KV_LORA, HD)
    wkv_p = jnp.concatenate([wuk2, wuv2], axis=1).astype(BF16)
    ukt = w_uk[l].transpose(1, 2, 0)
    eye = jnp.eye(MLA_HEADS, dtype=ukt.dtype)
    wuk_bd = (ukt[:, :, None, :] * eye[:, None, :, None]).reshape(HD, MLA_HEADS * MLA_KV_LORA)
    wuk_bd = wuk_bd.astype(BF16)
    g1 = norm1_g[l][None]
    gq = g_q[l][None]
    gkv = g_kv[l][None]
    g2 = norm2_g[l][None]
    gf = final_g[None]
    wa, wb, wo = w_br_a[l].astype(BF16), w_br_b[l].astype(BF16), w_o[l].astype(BF16)
    wg, wu, wd = w_ffn_g[l].astype(BF16), w_ffn_u[l].astype(BF16), w_ffn_d[l].astype(BF16)
    cw, cb = ffn_conv_w[l], ffn_conv_b[l][None]

    meta = jnp.broadcast_to(meta_tokens[None].astype(x_prompt.dtype), (nb, N_META, d))
    xp = jnp.concatenate([meta, x_prompt, jnp.zeros((nb, lp - l_valid, d), x_prompt.dtype)], axis=1)
    tm = _pick_tile(lp, (384, 256, 128))
    tabs_p = _rope_tables(np.arange(lp))
    (ckv_p, kb_p, vb_p, small_p, qn, qp, qb, qi, gate_p, kn, vm, kbb, vbb, ki2, kpe4) = _proj_call(
        xp, tabs_p, g1, gq, gkv, win, wuq, wkv_p, tm, sample=False)
    mla_p, dsa_p = _attn_call(qn, qp, qb, qi, small_p, kn, vm, kpe4, kbb, vbb, ki2, l_valid, tq, tk)
    x2_p, h2_p = _merge_call(xp, mla_p, dsa_p, gate_p, wa, wb, wo, g2, tm, "merge_prompt")
    zstate = jnp.zeros((1, SUBLANES, d_ff), F32)
    y_p, tail_p = _ffn_call(h2_p, x2_p, zstate, zstate, wg, wu, wd, cw, cb, gf, tm, l_valid,
                            False, "ffn_prompt")

    ns = nsb * t_new
    xs = x_sample.reshape(1, ns, d)
    tabs_s = _rope_tables(past_len + (np.arange(ns) % t_new))
    (ckv_s, kb_s, vb_s, small_s, qlat, qp_s, qb_s, qi_s, gate_s) = _proj_call(
        xs, tabs_s, g1, gq, gkv, win, wuq, wuk_bd, ns, sample=True)
    qabs = _head_major(qlat, nsb, t_new, MLA_HEADS)
    qpe_r = _head_major(qp_s, nsb, t_new, MLA_HEADS)
    qi_r = _head_major(qi_s, nsb, t_new, IDX_HEADS)
    w_rows = small_s[0, :, _L_WI:_L_WI + IDX_HEADS].reshape(nsb, t_new, IDX_HEADS)
    w_rows = jnp.broadcast_to(w_rows.transpose(0, 2, 1).reshape(nsb, IDX_HEADS * t_new, 1),
                              (nsb, IDX_HEADS * t_new, LANES))
    qb_r = _head_major(qb_s, nsb, t_new, DSA_HEADS)
    qbd = jnp.where(_np_head_mask(t_new), jnp.tile(qb_r, (1, 1, DSA_HEADS)), jnp.zeros((), BF16))
    new_rows = lambda a: _pad_rows(a.reshape(nsb, t_new, -1), PAGE_SIZE).astype(BF16)
    ckvn = new_rows(ckv_s)
    kpen = new_rows(small_s[..., _L_KPE:_L_KPE + MLA_ROPE])
    ikn = new_rows(small_s[..., _L_KI:_L_KI + IDX_DIM])
    kbn, vbn = new_rows(kb_s), new_rows(vb_s)
    mla_s, scp, scn = _smla_call(page_table, qabs, qpe_r, qi_r, w_rows, ckvn, kpen, ikn,
                                 wuv2.astype(BF16), cache_mla_ckv[l], cache_mla_kpe[l],
                                 cache_idx_k[l], t_new)
    n_pool = cache_dsa_k.shape[1]
    dsa_s = _sdsa_call(page_table, qbd, scp, scn, kbn, vbn,
                       cache_dsa_k[l].reshape(n_pool, PAGE_SIZE, HD),
                       cache_dsa_v[l].reshape(n_pool, PAGE_SIZE, HD), t_new)
    x2_s, h2_s = _merge_call(xs, mla_s.reshape(1, ns, HD), dsa_s.reshape(1, ns, HD), gate_s,
                             wa, wb, wo, g2, ns, "merge_sample")
    st = state_ffn_conv[l]
    zrow = jnp.zeros((nsb, 1, d_ff), st.dtype)
    s1 = jnp.concatenate([st[:, 1:2]] + [zrow] * (t_new - 1), axis=1).reshape(1, ns, d_ff)
    s2 = jnp.concatenate([st[:, 0:1], st[:, 1:2]] + [zrow] * (t_new - 2), axis=1).reshape(1, ns, d_ff)
    y_s, tail_s = _ffn_call(h2_s, x2_s, s1, s2, wg, wu, wd, cw, cb, gf, ns, t_new, True,
                            "ffn_sample")

    y_prompt = y_p[:, N_META:l_valid]
    y_sample = y_s.reshape(nsb, t_new, d)
    cut = lambda a: a[:, :l_valid]
    new_ckv_p = cut(ckv_p)[None]
    new_kpe_p = cut(small_p)[..., _L_KPE:_L_KPE + MLA_ROPE][None]
    new_k_p = cut(kb_p).reshape(1, nb, l_valid, DSA_HEADS, DSA_HEAD_DIM)
    new_v_p = cut(vb_p).reshape(1, nb, l_valid, DSA_HEADS, DSA_HEAD_DIM)
    new_ik_p = cut(small_p)[..., _L_KI:_L_KI + IDX_DIM][None]
    new_conv_p = tail_p[:, SUBLANES - (CONV_W - 1):][None]
    per_s = lambda a: a.reshape(nsb, t_new, -1)
    new_ckv_s = per_s(ckv_s)[None]
    new_kpe_s = per_s(small_s)[..., _L_KPE:_L_KPE + MLA_ROPE][None]
    new_k_s = per_s(kb_s).reshape(1, nsb, t_new, DSA_HEADS, DSA_HEAD_DIM)
    new_v_s = per_s(vb_s).reshape(1, nsb, t_new, DSA_HEADS, DSA_HEAD_DIM)
    new_ik_s = per_s(small_s)[..., _L_KI:_L_KI + IDX_DIM][None]
    new_conv_s = per_s(tail_s)[:, t_new - (CONV_W - 1):][None]
    return (y_prompt, y_sample, new_ckv_p, new_kpe_p, new_k_p, new_v_p, new_ik_p, new_conv_p,
            new_ckv_s, new_kpe_s, new_k_s, new_v_s, new_ik_s, new_conv_s)


def _np_head_mask(t_new):
    r = np.arange(DSA_HEADS * t_new)[:, None] // t_new
    c = np.arange(HD)[None, :] // DSA_HEAD_DIM
    return jnp.asarray(r == c)[None]
```

```python
import functools

import numpy as np
import jax
import jax.numpy as jnp
from jax import lax
from jax.experimental import pallas as pl
from jax.experimental.pallas import tpu as pltpu

N_META = 16
MLA_HEADS = 8
MLA_NOPE = 64
MLA_ROPE = 32
MLA_V = 64
MLA_KV_LORA = 256
MLA_Q_LORA = 768
MLA_SCALE = (MLA_NOPE + MLA_ROPE) ** -0.5
DSA_HEADS = 8
DSA_HEAD_DIM = 64
DSA_SCALE = DSA_HEAD_DIM ** -0.5
IDX_HEADS = 8
IDX_DIM = 64
IDX_SCALE = IDX_DIM ** -0.5
TOPK_MAX = 256
CONV_W = 3
ROPE_THETA = 10000.0
EPS = 1e-6
NEG_INF = -1e30
PAGE_SIZE = 128

LANES = 128
SUBLANES = 8
HD = 512
VMEM_LIMIT = 56 * 1024 * 1024

F32 = jnp.float32
BF16 = jnp.bfloat16

_C_Q, _C_KV, _Q_B, _K_B, _V_B, _Q_I, _G_A, _G_B, _SMALL, _D_IN_P = (
    0, 768, 1024, 1536, 2048, 2560, 3072, 4096, 5120, 5248)
_L_KI, _L_KPE, _L_WI = 0, 64, 96


def _dot(a, b):
    return jnp.dot(a, b, preferred_element_type=F32)


def _dot_t(a, b):
    return lax.dot_general(a, b, (((1,), (1,)), ((), ())), preferred_element_type=F32)


def _rms(x, g):
    ms = jnp.mean(x * x, axis=-1, keepdims=True)
    return x * lax.rsqrt(ms + EPS) * g


def _rope_tables(pos):
    pos = jnp.asarray(pos).astype(F32)[:, None]
    lane = np.arange(LANES)
    out = []
    for width in (64, 32):
        half = width // 2
        m = lane % width
        inv_freq = 1.0 / (ROPE_THETA ** (jnp.arange(half, dtype=F32) / half))
        ang = pos * inv_freq[m % half][None, :]
        c, s = jnp.cos(ang), jnp.sin(ang)
        out += [c, jnp.where(m >= half, s, 0.0), jnp.where(m < half, -s, 0.0)]
    return out


def _rope_blk(x, c, sa, sb, half):
    return x * c + pltpu.roll(x, half, 1) * sa + pltpu.roll(x, LANES - half, 1) * sb


def _proj_kernel(sample, x_ref, c64_ref, sa64_ref, sb64_ref, c32_ref, sa32_ref, sb32_ref,
                 g1_ref, gq_ref, gkv_ref, win_ref, wuq_ref, wkv_ref, *outs):
    if sample:
        (ckv_o, kb_o, vb_o, small_o, qlat_o, qp_o, qb_o, qi_o, gate_o) = outs
    else:
        (ckv_o, kb_o, vb_o, small_o, qn_o, qp_o, qb_o, qi_o, gate_o,
         kcat_o, vm_o, kbb_o, vbb_o, ki2_o) = outs
    hb = _rms(x_ref[0], g1_ref[...]).astype(BF16)
    c64, sa64, sb64 = c64_ref[...], sa64_ref[...], sb64_ref[...]
    c32, sa32, sb32 = c32_ref[...], sa32_ref[...], sb32_ref[...]

    def proj(a, b):
        return _dot(hb, win_ref[:, a:b])

    cq = _rms(proj(_C_Q, _C_KV), gq_ref[...]).astype(BF16)
    q = _dot(cq, wuq_ref[...])
    qn = (q[:, :HD] * MLA_SCALE).astype(BF16)
    for blk in range(2):
        qpe = q[:, HD + blk * LANES:HD + (blk + 1) * LANES]
        qp_o[0, :, blk * LANES:(blk + 1) * LANES] = (
            _rope_blk(qpe, c32, sa32, sb32, 16) * MLA_SCALE).astype(BF16)

    ckv = _rms(proj(_C_KV, _Q_B), gkv_ref[...])
    ckv_o[0] = ckv
    if sample:
        qlat_o[0] = _dot(qn, wkv_ref[...]).astype(BF16)
    else:
        qn_o[0] = qn
        kv = _dot(ckv.astype(BF16), wkv_ref[...])
        for pair in range(MLA_HEADS // 2):
            kcat_o[0, :, 2 * pair * LANES:(2 * pair + 1) * LANES] = (
                kv[:, pair * LANES:(pair + 1) * LANES].astype(BF16))
        vm_o[0] = kv[:, HD:].astype(BF16)

    zq = proj(_Q_B, _K_B)
    zk = proj(_K_B, _V_B)
    zi = proj(_Q_I, _G_A)
    for blk in range(HD // LANES):
        sl = slice(blk * LANES, (blk + 1) * LANES)
        qb_o[0, :, sl] = (_rope_blk(zq[:, sl], c64, sa64, sb64, 32) * DSA_SCALE).astype(BF16)
        qi_o[0, :, sl] = (_rope_blk(zi[:, sl], c64, sa64, sb64, 32) * IDX_SCALE).astype(BF16)
        kr = _rope_blk(zk[:, sl], c64, sa64, sb64, 32)
        if sample:
            kb_o[0, :, sl] = kr
        else:
            kb_o[0, sl, :] = kr.T
            kbb_o[0, :, sl] = kr.astype(BF16)
    zv = proj(_V_B, _Q_I)
    if sample:
        vb_o[0] = zv
    else:
        for blk in range(HD // LANES):
            sl = slice(blk * LANES, (blk + 1) * LANES)
            vb_o[0, sl, :] = zv[:, sl].T
        vbb_o[0] = zv.astype(BF16)

    gate_o[0] = jax.nn.sigmoid(proj(_G_A, _SMALL)).astype(BF16)

    zs = proj(_SMALL, _D_IN_P)
    lane = lax.broadcasted_iota(jnp.int32, (1, LANES), 1)
    m_ki = (lane < _L_KPE).astype(F32)
    m_kpe = ((lane >= _L_KPE) & (lane < _L_WI)).astype(F32)
    m_wi = ((lane >= _L_WI) & (lane < _L_WI + IDX_HEADS)).astype(F32)
    small = (zs * (c64 * m_ki + c32 * m_kpe + (IDX_HEADS ** -0.5) * m_wi)
             + pltpu.roll(zs, 32, 1) * (sa64 * m_ki) + pltpu.roll(zs, 96, 1) * (sb64 * m_ki)
             + pltpu.roll(zs, 16, 1) * (sa32 * m_kpe) + pltpu.roll(zs, 112, 1) * (sb32 * m_kpe))
    small_o[0] = small
    if not sample:
        ki = small * m_ki
        ki2_o[0] = (ki + pltpu.roll(ki, 64, 1)).astype(BF16)
        kp = pltpu.roll(small * m_kpe, 64, 1)
        kpe4 = (kp + pltpu.roll(kp, 32, 1) + pltpu.roll(kp, 64, 1)
                + pltpu.roll(kp, 96, 1)).astype(BF16)
        for pair in range(MLA_HEADS // 2):
            kcat_o[0, :, (2 * pair + 1) * LANES:(2 * pair + 2) * LANES] = kpe4


def _const_spec(shape):
    nd = len(shape)
    return pl.BlockSpec(shape, lambda *_: (0,) * nd, pipeline_mode=pl.Buffered(1))


def _proj_call(x, tables, g1, gq, gkv, win, wuq, wkv, tm, sample):
    nb, lp, d = x.shape
    grid = (nb, lp // tm)
    row = lambda w: pl.BlockSpec((1, tm, w), lambda b, j: (b, j, 0))
    tab = pl.BlockSpec((tm, LANES), lambda b, j: (j, 0))
    in_specs = ([row(d)] + [tab] * 6
                + [_const_spec(a.shape) for a in (g1, gq, gkv, win, wuq, wkv)])
    f32_w = [MLA_KV_LORA, HD, HD, LANES]
    if sample:
        bf_w = [MLA_HEADS * MLA_KV_LORA, 2 * LANES, HD, HD, 2048]
    else:
        bf_w = [HD, 2 * LANES, HD, HD, 2048, 2 * HD, HD, HD, HD, LANES]
    out_shape = ([jax.ShapeDtypeStruct((nb, lp, w), F32) for w in f32_w]
                 + [jax.ShapeDtypeStruct((nb, lp, w), BF16) for w in bf_w])
    out_specs = [row(w) for w in f32_w + bf_w]
    if not sample:
        for i in (1, 2):
            out_shape[i] = jax.ShapeDtypeStruct((nb, HD, lp), F32)
            out_specs[i] = pl.BlockSpec((1, HD, tm), lambda b, j: (b, 0, j))
    return pl.pallas_call(
        functools.partial(_proj_kernel, sample),
        grid=grid, in_specs=in_specs, out_specs=out_specs, out_shape=out_shape,
        compiler_params=pltpu.CompilerParams(
            dimension_semantics=("arbitrary", "arbitrary"), vmem_limit_bytes=VMEM_LIMIT),
        name="proj_sample" if sample else "proj_prompt",
    )(x, *tables, g1, gq, gkv, win, wuq, wkv)


_MAX_BISECT = 320
_BISECT_UNROLL = 2


def _chunk(ref, c, tk):
    return ref[:, pl.ds(pl.multiple_of(c * tk, tk), tk)]


def _lane_fold(v, op):
    part = v[:, :LANES]
    for i in range(1, v.shape[1] // LANES):
        part = op(part, v[:, i * LANES:(i + 1) * LANES])
    return part


def _count(sc_ref, nck, tk, pred):
    rows = sc_ref.shape[0]

    def body(c, acc):
        v = jnp.where(pred(_chunk(sc_ref, c, tk), c), 1.0, 0.0)
        return acc + _lane_fold(v, jnp.add)

    acc = lax.fori_loop(0, nck, body, jnp.zeros((rows, LANES), F32))
    return jnp.sum(acc, axis=1, keepdims=True)


def _select_threshold(sc_ref, nck, tk, k_eff, n_valid):
    rows = sc_ref.shape[0]
    big = -NEG_INF

    def mm_body(c, carry):
        mn, mx = carry
        x = _chunk(sc_ref, c, tk)
        xv = jnp.where(x > 0.5 * NEG_INF, x, big)
        return (jnp.minimum(mn, _lane_fold(xv, jnp.minimum)),
                jnp.maximum(mx, _lane_fold(x, jnp.maximum)))

    mn, mx = lax.fori_loop(0, nck, mm_body, (jnp.full((rows, LANES), big, F32),
                                             jnp.full((rows, LANES), NEG_INF, F32)))
    lo0 = jnp.min(mn, axis=1, keepdims=True)
    hi0 = jnp.max(mx, axis=1, keepdims=True)

    def count_ge(thr):
        return _count(sc_ref, nck, tk, lambda x, c: x >= thr)

    def all_rows(fin):
        return jnp.min(jnp.where(fin, 1.0, 0.0)) > 0.5

    def cond(st):
        it, _, _, _, done = st
        return jnp.logical_and(it < _MAX_BISECT, jnp.logical_not(done))

    def body(st):
        it, lo, hi, c_lo, _ = st
        for _ in range(_BISECT_UNROLL):
            mid = 0.5 * lo + 0.5 * hi
            c = count_ge(mid)
            ge = c >= k_eff
            fin = (mid <= lo) | (mid >= hi)
            lo, hi, c_lo = jnp.where(ge, mid, lo), jnp.where(ge, hi, mid), jnp.where(ge, c, c_lo)
            fin = fin | (c_lo == k_eff)
        return it + _BISECT_UNROLL, lo, hi, c_lo, all_rows(fin)

    fin0 = (n_valid == k_eff) | (lo0 >= hi0)
    _, lo, hi, _, _ = lax.while_loop(
        cond, body, (jnp.int32(0), lo0, hi0, n_valid, all_rows(fin0)))

    c_hi = count_ge(hi)
    t = jnp.where(c_hi >= k_eff, hi, lo)
    c_gt = _count(sc_ref, nck, tk, lambda x, c: x > t)
    c_get = count_ge(t)
    need = k_eff - c_gt
    tie_rows = (c_get - c_gt) > need

    @pl.when(jnp.max(jnp.where(tie_rows, 1.0, 0.0)) > 0.5)
    def _():
        lane = lax.broadcasted_iota(jnp.int32, (1, tk), 1).astype(F32)
        width = sc_ref.shape[1]

        def tie_count(m):
            return _count(sc_ref, nck, tk,
                          lambda x, c: (x == t) & ((c * tk).astype(F32) + lane <= m))

        def ibody(_, st):
            lo_i, hi_i = st
            mid = jnp.floor(0.5 * (lo_i + hi_i))
            ok = tie_count(mid) >= need
            return jnp.where(ok, lo_i, mid), jnp.where(ok, mid, hi_i)

        steps = int(np.ceil(np.log2(width))) + 1
        _, m_idx = lax.fori_loop(0, steps, ibody,
                                 (jnp.full((rows, 1), -1.0, F32),
                                  jnp.full((rows, 1), float(width - 1), F32)))

        def fix(c, carry):
            x = _chunk(sc_ref, c, tk)
            drop = (x == t) & ((c * tk).astype(F32) + lane > m_idx) & tie_rows
            sc_ref[:, pl.ds(pl.multiple_of(c * tk, tk), tk)] = jnp.where(drop, NEG_INF, x)
            return carry

        lax.fori_loop(0, nck, fix, 0)

    return t


def _lane_blocks(s):
    return [s[:, i * LANES:(i + 1) * LANES] for i in range(s.shape[1] // LANES)]


def _scale_cols(alpha, x):
    return jnp.concatenate([alpha * b for b in _lane_blocks(x)], axis=1)


def _online_update(m_ref, l_ref, acc_ref, idx, blocks, pv):
    m = m_ref[idx]
    bm = blocks[0]
    for b in blocks[1:]:
        bm = jnp.maximum(bm, b)
    m_new = jnp.maximum(m, jnp.max(bm, axis=1, keepdims=True))
    alpha = jnp.exp(m - m_new)
    ps = [jnp.exp(b - m_new) for b in blocks]
    rs = ps[0]
    for p in ps[1:]:
        rs = rs + p
    l_ref[idx] = alpha * l_ref[idx] + jnp.sum(rs, axis=1, keepdims=True)
    acc_ref[idx] = _scale_cols(alpha, acc_ref[idx]) + pv([p.astype(BF16) for p in ps])
    m_ref[idx] = m_new


def _reset_state(m_ref, l_ref, acc_ref):
    m_ref[...] = jnp.full_like(m_ref, NEG_INF)
    l_ref[...] = jnp.zeros_like(l_ref)
    acc_ref[...] = jnp.zeros_like(acc_ref)


def _softmax_part(m_ref, l_ref, idx, blocks):
    m = m_ref[idx]
    bm = blocks[0]
    for b in blocks[1:]:
        bm = jnp.maximum(bm, b)
    m_new = jnp.maximum(m, jnp.max(bm, axis=1, keepdims=True))
    alpha = jnp.exp(m - m_new)
    ps = [jnp.exp(b - m_new) for b in blocks]
    rs = ps[0]
    for p in ps[1:]:
        rs = rs + p
    l_ref[idx] = alpha * l_ref[idx] + jnp.sum(rs, axis=1, keepdims=True)
    m_ref[idx] = m_new
    return alpha, [p.astype(BF16) for p in ps]


def _attn_kernel(tq, tk, n_keep, topk,
                 qn_ref, qp_ref, qb_ref, qi_ref, small_ref,
                 kcat_ref, vm_ref, kb_ref, vb_ref, ki2_ref,
                 mla_ref, dsa_ref, sc_ref, qc_ref, qd_ref, qx_ref, w_ref, m_ref, l_ref, acc_ref):
    j = pl.program_id(1)
    nh = MLA_HEADS
    n_pairs = nh // 2

    @pl.when(j >= n_keep)
    def _():
        mla_ref[...] = jnp.zeros_like(mla_ref)
        dsa_ref[...] = jnp.zeros_like(dsa_ref)

    @pl.when(j < n_keep)
    def _():
        nck = (j * tq + tq - 1) // tk + 1
        last = nck - 1
        lane = lax.broadcasted_iota(jnp.int32, (1, LANES), 1)
        q_pos = j * tq + lax.broadcasted_iota(jnp.int32, (tq, 1), 0)
        k_lane = lax.broadcasted_iota(jnp.int32, (1, tk), 1)
        zero_b = jnp.zeros((tq, LANES), BF16)

        def causal(c):
            return (c * tk + k_lane) <= q_pos

        def ksl(c):
            return pl.ds(pl.multiple_of(c * tk, tk), tk)

        def pair_sl(pair):
            return slice(pair * LANES, (pair + 1) * LANES)

        def two(x):
            return jnp.concatenate([x, x], axis=0)

        def write_heads(out_ref):
            for pair in range(n_pairs):
                o = acc_ref[pair] / l_ref[pair]
                out_ref[0, :, pair_sl(pair)] = jnp.where(lane < 64, o[:tq], o[tq:]).astype(BF16)

        small = small_ref[0]
        for h in range(nh):
            pair, sub = divmod(h, 2)
            grp, gsub = divmod(h, 4)
            rows = slice(sub * tq, (sub + 1) * tq)
            own = (lane >= 64) == bool(sub)
            qc_ref[pair, rows, :LANES] = jnp.where(own, qn_ref[0, :, pair_sl(pair)], zero_b)
            qc_ref[pair, rows, LANES:] = jnp.where((lane // 32) == gsub,
                                                   qp_ref[0, :, pair_sl(grp)], zero_b)
            qd_ref[pair, rows, :] = jnp.where(own, qb_ref[0, :, pair_sl(pair)], zero_b)
            qx_ref[h * tq:(h + 1) * tq, :] = jnp.where(own, qi_ref[0, :, pair_sl(pair)], zero_b)
            w_ref[h] = jnp.broadcast_to(small[:, _L_WI + h:_L_WI + h + 1], (tq, LANES))

        def attend(c, score_fn, fix_fn, v_ref):
            ks = ksl(c)
            s = {p: score_fn(p, ks) for p in range(min(2, n_pairs))}
            for p in range(n_pairs):
                alpha, ps = _softmax_part(m_ref, l_ref, p, fix_fn(_lane_blocks(s.pop(p))))
                if p + 2 < n_pairs:
                    s[p + 2] = score_fn(p + 2, ks)
                acc_ref[p] = alpha * acc_ref[p] + _dot(jnp.concatenate(ps, axis=1),
                                                       v_ref[0, ks, pair_sl(p)])

        _reset_state(m_ref, l_ref, acc_ref)

        def mla_scores(p, ks):
            return _dot_t(qc_ref[p], kcat_ref[0, ks, 2 * p * LANES:(2 * p + 2) * LANES])

        def mla_idx_chunk(c, masked):
            ks = ksl(c)
            vis = causal(c) if masked else None
            r = _dot_t(qx_ref[...], ki2_ref[0, ks, :])
            sc = None
            for h in range(IDX_HEADS):
                term = _scale_cols(w_ref[h], jnp.maximum(r[h * tq:(h + 1) * tq], 0.0))
                sc = term if sc is None else sc + term
            if masked:
                sc = jnp.where(vis, sc, NEG_INF)
                vis2 = _lane_blocks((c * tk + k_lane) <= two(q_pos))
                fix = lambda blocks: [jnp.where(m, b, NEG_INF) for m, b in zip(vis2, blocks)]
            else:
                fix = lambda blocks: blocks
            sc_ref[:, ks] = sc
            attend(c, mla_scores, fix, vm_ref)

        def mla_idx_step(c, carry):
            mla_idx_chunk(c, False)
            return carry

        lax.fori_loop(0, last, mla_idx_step, 0)
        mla_idx_chunk(last, True)
        write_heads(mla_ref)

        n_valid = (q_pos + 1).astype(F32)
        k_eff = jnp.minimum(n_valid, float(topk))
        t = _select_threshold(sc_ref, nck, tk, k_eff, n_valid)
        t2 = two(jnp.broadcast_to(t, (tq, LANES)))

        _reset_state(m_ref, l_ref, acc_ref)

        def dsa_scores(p, ks):
            return _dot_t(qd_ref[p], kb_ref[0, ks, pair_sl(p)])

        def dsa_step(c, carry):
            sel = [b >= t2 for b in _lane_blocks(two(sc_ref[:, ksl(c)]))]
            fix = lambda blocks: [jnp.where(m, b, NEG_INF) for m, b in zip(sel, blocks)]
            attend(c, dsa_scores, fix, vb_ref)
            return carry

        lax.fori_loop(0, nck, dsa_step, 0)
        write_heads(dsa_ref)


def _attn_call(qn, qp, qb, qi, small, kcat, vm, kbb, vbb, ki2, l_valid, tq, tk):
    nb, lp, _ = qn.shape
    n_keep = -(-l_valid // tq)
    topk = min(TOPK_MAX, l_valid // 4)
    n_pairs = MLA_HEADS // 2
    qrow = lambda w: pl.BlockSpec((1, tq, w), lambda b, j: (b, j, 0))
    krow = lambda w: pl.BlockSpec((1, lp, w), lambda b, j: (b, 0, 0))
    in_specs = [qrow(HD), qrow(2 * LANES), qrow(HD), qrow(HD), qrow(LANES),
                krow(2 * HD), krow(HD), krow(HD), krow(HD), krow(LANES)]
    return pl.pallas_call(
        functools.partial(_attn_kernel, tq, tk, n_keep, topk),
        grid=(nb, lp // tq), in_specs=in_specs,
        out_specs=[qrow(HD), qrow(HD)],
        out_shape=[jax.ShapeDtypeStruct((nb, lp, HD), BF16)] * 2,
        scratch_shapes=[pltpu.VMEM((tq, lp), F32),
                        pltpu.VMEM((n_pairs, 2 * tq, 2 * LANES), BF16),
                        pltpu.VMEM((n_pairs, 2 * tq, LANES), BF16),
                        pltpu.VMEM((IDX_HEADS * tq, LANES), BF16),
                        pltpu.VMEM((IDX_HEADS, tq, LANES), F32),
                        pltpu.VMEM((n_pairs, 2 * tq, LANES), F32),
                        pltpu.VMEM((n_pairs, 2 * tq, LANES), F32),
                        pltpu.VMEM((n_pairs, 2 * tq, LANES), F32)],
        compiler_params=pltpu.CompilerParams(
            dimension_semantics=("arbitrary", "arbitrary"), vmem_limit_bytes=VMEM_LIMIT),
        name="attn_prompt",
    )(qn, qp, qb, qi, small, kcat, vm, kbb, vbb, ki2)


def _merge_kernel(x_ref, mla_ref, dsa_ref, gate_ref, wa_ref, wb_ref, wo_ref, g2_ref,
                  x2_ref, h2_ref):
    d = x_ref.shape[-1]
    a = _dot(mla_ref[0], wa_ref[...])
    b = _dot(dsa_ref[0], wb_ref[...])
    g = gate_ref[0]
    o = g[:, :d].astype(F32) * a + g[:, d:].astype(F32) * b
    x2 = x_ref[0] + _dot(o.astype(BF16), wo_ref[...])
    x2_ref[0] = x2
    h2_ref[0] = _rms(x2, g2_ref[...]).astype(BF16)


def _merge_call(x, mla, dsa, gate, wa, wb, wo, g2, tm, name):
    nb, lp, d = x.shape
    row = lambda w: pl.BlockSpec((1, tm, w), lambda b, j: (b, j, 0))
    return pl.pallas_call(
        _merge_kernel, grid=(nb, lp // tm),
        in_specs=[row(d), row(HD), row(HD), row(2 * d)]
        + [_const_spec(a.shape) for a in (wa, wb, wo, g2)],
        out_specs=[row(d), row(d)],
        out_shape=[jax.ShapeDtypeStruct((nb, lp, d), F32), jax.ShapeDtypeStruct((nb, lp, d), BF16)],
        compiler_params=pltpu.CompilerParams(
            dimension_semantics=("arbitrary", "arbitrary"), vmem_limit_bytes=VMEM_LIMIT),
        name=name,
    )(x, mla, dsa, gate, wa, wb, wo, g2)


_FF_CHUNK = 256


def _ffn_kernel(seq_rows, tail_tile, tail_off, h_ref, x_ref, s1_ref, s2_ref,
                wg_ref, wu_ref, wd_ref, cw_ref, cb_ref, gf_ref, y_ref, tail_ref, prev_ref):
    j = pl.program_id(1)
    tm = h_ref.shape[1]
    d_ff = wg_ref.shape[1]
    h = h_ref[0]
    row = lax.broadcasted_iota(jnp.int32, (tm, 1), 0)
    if seq_rows is None:
        @pl.when(j == 0)
        def _():
            prev_ref[...] = jnp.zeros_like(prev_ref)
        first1, first2 = row < 1, row < 2
    else:
        first1, first2 = (row % seq_rows) < 1, (row % seq_rows) < 2

    acc = jnp.zeros((tm, x_ref.shape[-1]), F32)
    for c0 in range(0, d_ff, _FF_CHUNK):
        sl = slice(c0, c0 + _FF_CHUNK)
        g = _dot(h, wg_ref[:, sl])
        u = _dot(h, wu_ref[:, sl])
        if seq_rows is None:
            p = prev_ref[:, sl]
            hist1 = jnp.broadcast_to(p[SUBLANES - 1:SUBLANES], g.shape)
            hist2 = jnp.where(row < 1, jnp.broadcast_to(p[SUBLANES - 2:SUBLANES - 1], g.shape),
                              hist1)
            prev_ref[:, sl] = g[tm - SUBLANES:]
        else:
            hist1, hist2 = s1_ref[0, :, sl], s2_ref[0, :, sl]
        g1 = jnp.where(first1, hist1, pltpu.roll(g, 1, 0))
        g2 = jnp.where(first2, hist2, pltpu.roll(g, 2, 0))
        cw = cw_ref[:, sl]
        gc = cb_ref[:, sl] + cw[0:1] * g2 + cw[1:2] * g1 + cw[2:3] * g
        act = (gc * jax.nn.sigmoid(gc) * u).astype(BF16)
        acc = acc + _dot(act, wd_ref[sl, :])
        if seq_rows is None:
            @pl.when(j == tail_tile)
            def _(g=g, sl=sl):
                tail_ref[0, :, sl] = g[tail_off:tail_off + SUBLANES]
        else:
            tail_ref[0, :, sl] = g
    y_ref[0] = _rms(x_ref[0] + acc, gf_ref[...])


def _ffn_call(h2, x2, s1, s2, wg, wu, wd, cw, cb, gf, tm, l_valid, sample, name):
    nb, lp, d = x2.shape
    d_ff = wg.shape[1]
    row = lambda w: pl.BlockSpec((1, tm, w), lambda b, j: (b, j, 0))
    if sample:
        seq_rows, tail_tile, tail_off = l_valid, 0, 0
        tail_shape, tail_spec = (nb, lp, d_ff), row(d_ff)
        s_spec = row(d_ff)
    else:
        seq_rows = None
        tail_tile, tail_off = divmod(l_valid - SUBLANES, tm)
        tail_shape = (nb, SUBLANES, d_ff)
        tail_spec = pl.BlockSpec((1, SUBLANES, d_ff), lambda b, j: (b, 0, 0))
        s_spec = pl.BlockSpec((1, SUBLANES, d_ff), lambda b, j: (0, 0, 0))
    return pl.pallas_call(
        functools.partial(_ffn_kernel, seq_rows, tail_tile, tail_off),
        grid=(nb, lp // tm),
        in_specs=[row(d), row(d), s_spec, s_spec]
        + [_const_spec(a.shape) for a in (wg, wu, wd, cw, cb, gf)],
        out_specs=[row(d), tail_spec],
        out_shape=[jax.ShapeDtypeStruct((nb, lp, d), F32), jax.ShapeDtypeStruct(tail_shape, F32)],
        scratch_shapes=[pltpu.VMEM((SUBLANES, d_ff), F32)],
        compiler_params=pltpu.CompilerParams(
            dimension_semantics=("arbitrary", "arbitrary"), vmem_limit_bytes=VMEM_LIMIT),
        name=name,
    )(h2, x2, s1, s2, wg, wu, wd, cw, cb, gf)


def _head_rows_mask(rows_per_head, n_heads, width_per_head):
    r = lax.broadcasted_iota(jnp.int32, (n_heads * rows_per_head, 1), 0) // rows_per_head
    c = lax.broadcasted_iota(jnp.int32, (1, n_heads * width_per_head), 1) // width_per_head
    return r == c


def _diag_heads(full, t, n_heads, width):
    col_head = lax.broadcasted_iota(jnp.int32, (1, n_heads * width), 1) // width
    out = jnp.zeros((t, n_heads * width), F32)
    for h in range(n_heads):
        out = jnp.where(col_head == h, full[h * t:(h + 1) * t], out)
    return out


def _smla_kernel(npp, t_new, pt_ref, qabs_ref, qpe_ref, qi_ref, wrow_ref,
                 ckvn_ref, kpen_ref, ikn_ref, wuv_ref, *rest):
    ckv_pages = rest[:npp]
    kpe_pages = rest[npp:2 * npp]
    ik_pages = rest[2 * npp:3 * npp]
    mla_ref, scp_ref, scn_ref, m_ref, l_ref, acc_ref = rest[3 * npp:]
    c = pl.program_id(1)
    rows = qabs_ref.shape[1]
    qabs, qpe, qi, wrow = qabs_ref[0], qpe_ref[0], qi_ref[0], wrow_ref[0]

    @pl.when(c == 0)
    def _():
        _reset_state(m_ref, l_ref, acc_ref)

    def update(blocks, vals):
        def pv(ps):
            out = _dot(ps[0], vals[0])
            for p, v in zip(ps[1:], vals[1:]):
                out = out + _dot(p, v)
            return out
        _online_update(m_ref, l_ref, acc_ref, 0, blocks, pv)

    def idx_score(s):
        s = jnp.maximum(s, 0.0) * wrow
        out = s[:t_new]
        for h in range(1, IDX_HEADS):
            out = out + s[h * t_new:(h + 1) * t_new]
        return out

    cks = [ckv_pages[i][...].astype(BF16) for i in range(npp)]
    update([_dot_t(qabs, cks[i]) + _dot(qpe, kpe_pages[i][...].astype(BF16))
            for i in range(npp)], cks)
    for i in range(npp):
        scp_ref[0, :, i * PAGE_SIZE:(i + 1) * PAGE_SIZE] = idx_score(
            _dot(qi, ik_pages[i][...].astype(BF16)))

    @pl.when(c == pl.num_programs(1) - 1)
    def _():
        ck = ckvn_ref[0]
        tok = lax.broadcasted_iota(jnp.int32, (rows, 1), 0) % t_new
        key = lax.broadcasted_iota(jnp.int32, (1, PAGE_SIZE), 1)
        vis = key <= tok
        s = _dot_t(qabs, ck) + _dot_t(qpe, kpen_ref[0])
        update([jnp.where(vis, s, NEG_INF)], [ck])
        scn_ref[0] = jnp.where(vis[:t_new], idx_score(_dot_t(qi, ikn_ref[0])), NEG_INF)
        o_lat = _scale_cols(1.0 / l_ref[0], acc_ref[0]).astype(BF16)
        mla_ref[0] = _diag_heads(_dot(o_lat, wuv_ref[...]), t_new, MLA_HEADS, MLA_V).astype(BF16)


def _sdsa_kernel(npp, t_new, topk, past_len, sel_tk, pt_ref, qbd_ref, scp_ref, scn_ref, kbn_ref,
                 vbn_ref, *rest):
    k_pages = rest[:npp]
    v_pages = rest[npp:2 * npp]
    dsa_ref, sc_ref, t_ref, m_ref, l_ref, acc_ref = rest[2 * npp:]
    c = pl.program_id(1)
    rows = qbd_ref.shape[1]
    qbd = qbd_ref[0]
    step_keys = npp * PAGE_SIZE

    @pl.when(c == 0)
    def _():
        _reset_state(m_ref, l_ref, acc_ref)
        sc_ref[:, :past_len] = scp_ref[0]
        sc_ref[:, past_len:past_len + PAGE_SIZE] = scn_ref[0]
        if sel_tk > PAGE_SIZE:
            sc_ref[:, past_len + PAGE_SIZE:] = jnp.full((t_new, sel_tk - PAGE_SIZE), NEG_INF, F32)
        tok = lax.broadcasted_iota(jnp.int32, (t_new, 1), 0)
        n_valid = (past_len + 1 + tok).astype(F32)
        k_eff = jnp.minimum(n_valid, float(topk))
        t_ref[...] = _select_threshold(sc_ref, (past_len + sel_tk) // sel_tk, sel_tk,
                                       k_eff, n_valid)

    t = t_ref[...]

    def masked(s, sc):
        pen = jnp.where(sc >= t, 0.0, NEG_INF)
        sel = jnp.concatenate([pen] * DSA_HEADS, axis=0) == 0.0
        return jnp.where(sel, s, NEG_INF)

    def update(blocks, pv_one, vals):
        def pv(ps):
            out = pv_one(ps[0], vals[0])
            for p, v in zip(ps[1:], vals[1:]):
                out = out + pv_one(p, v)
            return out
        _online_update(m_ref, l_ref, acc_ref, 0, blocks, pv)

    blocks = []
    for i in range(npp):
        start = pl.multiple_of(c * step_keys + i * PAGE_SIZE, PAGE_SIZE)
        blocks.append(masked(_dot(qbd, k_pages[i][...].astype(BF16)),
                             sc_ref[:, pl.ds(start, PAGE_SIZE)]))
    update(blocks, _dot_t, [v_pages[i][...].astype(BF16) for i in range(npp)])

    @pl.when(c == pl.num_programs(1) - 1)
    def _():
        s = masked(_dot_t(qbd, kbn_ref[0]), sc_ref[:, past_len:past_len + PAGE_SIZE])
        update([s], _dot, [vbn_ref[0]])
        o = _scale_cols(1.0 / l_ref[0], acc_ref[0])
        dsa_ref[0] = _diag_heads(o, t_new, DSA_HEADS, DSA_HEAD_DIM).astype(BF16)


def _pages_per_step(n_pages, want):
    p = min(want, n_pages)
    while n_pages % p:
        p -= 1
    return p


def _page_specs(npp, rows, width):
    return [pl.BlockSpec((None, rows, width),
                         functools.partial(lambda i, b, c, pt: (pt[b, c * npp + i], 0, 0), i))
            for i in range(npp)]


def _smla_call(page_table, qabs, qpe, qi, wrow, ckvn, kpen, ikn, wuv, pool_ckv, pool_kpe, pool_ik,
               t_new):
    nb, n_pages = page_table.shape
    npp = _pages_per_step(n_pages, 16)
    rows = qabs.shape[1]
    past_len = n_pages * PAGE_SIZE
    per_b = lambda shape: pl.BlockSpec((1,) + shape, lambda b, c, pt: (b,) + (0,) * len(shape))
    in_specs = ([per_b((rows, MLA_KV_LORA)), per_b((rows, MLA_ROPE)), per_b((rows, IDX_DIM)),
                 per_b((rows, LANES)), per_b((PAGE_SIZE, MLA_KV_LORA)),
                 per_b((PAGE_SIZE, MLA_ROPE)), per_b((PAGE_SIZE, IDX_DIM)),
                 pl.BlockSpec(wuv.shape, lambda b, c, pt: (0, 0))]
                + _page_specs(npp, PAGE_SIZE, MLA_KV_LORA) + _page_specs(npp, MLA_ROPE, PAGE_SIZE)
                + _page_specs(npp, IDX_DIM, PAGE_SIZE))
    out_specs = [per_b((t_new, HD)),
                 pl.BlockSpec((1, t_new, npp * PAGE_SIZE), lambda b, c, pt: (b, 0, c)),
                 per_b((t_new, PAGE_SIZE))]
    out_shape = [jax.ShapeDtypeStruct((nb, t_new, HD), BF16),
                 jax.ShapeDtypeStruct((nb, t_new, past_len), F32),
                 jax.ShapeDtypeStruct((nb, t_new, PAGE_SIZE), F32)]
    grid_spec = pltpu.PrefetchScalarGridSpec(
        num_scalar_prefetch=1, grid=(nb, n_pages // npp), in_specs=in_specs, out_specs=out_specs,
        scratch_shapes=[pltpu.VMEM((1, rows, LANES), F32), pltpu.VMEM((1, rows, LANES), F32),
                        pltpu.VMEM((1, rows, MLA_KV_LORA), F32)])
    return pl.pallas_call(
        functools.partial(_smla_kernel, npp, t_new), grid_spec=grid_spec, out_shape=out_shape,
        compiler_params=pltpu.CompilerParams(
            dimension_semantics=("arbitrary", "arbitrary"), vmem_limit_bytes=VMEM_LIMIT),
        name="sample_mla",
    )(page_table, qabs, qpe, qi, wrow, ckvn, kpen, ikn, wuv,
      *([pool_ckv] * npp), *([pool_kpe] * npp), *([pool_ik] * npp))


def _sdsa_call(page_table, qbd, scp, scn, kbn, vbn, pool_k, pool_v, t_new):
    nb, n_pages = page_table.shape
    npp = _pages_per_step(n_pages, 16)
    rows = qbd.shape[1]
    past_len = n_pages * PAGE_SIZE
    topk = min(TOPK_MAX, (past_len + t_new) // 4)
    sel_tk = PAGE_SIZE
    while sel_tk < 2048 and past_len % (2 * sel_tk) == 0:
        sel_tk *= 2
    per_b = lambda shape: pl.BlockSpec((1,) + shape, lambda b, c, pt: (b,) + (0,) * len(shape))
    in_specs = ([per_b((rows, HD)), per_b((t_new, past_len)), per_b((t_new, PAGE_SIZE)),
                 per_b((PAGE_SIZE, HD)), per_b((PAGE_SIZE, HD))]
                + _page_specs(npp, HD, PAGE_SIZE) + _page_specs(npp, HD, PAGE_SIZE))
    grid_spec = pltpu.PrefetchScalarGridSpec(
        num_scalar_prefetch=1, grid=(nb, n_pages // npp), in_specs=in_specs,
        out_specs=[per_b((t_new, HD))],
        scratch_shapes=[pltpu.VMEM((t_new, past_len + sel_tk), F32),
                        pltpu.VMEM((t_new, 1), F32),
                        pltpu.VMEM((1, rows, LANES), F32), pltpu.VMEM((1, rows, LANES), F32),
                        pltpu.VMEM((1, rows, HD), F32)])
    return pl.pallas_call(
        functools.partial(_sdsa_kernel, npp, t_new, topk, past_len, sel_tk), grid_spec=grid_spec,
        out_shape=[jax.ShapeDtypeStruct((nb, t_new, HD), BF16)],
        compiler_params=pltpu.CompilerParams(
            dimension_semantics=("arbitrary", "arbitrary"), vmem_limit_bytes=VMEM_LIMIT),
        name="sample_dsa",
    )(page_table, qbd, scp, scn, kbn, vbn, *([pool_k] * npp), *([pool_v] * npp))[0]


def _pick_tile(n, candidates):
    for c in candidates:
        if n % c == 0:
            return c
    return n


def _pad_rows(a, rows):
    return jnp.pad(a, ((0, 0), (0, rows - a.shape[1]), (0, 0)))


def _head_major(a, nb, t, heads):
    w = a.shape[-1] // heads
    return a.reshape(nb, t, heads, w).transpose(0, 2, 1, 3).reshape(nb, heads * t, w)


def kernel(x_prompt, x_sample, cache_mla_ckv, cache_mla_kpe, cache_dsa_k, cache_dsa_v, cache_idx_k,
           state_ffn_conv, page_table, meta_tokens, norm1_g, w_in, g_q, g_kv, w_uq, w_uk, w_uv,
           w_br_a, w_br_b, w_o, norm2_g, w_ffn_g, w_ffn_u, ffn_conv_w, ffn_conv_b, w_ffn_d, final_g):
    depth = w_in.shape[0]
    assert depth == 1
    nb, seq, d = x_prompt.shape
    nsb, t_new, _ = x_sample.shape
    n_pages = page_table.shape[1]
    past_len = n_pages * PAGE_SIZE
    l_valid = N_META + seq
    tq, tk = 128, 256
    lp = -(-l_valid // tk) * tk
    assert l_valid % SUBLANES == 0 and t_new % SUBLANES == 0 and t_new <= PAGE_SIZE
    d_ff = w_ffn_g.shape[-1]
    l = 0

    wi = w_in[l]
    cuts = np.cumsum([MLA_Q_LORA, MLA_KV_LORA, MLA_ROPE, HD, HD, HD, HD, IDX_DIM, IDX_HEADS, d, d])
    c_q, c_kv, k_pe, q_b, k_b, v_b, q_i, k_i, w_i, g_a, g_b = jnp.split(wi, cuts[:-1], axis=1)
    pad = jnp.zeros((d, LANES - IDX_DIM - MLA_ROPE - IDX_HEADS), wi.dtype)
    win = jnp.concatenate([c_q, c_kv, q_b, k_b, v_b, q_i, g_a, g_b, k_i, k_pe, w_i, pad],
                          axis=1).astype(BF16)
    wuq = jnp.concatenate([w_uq[l][:, :, :MLA_NOPE].reshape(MLA_Q_LORA, -1),
                           w_uq[l][:, :, MLA_NOPE:].reshape(MLA_Q_LORA, -1)], axis=1).astype(BF16)
    wuk2 = w_uk[l].reshape(MLA_KV_LORA, HD)
    wuv2 = w_uv[l].reshape(MLA_KV_LORA, HD)
    wkv_p = jnp.concatenate([wuk2, wuv2], axis=1).astype(BF16)
    ukt = w_uk[l].transpose(1, 2, 0)
    eye = jnp.eye(MLA_HEADS, dtype=ukt.dtype)
    wuk_bd = (ukt[:, :, None, :] * eye[:, None, :, None]).reshape(HD, MLA_HEADS * MLA_KV_LORA)
    wuk_bd = wuk_bd.astype(BF16)
    g1 = norm1_g[l][None]
    gq = g_q[l][None]
    gkv = g_kv[l][None]
    g2 = norm2_g[l][None]
    gf = final_g[None]
    wa, wb, wo = w_br_a[l].astype(BF16), w_br_b[l].astype(BF16), w_o[l].astype(BF16)
    wg, wu, wd = w_ffn_g[l].astype(BF16), w_ffn_u[l].astype(BF16), w_ffn_d[l].astype(BF16)
    cw, cb = ffn_conv_w[l], ffn_conv_b[l][None]

    meta = jnp.broadcast_to(meta_tokens[None].astype(x_prompt.dtype), (nb, N_META, d))
    xp = jnp.concatenate([meta, x_prompt, jnp.zeros((nb, lp - l_valid, d), x_prompt.dtype)], axis=1)
    tm = _pick_tile(lp, (384, 256, 128))
    tabs_p = _rope_tables(np.arange(lp))
    (ckv_p, kb_p, vb_p, small_p, qn, qp, qb, qi, gate_p, kcat, vm, kbb, vbb, ki2) = _proj_call(
        xp, tabs_p, g1, gq, gkv, win, wuq, wkv_p, tm, sample=False)
    mla_p, dsa_p = _attn_call(qn, qp, qb, qi, small_p, kcat, vm, kbb, vbb, ki2, l_valid, tq, tk)
    x2_p, h2_p = _merge_call(xp, mla_p, dsa_p, gate_p, wa, wb, wo, g2, tm, "merge_prompt")
    zstate = jnp.zeros((1, SUBLANES, d_ff), F32)
    y_p, tail_p = _ffn_call(h2_p, x2_p, zstate, zstate, wg, wu, wd, cw, cb, gf, tm, l_valid,
                            False, "ffn_prompt")

    ns = nsb * t_new
    xs = x_sample.reshape(1, ns, d)
    tabs_s = _rope_tables(past_len + (np.arange(ns) % t_new))
    (ckv_s, kb_s, vb_s, small_s, qlat, qp_s, qb_s, qi_s, gate_s) = _proj_call(
        xs, tabs_s, g1, gq, gkv, win, wuq, wuk_bd, ns, sample=True)
    qabs = _head_major(qlat, nsb, t_new, MLA_HEADS)
    qpe_r = _head_major(qp_s, nsb, t_new, MLA_HEADS)
    qi_r = _head_major(qi_s, nsb, t_new, IDX_HEADS)
    w_rows = small_s[0, :, _L_WI:_L_WI + IDX_HEADS].reshape(nsb, t_new, IDX_HEADS)
    w_rows = jnp.broadcast_to(w_rows.transpose(0, 2, 1).reshape(nsb, IDX_HEADS * t_new, 1),
                              (nsb, IDX_HEADS * t_new, LANES))
    qb_r = _head_major(qb_s, nsb, t_new, DSA_HEADS)
    qbd = jnp.where(_np_head_mask(t_new), jnp.tile(qb_r, (1, 1, DSA_HEADS)), jnp.zeros((), BF16))
    new_rows = lambda a: _pad_rows(a.reshape(nsb, t_new, -1), PAGE_SIZE).astype(BF16)
    ckvn = new_rows(ckv_s)
    kpen = new_rows(small_s[..., _L_KPE:_L_KPE + MLA_ROPE])
    ikn = new_rows(small_s[..., _L_KI:_L_KI + IDX_DIM])
    kbn, vbn = new_rows(kb_s), new_rows(vb_s)
    n_pool = cache_dsa_k.shape[1]
    keys_minor = lambda pool: jnp.moveaxis(pool[l], 1, -1).reshape(n_pool, -1, PAGE_SIZE)
    mla_s, scp, scn = _smla_call(page_table, qabs, qpe_r, qi_r, w_rows, ckvn, kpen, ikn,
                                 wuv2.astype(BF16), cache_mla_ckv[l], keys_minor(cache_mla_kpe),
                                 keys_minor(cache_idx_k), t_new)
    dsa_s = _sdsa_call(page_table, qbd, scp, scn, kbn, vbn,
                       keys_minor(cache_dsa_k), keys_minor(cache_dsa_v), t_new)
    x2_s, h2_s = _merge_call(xs, mla_s.reshape(1, ns, HD), dsa_s.reshape(1, ns, HD), gate_s,
                             wa, wb, wo, g2, ns, "merge_sample")
    st = state_ffn_conv[l]
    zrow = jnp.zeros((nsb, 1, d_ff), st.dtype)
    s1 = jnp.concatenate([st[:, 1:2]] + [zrow] * (t_new - 1), axis=1).reshape(1, ns, d_ff)
    s2 = jnp.concatenate([st[:, 0:1], st[:, 1:2]] + [zrow] * (t_new - 2), axis=1).reshape(1, ns, d_ff)
    y_s, tail_s = _ffn_call(h2_s, x2_s, s1, s2, wg, wu, wd, cw, cb, gf, ns, t_new, True,
                            "ffn_sample")

    y_prompt = y_p[:, N_META:l_valid]
    y_sample = y_s.reshape(nsb, t_new, d)
    cut = lambda a: a[:, :l_valid]
    new_ckv_p = cut(ckv_p)[None]
    new_kpe_p = cut(small_p)[..., _L_KPE:_L_KPE + MLA_ROPE][None]
    pos_last = lambda a: jnp.moveaxis(
        a[:, :, :l_valid].reshape(nb, DSA_HEADS, DSA_HEAD_DIM, l_valid), -1, 1)[None]
    new_k_p, new_v_p = pos_last(kb_p), pos_last(vb_p)
    new_ik_p = cut(small_p)[..., _L_KI:_L_KI + IDX_DIM][None]
    new_conv_p = tail_p[:, SUBLANES - (CONV_W - 1):][None]
    per_s = lambda a: a.reshape(nsb, t_new, -1)
    new_ckv_s = per_s(ckv_s)[None]
    new_kpe_s = per_s(small_s)[..., _L_KPE:_L_KPE + MLA_ROPE][None]
    new_k_s = per_s(kb_s).reshape(1, nsb, t_new, DSA_HEADS, DSA_HEAD_DIM)
    new_v_s = per_s(vb_s).reshape(1, nsb, t_new, DSA_HEADS, DSA_HEAD_DIM)
    new_ik_s = per_s(small_s)[..., _L_KI:_L_KI + IDX_DIM][None]
    new_conv_s = per_s(tail_s)[:, t_new - (CONV_W - 1):][None]
    return (y_prompt, y_sample, new_ckv_p, new_kpe_p, new_k_p, new_v_p, new_ik_p, new_conv_p,
            new_ckv_s, new_kpe_s, new_k_s, new_v_s, new_ik_s, new_conv_s)


def _np_head_mask(t_new):
    r = np.arange(DSA_HEADS * t_new)[:, None] // t_new
    c = np.arange(HD)[None, :] // DSA_HEAD_DIM
    return jnp.asarray(r == c)[None]
```

```python
import functools

import numpy as np
import jax
import jax.numpy as jnp
from jax import lax
from jax.experimental import pallas as pl
from jax.experimental.pallas import tpu as pltpu

N_META = 16
MLA_HEADS = 8
MLA_NOPE = 64
MLA_ROPE = 32
MLA_V = 64
MLA_KV_LORA = 256
MLA_Q_LORA = 768
MLA_SCALE = (MLA_NOPE + MLA_ROPE) ** -0.5
DSA_HEADS = 8
DSA_HEAD_DIM = 64
DSA_SCALE = DSA_HEAD_DIM ** -0.5
IDX_HEADS = 8
IDX_DIM = 64
IDX_SCALE = IDX_DIM ** -0.5
TOPK_MAX = 256
CONV_W = 3
ROPE_THETA = 10000.0
EPS = 1e-6
NEG_INF = -1e30
PAGE_SIZE = 128

LANES = 128
SUBLANES = 8
HD = 512
VMEM_LIMIT = 56 * 1024 * 1024

F32 = jnp.float32
BF16 = jnp.bfloat16

_C_Q, _C_KV, _Q_B, _K_B, _V_B, _Q_I, _G_A, _G_B, _SMALL, _D_IN_P = (
    0, 768, 1024, 1536, 2048, 2560, 3072, 4096, 5120, 5248)
_L_KI, _L_KPE, _L_WI = 0, 64, 96


def _dot(a, b):
    return jnp.dot(a, b, preferred_element_type=F32)


def _dot_t(a, b):
    return lax.dot_general(a, b, (((1,), (1,)), ((), ())), preferred_element_type=F32)


def _rms(x, g):
    ms = jnp.mean(x * x, axis=-1, keepdims=True)
    return x * lax.rsqrt(ms + EPS) * g


def _rope_tables(pos):
    pos = jnp.asarray(pos).astype(F32)[:, None]
    lane = np.arange(LANES)
    out = []
    for width in (64, 32):
        half = width // 2
        m = lane % width
        inv_freq = 1.0 / (ROPE_THETA ** (jnp.arange(half, dtype=F32) / half))
        ang = pos * inv_freq[m % half][None, :]
        c, s = jnp.cos(ang), jnp.sin(ang)
        out += [c, jnp.where(m >= half, s, 0.0), jnp.where(m < half, -s, 0.0)]
    return out


def _rope_blk(x, c, sa, sb, half):
    return x * c + pltpu.roll(x, half, 1) * sa + pltpu.roll(x, LANES - half, 1) * sb


def _proj_kernel(sample, x_ref, c64_ref, sa64_ref, sb64_ref, c32_ref, sa32_ref, sb32_ref,
                 g1_ref, gq_ref, gkv_ref, win_ref, wuq_ref, wkv_ref, *outs):
    if sample:
        (ckv_o, kb_o, vb_o, small_o, qlat_o, qp_o, qb_o, qi_o, gate_o) = outs
    else:
        (ckv_o, kb_o, vb_o, small_o, qn_o, qp_o, qb_o, qi_o, gate_o,
         kcat_o, vm_o, kbb_o, vbb_o, ki2_o) = outs
    hb = _rms(x_ref[0], g1_ref[...]).astype(BF16)
    c64, sa64, sb64 = c64_ref[...], sa64_ref[...], sb64_ref[...]
    c32, sa32, sb32 = c32_ref[...], sa32_ref[...], sb32_ref[...]

    def proj(a, b):
        return _dot(hb, win_ref[:, a:b])

    cq = _rms(proj(_C_Q, _C_KV), gq_ref[...]).astype(BF16)
    q = _dot(cq, wuq_ref[...])
    qn = (q[:, :HD] * MLA_SCALE).astype(BF16)
    for blk in range(2):
        qpe = q[:, HD + blk * LANES:HD + (blk + 1) * LANES]
        qp_o[0, :, blk * LANES:(blk + 1) * LANES] = (
            _rope_blk(qpe, c32, sa32, sb32, 16) * MLA_SCALE).astype(BF16)

    ckv = _rms(proj(_C_KV, _Q_B), gkv_ref[...])
    ckv_o[0] = ckv
    if sample:
        qlat_o[0] = _dot(qn, wkv_ref[...]).astype(BF16)
    else:
        qn_o[0] = qn
        kv = _dot(ckv.astype(BF16), wkv_ref[...])
        for pair in range(MLA_HEADS // 2):
            kcat_o[0, :, 2 * pair * LANES:(2 * pair + 1) * LANES] = (
                kv[:, pair * LANES:(pair + 1) * LANES].astype(BF16))
        vm_o[0] = kv[:, HD:].astype(BF16)

    zq = proj(_Q_B, _K_B)
    zk = proj(_K_B, _V_B)
    zi = proj(_Q_I, _G_A)
    for blk in range(HD // LANES):
        sl = slice(blk * LANES, (blk + 1) * LANES)
        qb_o[0, :, sl] = (_rope_blk(zq[:, sl], c64, sa64, sb64, 32) * DSA_SCALE).astype(BF16)
        qi_o[0, :, sl] = (_rope_blk(zi[:, sl], c64, sa64, sb64, 32) * IDX_SCALE).astype(BF16)
        kr = _rope_blk(zk[:, sl], c64, sa64, sb64, 32)
        if sample:
            kb_o[0, :, sl] = kr
        else:
            kb_o[0, sl, :] = kr.T
            kbb_o[0, :, sl] = kr.astype(BF16)
    zv = proj(_V_B, _Q_I)
    if sample:
        vb_o[0] = zv
    else:
        for blk in range(HD // LANES):
            sl = slice(blk * LANES, (blk + 1) * LANES)
            vb_o[0, sl, :] = zv[:, sl].T
        vbb_o[0] = zv.astype(BF16)

    gate_o[0] = jax.nn.sigmoid(proj(_G_A, _SMALL)).astype(BF16)

    zs = proj(_SMALL, _D_IN_P)
    lane = lax.broadcasted_iota(jnp.int32, (1, LANES), 1)
    m_ki = (lane < _L_KPE).astype(F32)
    m_kpe = ((lane >= _L_KPE) & (lane < _L_WI)).astype(F32)
    m_wi = ((lane >= _L_WI) & (lane < _L_WI + IDX_HEADS)).astype(F32)
    small = (zs * (c64 * m_ki + c32 * m_kpe + (IDX_HEADS ** -0.5) * m_wi)
             + pltpu.roll(zs, 32, 1) * (sa64 * m_ki) + pltpu.roll(zs, 96, 1) * (sb64 * m_ki)
             + pltpu.roll(zs, 16, 1) * (sa32 * m_kpe) + pltpu.roll(zs, 112, 1) * (sb32 * m_kpe))
    small_o[0] = small
    if not sample:
        ki = small * m_ki
        ki2_o[0] = (ki + pltpu.roll(ki, 64, 1)).astype(BF16)
        kp = pltpu.roll(small * m_kpe, 64, 1)
        kpe4 = (kp + pltpu.roll(kp, 32, 1) + pltpu.roll(kp, 64, 1)
                + pltpu.roll(kp, 96, 1)).astype(BF16)
        for pair in range(MLA_HEADS // 2):
            kcat_o[0, :, (2 * pair + 1) * LANES:(2 * pair + 2) * LANES] = kpe4


def _const_spec(shape):
    nd = len(shape)
    return pl.BlockSpec(shape, lambda *_: (0,) * nd, pipeline_mode=pl.Buffered(1))


def _proj_call(x, tables, g1, gq, gkv, win, wuq, wkv, tm, sample):
    nb, lp, d = x.shape
    grid = (nb, lp // tm)
    row = lambda w: pl.BlockSpec((1, tm, w), lambda b, j: (b, j, 0))
    tab = pl.BlockSpec((tm, LANES), lambda b, j: (j, 0))
    in_specs = ([row(d)] + [tab] * 6
                + [_const_spec(a.shape) for a in (g1, gq, gkv, win, wuq, wkv)])
    f32_w = [MLA_KV_LORA, HD, HD, LANES]
    if sample:
        bf_w = [MLA_HEADS * MLA_KV_LORA, 2 * LANES, HD, HD, 2048]
    else:
        bf_w = [HD, 2 * LANES, HD, HD, 2048, 2 * HD, HD, HD, HD, LANES]
    out_shape = ([jax.ShapeDtypeStruct((nb, lp, w), F32) for w in f32_w]
                 + [jax.ShapeDtypeStruct((nb, lp, w), BF16) for w in bf_w])
    out_specs = [row(w) for w in f32_w + bf_w]
    if not sample:
        for i in (1, 2):
            out_shape[i] = jax.ShapeDtypeStruct((nb, HD, lp), F32)
            out_specs[i] = pl.BlockSpec((1, HD, tm), lambda b, j: (b, 0, j))
    return pl.pallas_call(
        functools.partial(_proj_kernel, sample),
        grid=grid, in_specs=in_specs, out_specs=out_specs, out_shape=out_shape,
        compiler_params=pltpu.CompilerParams(
            dimension_semantics=("arbitrary", "arbitrary"), vmem_limit_bytes=VMEM_LIMIT),
        name="proj_sample" if sample else "proj_prompt",
    )(x, *tables, g1, gq, gkv, win, wuq, wkv)


_MAX_BISECT = 320
_BISECT_UNROLL = 2


def _chunk_ds(c, tk):
    start = c * tk
    return pl.ds(start if isinstance(start, int) else pl.multiple_of(start, tk), tk)


def _key_chunk(ref, c, tk, axis):
    ds = _chunk_ds(c, tk)
    return ref[ds, :] if axis == 0 else ref[:, ds]


def _key_fold(v, op, axis):
    if axis == 0:
        parts = [v[i * SUBLANES:(i + 1) * SUBLANES] for i in range(v.shape[0] // SUBLANES)]
    else:
        parts = [v[:, i * LANES:(i + 1) * LANES] for i in range(v.shape[1] // LANES)]
    while len(parts) > 1:
        parts = [op(a, b) for a, b in zip(parts[::2], parts[1::2])] + (
            [parts[-1]] if len(parts) % 2 else [])
    return parts[0]


def _key_index(c, tk, axis):
    shape = (tk, 1) if axis == 0 else (1, tk)
    return (lax.convert_element_type(c * tk, F32)
            + lax.broadcasted_iota(jnp.int32, shape, axis).astype(F32))


def _count(sc_ref, nck, tk, axis, pred):
    nq = sc_ref.shape[1 - axis]
    part = (SUBLANES, nq) if axis == 0 else (nq, LANES)

    def body(c, acc):
        v = jnp.where(pred(_key_chunk(sc_ref, c, tk, axis), c), 1.0, 0.0)
        return acc + _key_fold(v, jnp.add, axis)

    acc = lax.fori_loop(0, nck, body, jnp.zeros(part, F32))
    return jnp.sum(acc, axis=axis, keepdims=True)


def _select_threshold(sc_ref, nck, tk, k_eff, n_valid, axis):
    nq = sc_ref.shape[1 - axis]
    part = (SUBLANES, nq) if axis == 0 else (nq, LANES)
    big = -NEG_INF

    def mm_body(c, carry):
        mn, mx = carry
        x = _key_chunk(sc_ref, c, tk, axis)
        xv = jnp.where(x > 0.5 * NEG_INF, x, big)
        return (jnp.minimum(mn, _key_fold(xv, jnp.minimum, axis)),
                jnp.maximum(mx, _key_fold(x, jnp.maximum, axis)))

    mn, mx = lax.fori_loop(0, nck, mm_body, (jnp.full(part, big, F32),
                                             jnp.full(part, NEG_INF, F32)))
    lo0 = jnp.min(mn, axis=axis, keepdims=True)
    hi0 = jnp.max(mx, axis=axis, keepdims=True)

    def count_ge(thr):
        return _count(sc_ref, nck, tk, axis, lambda x, c: x >= thr)

    def all_rows(fin):
        return jnp.min(jnp.where(fin, 1.0, 0.0)) > 0.5

    c_gt0 = _count(sc_ref, nck, tk, axis, lambda x, c: x > 0.0)
    c_ge0 = count_ge(jnp.zeros_like(lo0))
    at_zero = (c_gt0 < k_eff) & (c_ge0 >= k_eff)
    above = c_gt0 >= k_eff
    inside = (lo0 < 0.0) & (hi0 > 0.0)
    c_lo0 = jnp.where(above & inside, c_ge0, n_valid)
    lo0, hi0 = (jnp.where(at_zero | (above & inside), 0.0, lo0),
                jnp.where(at_zero | (~above & inside), 0.0, hi0))

    def cond(st):
        it, _, _, _, done = st
        return jnp.logical_and(it < _MAX_BISECT, jnp.logical_not(done))

    def body(st):
        it, lo, hi, c_lo, _ = st
        for _ in range(_BISECT_UNROLL):
            mid = 0.5 * lo + 0.5 * hi
            c = count_ge(mid)
            ge = c >= k_eff
            fin = (mid <= lo) | (mid >= hi)
            lo, hi, c_lo = jnp.where(ge, mid, lo), jnp.where(ge, hi, mid), jnp.where(ge, c, c_lo)
            fin = fin | (c_lo == k_eff)
        return it + _BISECT_UNROLL, lo, hi, c_lo, all_rows(fin)

    fin0 = (n_valid == k_eff) | (lo0 >= hi0)
    _, lo, hi, _, _ = lax.while_loop(
        cond, body, (jnp.int32(0), lo0, hi0, c_lo0, all_rows(fin0)))

    c_hi = count_ge(hi)
    t = jnp.where(c_hi >= k_eff, hi, lo)
    c_gt = _count(sc_ref, nck, tk, axis, lambda x, c: x > t)
    c_get = count_ge(t)
    need = k_eff - c_gt
    tie_rows = (c_get - c_gt) > need

    @pl.when(jnp.max(jnp.where(tie_rows, 1.0, 0.0)) > 0.5)
    def _():
        n_keys = sc_ref.shape[axis]

        def tie_count(m):
            return _count(sc_ref, nck, tk, axis,
                          lambda x, c: (x == t) & (_key_index(c, tk, axis) <= m))

        def ibody(_, st):
            lo_i, hi_i = st
            mid = jnp.floor(0.5 * (lo_i + hi_i))
            ok = tie_count(mid) >= need
            return jnp.where(ok, lo_i, mid), jnp.where(ok, mid, hi_i)

        steps = int(np.ceil(np.log2(n_keys))) + 1
        _, m_idx = lax.fori_loop(0, steps, ibody,
                                 (jnp.full(t.shape, -1.0, F32),
                                  jnp.full(t.shape, float(n_keys - 1), F32)))

        def fix(c, carry):
            x = _key_chunk(sc_ref, c, tk, axis)
            drop = (x == t) & (_key_index(c, tk, axis) > m_idx) & tie_rows
            ds = _chunk_ds(c, tk)
            if axis == 0:
                sc_ref[ds, :] = jnp.where(drop, NEG_INF, x)
            else:
                sc_ref[:, ds] = jnp.where(drop, NEG_INF, x)
            return carry

        lax.fori_loop(0, nck, fix, 0)

    return t


def _lane_blocks(s):
    return [s[:, i * LANES:(i + 1) * LANES] for i in range(s.shape[1] // LANES)]


def _scale_cols(alpha, x):
    return jnp.concatenate([alpha * b for b in _lane_blocks(x)], axis=1)


def _online_update(m_ref, l_ref, acc_ref, idx, blocks, pv):
    m = m_ref[idx]
    bm = blocks[0]
    for b in blocks[1:]:
        bm = jnp.maximum(bm, b)
    m_new = jnp.maximum(m, jnp.max(bm, axis=1, keepdims=True))
    alpha = jnp.exp(m - m_new)
    ps = [jnp.exp(b - m_new) for b in blocks]
    rs = ps[0]
    for p in ps[1:]:
        rs = rs + p
    l_ref[idx] = alpha * l_ref[idx] + jnp.sum(rs, axis=1, keepdims=True)
    acc_ref[idx] = _scale_cols(alpha, acc_ref[idx]) + pv([p.astype(BF16) for p in ps])
    m_ref[idx] = m_new


def _reset_state(m_ref, l_ref, acc_ref):
    m_ref[...] = jnp.full_like(m_ref, NEG_INF)
    l_ref[...] = jnp.zeros_like(l_ref)
    acc_ref[...] = jnp.zeros_like(acc_ref)


def _softmax_part(m_ref, l_ref, idx, blocks):
    m = m_ref[idx]
    bm = blocks[0]
    for b in blocks[1:]:
        bm = jnp.maximum(bm, b)
    m_new = jnp.maximum(m, jnp.max(bm, axis=1, keepdims=True))
    alpha = jnp.exp(m - m_new)
    ps = [jnp.exp(b - m_new) for b in blocks]
    rs = ps[0]
    for p in ps[1:]:
        rs = rs + p
    l_ref[idx] = alpha * l_ref[idx] + jnp.sum(rs, axis=1, keepdims=True)
    m_ref[idx] = m_new
    return alpha, [p.astype(BF16) for p in ps]


def _attn_kernel(tq, tk, n_keep, topk,
                 qn_ref, qp_ref, qb_ref, qi_ref, small_ref,
                 kcat_ref, vm_ref, kb_ref, vb_ref, ki2_ref,
                 mla_ref, dsa_ref, sc_ref, qc_ref, qd_ref, qx_ref, w_ref, m_ref, l_ref, acc_ref):
    j = pl.program_id(1)
    nh = MLA_HEADS
    n_pairs = nh // 2

    @pl.when(j >= n_keep)
    def _():
        mla_ref[...] = jnp.zeros_like(mla_ref)
        dsa_ref[...] = jnp.zeros_like(dsa_ref)

    @pl.when(j < n_keep)
    def _():
        nck = (j * tq + tq - 1) // tk + 1
        last = nck - 1
        lane = lax.broadcasted_iota(jnp.int32, (1, LANES), 1)
        q_pos = j * tq + lax.broadcasted_iota(jnp.int32, (tq, 1), 0)
        k_lane = lax.broadcasted_iota(jnp.int32, (1, tk), 1)
        q_row = j * tq + lax.broadcasted_iota(jnp.int32, (1, tq), 1)
        k_row = lax.broadcasted_iota(jnp.int32, (tk, 1), 0)
        zero_b = jnp.zeros((tq, LANES), BF16)

        def ksl(c):
            return _chunk_ds(c, tk)

        def pair_sl(pair):
            return slice(pair * LANES, (pair + 1) * LANES)

        def two(x):
            return jnp.concatenate([x, x], axis=0)

        def write_heads(out_ref):
            for pair in range(n_pairs):
                o = acc_ref[pair] / l_ref[pair]
                out_ref[0, :, pair_sl(pair)] = jnp.where(lane < 64, o[:tq], o[tq:]).astype(BF16)

        small = small_ref[0]
        for h in range(nh):
            pair, sub = divmod(h, 2)
            grp, gsub = divmod(h, 4)
            rows = slice(sub * tq, (sub + 1) * tq)
            own = (lane >= 64) == bool(sub)
            qc_ref[pair, rows, :LANES] = jnp.where(own, qn_ref[0, :, pair_sl(pair)], zero_b)
            qc_ref[pair, rows, LANES:] = jnp.where((lane // 32) == gsub,
                                                   qp_ref[0, :, pair_sl(grp)], zero_b)
            qd_ref[pair, rows, :] = jnp.where(own, qb_ref[0, :, pair_sl(pair)], zero_b)
            qx_ref[h * tq:(h + 1) * tq, :] = jnp.where(own, qi_ref[0, :, pair_sl(pair)], zero_b)
        w_ref[...] = small.T[_L_WI:_L_WI + IDX_HEADS]

        def attend(c, score_fn, fix_fn, v_ref):
            ks = ksl(c)
            s = {p: score_fn(p, ks) for p in range(min(2, n_pairs))}
            for p in range(n_pairs):
                alpha, ps = _softmax_part(m_ref, l_ref, p, fix_fn(_lane_blocks(s.pop(p))))
                if p + 2 < n_pairs:
                    s[p + 2] = score_fn(p + 2, ks)
                acc_ref[p] = alpha * acc_ref[p] + _dot(jnp.concatenate(ps, axis=1),
                                                       v_ref[0, ks, pair_sl(p)])

        _reset_state(m_ref, l_ref, acc_ref)

        def mla_scores(p, ks):
            return _dot_t(qc_ref[p], kcat_ref[0, ks, 2 * p * LANES:(2 * p + 2) * LANES])

        def mla_idx_chunk(c, masked):
            ks = ksl(c)
            r = _dot_t(ki2_ref[0, ks, :], qx_ref[...])
            sc = None
            for h in range(IDX_HEADS):
                term = jnp.maximum(r[:, h * tq:(h + 1) * tq], 0.0) * w_ref[h:h + 1, :]
                sc = term if sc is None else sc + term
            if masked:
                sc = jnp.where((c * tk + k_row) <= q_row, sc, NEG_INF)
                vis2 = _lane_blocks((c * tk + k_lane) <= two(q_pos))
                fix = lambda blocks: [jnp.where(m, b, NEG_INF) for m, b in zip(vis2, blocks)]
            else:
                fix = lambda blocks: blocks
            sc_ref[ks, :] = sc
            attend(c, mla_scores, fix, vm_ref)

        def mla_idx_step(c, carry):
            mla_idx_chunk(c, False)
            return carry

        lax.fori_loop(0, last, mla_idx_step, 0)
        mla_idx_chunk(last, True)
        write_heads(mla_ref)

        n_valid = (q_row + 1).astype(F32)
        k_eff = jnp.minimum(n_valid, float(topk))
        t = _select_threshold(sc_ref, nck, tk, k_eff, n_valid, 0)
        t2 = two(jnp.broadcast_to(t, (LANES, tq)).T)

        _reset_state(m_ref, l_ref, acc_ref)

        def dsa_scores(p, ks):
            return _dot_t(qd_ref[p], kb_ref[0, ks, pair_sl(p)])

        def dsa_step(c, carry):
            sel = [b >= t2 for b in _lane_blocks(two(sc_ref[ksl(c), :].T))]
            fix = lambda blocks: [jnp.where(m, b, NEG_INF) for m, b in zip(sel, blocks)]
            attend(c, dsa_scores, fix, vb_ref)
            return carry

        lax.fori_loop(0, nck, dsa_step, 0)
        write_heads(dsa_ref)


def _attn_call(qn, qp, qb, qi, small, kcat, vm, kbb, vbb, ki2, l_valid, tq, tk):
    nb, lp, _ = qn.shape
    n_keep = -(-l_valid // tq)
    topk = min(TOPK_MAX, l_valid // 4)
    n_pairs = MLA_HEADS // 2
    qrow = lambda w: pl.BlockSpec((1, tq, w), lambda b, j: (b, j, 0))
    krow = lambda w: pl.BlockSpec((1, lp, w), lambda b, j: (b, 0, 0))
    in_specs = [qrow(HD), qrow(2 * LANES), qrow(HD), qrow(HD), qrow(LANES),
                krow(2 * HD), krow(HD), krow(HD), krow(HD), krow(LANES)]
    return pl.pallas_call(
        functools.partial(_attn_kernel, tq, tk, n_keep, topk),
        grid=(nb, lp // tq), in_specs=in_specs,
        out_specs=[qrow(HD), qrow(HD)],
        out_shape=[jax.ShapeDtypeStruct((nb, lp, HD), BF16)] * 2,
        scratch_shapes=[pltpu.VMEM((lp, tq), F32),
                        pltpu.VMEM((n_pairs, 2 * tq, 2 * LANES), BF16),
                        pltpu.VMEM((n_pairs, 2 * tq, LANES), BF16),
                        pltpu.VMEM((IDX_HEADS * tq, LANES), BF16),
                        pltpu.VMEM((IDX_HEADS, tq), F32),
                        pltpu.VMEM((n_pairs, 2 * tq, LANES), F32),
                        pltpu.VMEM((n_pairs, 2 * tq, LANES), F32),
                        pltpu.VMEM((n_pairs, 2 * tq, LANES), F32)],
        compiler_params=pltpu.CompilerParams(
            dimension_semantics=("arbitrary", "arbitrary"), vmem_limit_bytes=VMEM_LIMIT),
        name="attn_prompt",
    )(qn, qp, qb, qi, small, kcat, vm, kbb, vbb, ki2)


def _merge_kernel(x_ref, mla_ref, dsa_ref, gate_ref, wa_ref, wb_ref, wo_ref, g2_ref,
                  x2_ref, h2_ref):
    d = x_ref.shape[-1]
    a = _dot(mla_ref[0], wa_ref[...])
    b = _dot(dsa_ref[0], wb_ref[...])
    g = gate_ref[0]
    o = g[:, :d].astype(F32) * a + g[:, d:].astype(F32) * b
    x2 = x_ref[0] + _dot(o.astype(BF16), wo_ref[...])
    x2_ref[0] = x2
    h2_ref[0] = _rms(x2, g2_ref[...]).astype(BF16)


def _merge_call(x, mla, dsa, gate, wa, wb, wo, g2, tm, name):
    nb, lp, d = x.shape
    row = lambda w: pl.BlockSpec((1, tm, w), lambda b, j: (b, j, 0))
    return pl.pallas_call(
        _merge_kernel, grid=(nb, lp // tm),
        in_specs=[row(d), row(HD), row(HD), row(2 * d)]
        + [_const_spec(a.shape) for a in (wa, wb, wo, g2)],
        out_specs=[row(d), row(d)],
        out_shape=[jax.ShapeDtypeStruct((nb, lp, d), F32), jax.ShapeDtypeStruct((nb, lp, d), BF16)],
        compiler_params=pltpu.CompilerParams(
            dimension_semantics=("arbitrary", "arbitrary"), vmem_limit_bytes=VMEM_LIMIT),
        name=name,
    )(x, mla, dsa, gate, wa, wb, wo, g2)


_FF_CHUNK = 256
_FF_AHEAD = 1


def _ffn_kernel(seq_rows, tail_tile, tail_off, h_ref, x_ref, s1_ref, s2_ref,
                wg_ref, wu_ref, wd_ref, cw_ref, cb_ref, gf_ref, y_ref, tail_ref, prev_ref):
    j = pl.program_id(1)
    tm = h_ref.shape[1]
    d_ff = wg_ref.shape[1]
    h = h_ref[0]
    row = lax.broadcasted_iota(jnp.int32, (tm, 1), 0)
    if seq_rows is None:
        @pl.when(j == 0)
        def _():
            prev_ref[...] = jnp.zeros_like(prev_ref)
        first1, first2 = row < 1, row < 2
    else:
        first1, first2 = (row % seq_rows) < 1, (row % seq_rows) < 2

    acc = jnp.zeros((tm, x_ref.shape[-1]), F32)
    def up(c0):
        return _dot(h, wg_ref[:, c0:c0 + _FF_CHUNK]), _dot(h, wu_ref[:, c0:c0 + _FF_CHUNK])

    ahead = [up(c0) for c0 in range(0, min(d_ff, _FF_AHEAD * _FF_CHUNK), _FF_CHUNK)]
    for c0 in range(0, d_ff, _FF_CHUNK):
        sl = slice(c0, c0 + _FF_CHUNK)
        g, u = ahead.pop(0)
        if c0 + _FF_AHEAD * _FF_CHUNK < d_ff:
            ahead.append(up(c0 + _FF_AHEAD * _FF_CHUNK))
        if seq_rows is None:
            p = prev_ref[:, sl]
            hist1 = jnp.broadcast_to(p[SUBLANES - 1:SUBLANES], g.shape)
            hist2 = jnp.where(row < 1, jnp.broadcast_to(p[SUBLANES - 2:SUBLANES - 1], g.shape),
                              hist1)
            prev_ref[:, sl] = g[tm - SUBLANES:]
        else:
            hist1, hist2 = s1_ref[0, :, sl], s2_ref[0, :, sl]
        g1 = jnp.where(first1, hist1, pltpu.roll(g, 1, 0))
        g2 = jnp.where(first2, hist2, pltpu.roll(g, 2, 0))
        cw = cw_ref[:, sl]
        gc = cb_ref[:, sl] + cw[0:1] * g2 + cw[1:2] * g1 + cw[2:3] * g
        act = (gc * jax.nn.sigmoid(gc) * u).astype(BF16)
        acc = acc + _dot(act, wd_ref[sl, :])
        if seq_rows is None:
            @pl.when(j == tail_tile)
            def _(g=g, sl=sl):
                tail_ref[0, :, sl] = g[tail_off:tail_off + SUBLANES]
        else:
            tail_ref[0, :, sl] = g
    y_ref[0] = _rms(x_ref[0] + acc, gf_ref[...])


def _ffn_call(h2, x2, s1, s2, wg, wu, wd, cw, cb, gf, tm, l_valid, sample, name):
    nb, lp, d = x2.shape
    d_ff = wg.shape[1]
    row = lambda w: pl.BlockSpec((1, tm, w), lambda b, j: (b, j, 0))
    if sample:
        seq_rows, tail_tile, tail_off = l_valid, 0, 0
        tail_shape, tail_spec = (nb, lp, d_ff), row(d_ff)
        s_spec = row(d_ff)
    else:
        seq_rows = None
        tail_tile, tail_off = divmod(l_valid - SUBLANES, tm)
        tail_shape = (nb, SUBLANES, d_ff)
        tail_spec = pl.BlockSpec((1, SUBLANES, d_ff), lambda b, j: (b, 0, 0))
        s_spec = pl.BlockSpec((1, SUBLANES, d_ff), lambda b, j: (0, 0, 0))
    return pl.pallas_call(
        functools.partial(_ffn_kernel, seq_rows, tail_tile, tail_off),
        grid=(nb, lp // tm),
        in_specs=[row(d), row(d), s_spec, s_spec]
        + [_const_spec(a.shape) for a in (wg, wu, wd, cw, cb, gf)],
        out_specs=[row(d), tail_spec],
        out_shape=[jax.ShapeDtypeStruct((nb, lp, d), F32), jax.ShapeDtypeStruct(tail_shape, F32)],
        scratch_shapes=[pltpu.VMEM((SUBLANES, d_ff), F32)],
        compiler_params=pltpu.CompilerParams(
            dimension_semantics=("arbitrary", "arbitrary"), vmem_limit_bytes=VMEM_LIMIT),
        name=name,
    )(h2, x2, s1, s2, wg, wu, wd, cw, cb, gf)


def _head_rows_mask(rows_per_head, n_heads, width_per_head):
    r = lax.broadcasted_iota(jnp.int32, (n_heads * rows_per_head, 1), 0) // rows_per_head
    c = lax.broadcasted_iota(jnp.int32, (1, n_heads * width_per_head), 1) // width_per_head
    return r == c


def _diag_heads(full, t, n_heads, width):
    col_head = lax.broadcasted_iota(jnp.int32, (1, n_heads * width), 1) // width
    out = jnp.zeros((t, n_heads * width), F32)
    for h in range(n_heads):
        out = jnp.where(col_head == h, full[h * t:(h + 1) * t], out)
    return out


def _smla_kernel(npp, t_new, pt_ref, qabs_ref, qpe_ref, qi_ref, wrow_ref,
                 ckvn_ref, kpen_ref, ikn_ref, wuv_ref, *rest):
    ckv_pages = rest[:npp]
    kpe_pages = rest[npp:2 * npp]
    ik_pages = rest[2 * npp:3 * npp]
    mla_ref, scp_ref, scn_ref, m_ref, l_ref, acc_ref = rest[3 * npp:]
    c = pl.program_id(1)
    rows = qabs_ref.shape[1]
    qabs, qpe, qi, wrow = qabs_ref[0], qpe_ref[0], qi_ref[0], wrow_ref[0]

    @pl.when(c == 0)
    def _():
        _reset_state(m_ref, l_ref, acc_ref)

    def update(blocks, vals):
        def pv(ps):
            out = _dot(ps[0], vals[0])
            for p, v in zip(ps[1:], vals[1:]):
                out = out + _dot(p, v)
            return out
        _online_update(m_ref, l_ref, acc_ref, 0, blocks, pv)

    def idx_score(s):
        s = jnp.maximum(s, 0.0) * wrow
        out = s[:t_new]
        for h in range(1, IDX_HEADS):
            out = out + s[h * t_new:(h + 1) * t_new]
        return out

    cks = [ckv_pages[i][...].astype(BF16) for i in range(npp)]
    update([_dot_t(qabs, cks[i]) + _dot(qpe, kpe_pages[i][...].astype(BF16))
            for i in range(npp)], cks)
    for i in range(npp):
        scp_ref[0, :, i * PAGE_SIZE:(i + 1) * PAGE_SIZE] = idx_score(
            _dot(qi, ik_pages[i][...].astype(BF16)))

    @pl.when(c == pl.num_programs(1) - 1)
    def _():
        ck = ckvn_ref[0]
        tok = lax.broadcasted_iota(jnp.int32, (rows, 1), 0) % t_new
        key = lax.broadcasted_iota(jnp.int32, (1, PAGE_SIZE), 1)
        vis = key <= tok
        s = _dot_t(qabs, ck) + _dot_t(qpe, kpen_ref[0])
        update([jnp.where(vis, s, NEG_INF)], [ck])
        scn_ref[0] = jnp.where(vis[:t_new], idx_score(_dot_t(qi, ikn_ref[0])), NEG_INF)
        o_lat = _scale_cols(1.0 / l_ref[0], acc_ref[0]).astype(BF16)
        mla_ref[0] = _diag_heads(_dot(o_lat, wuv_ref[...]), t_new, MLA_HEADS, MLA_V).astype(BF16)


def _sdsa_kernel(npp, t_new, topk, past_len, sel_tk, pt_ref, qbd_ref, scp_ref, scn_ref, kbn_ref,
                 vbn_ref, *rest):
    k_pages = rest[:npp]
    v_pages = rest[npp:2 * npp]
    dsa_ref, sc_ref, t_ref, m_ref, l_ref, acc_ref = rest[2 * npp:]
    c = pl.program_id(1)
    rows = qbd_ref.shape[1]
    qbd = qbd_ref[0]
    step_keys = npp * PAGE_SIZE

    @pl.when(c == 0)
    def _():
        _reset_state(m_ref, l_ref, acc_ref)
        sc_ref[:, :past_len] = scp_ref[0]
        sc_ref[:, past_len:past_len + PAGE_SIZE] = scn_ref[0]
        if sel_tk > PAGE_SIZE:
            sc_ref[:, past_len + PAGE_SIZE:] = jnp.full((t_new, sel_tk - PAGE_SIZE), NEG_INF, F32)
        tok = lax.broadcasted_iota(jnp.int32, (t_new, 1), 0)
        n_valid = (past_len + 1 + tok).astype(F32)
        k_eff = jnp.minimum(n_valid, float(topk))
        t_ref[...] = _select_threshold(sc_ref, (past_len + sel_tk) // sel_tk, sel_tk,
                                       k_eff, n_valid, 1)

    t = t_ref[...]

    def masked(s, sc):
        pen = jnp.where(sc >= t, 0.0, NEG_INF)
        sel = jnp.concatenate([pen] * DSA_HEADS, axis=0) == 0.0
        return jnp.where(sel, s, NEG_INF)

    def update(blocks, pv_one, vals):
        def pv(ps):
            out = pv_one(ps[0], vals[0])
            for p, v in zip(ps[1:], vals[1:]):
                out = out + pv_one(p, v)
            return out
        _online_update(m_ref, l_ref, acc_ref, 0, blocks, pv)

    blocks = []
    for i in range(npp):
        start = pl.multiple_of(c * step_keys + i * PAGE_SIZE, PAGE_SIZE)
        blocks.append(masked(_dot(qbd, k_pages[i][...].astype(BF16)),
                             sc_ref[:, pl.ds(start, PAGE_SIZE)]))
    update(blocks, _dot_t, [v_pages[i][...].astype(BF16) for i in range(npp)])

    @pl.when(c == pl.num_programs(1) - 1)
    def _():
        s = masked(_dot_t(qbd, kbn_ref[0]), sc_ref[:, past_len:past_len + PAGE_SIZE])
        update([s], _dot, [vbn_ref[0]])
        o = _scale_cols(1.0 / l_ref[0], acc_ref[0])
        dsa_ref[0] = _diag_heads(o, t_new, DSA_HEADS, DSA_HEAD_DIM).astype(BF16)


def _pages_per_step(n_pages, want):
    p = min(want, n_pages)
    while n_pages % p:
        p -= 1
    return p


def _page_specs(npp, rows, width):
    return [pl.BlockSpec((None, rows, width),
                         functools.partial(lambda i, b, c, pt: (pt[b, c * npp + i], 0, 0), i))
            for i in range(npp)]


def _smla_call(page_table, qabs, qpe, qi, wrow, ckvn, kpen, ikn, wuv, pool_ckv, pool_kpe, pool_ik,
               t_new):
    nb, n_pages = page_table.shape
    npp = _pages_per_step(n_pages, 16)
    rows = qabs.shape[1]
    past_len = n_pages * PAGE_SIZE
    per_b = lambda shape: pl.BlockSpec((1,) + shape, lambda b, c, pt: (b,) + (0,) * len(shape))
    in_specs = ([per_b((rows, MLA_KV_LORA)), per_b((rows, MLA_ROPE)), per_b((rows, IDX_DIM)),
                 per_b((rows, LANES)), per_b((PAGE_SIZE, MLA_KV_LORA)),
                 per_b((PAGE_SIZE, MLA_ROPE)), per_b((PAGE_SIZE, IDX_DIM)),
                 pl.BlockSpec(wuv.shape, lambda b, c, pt: (0, 0))]
                + _page_specs(npp, PAGE_SIZE, MLA_KV_LORA) + _page_specs(npp, MLA_ROPE, PAGE_SIZE)
                + _page_specs(npp, IDX_DIM, PAGE_SIZE))
    out_specs = [per_b((t_new, HD)),
                 pl.BlockSpec((1, t_new, npp * PAGE_SIZE), lambda b, c, pt: (b, 0, c)),
                 per_b((t_new, PAGE_SIZE))]
    out_shape = [jax.ShapeDtypeStruct((nb, t_new, HD), BF16),
                 jax.ShapeDtypeStruct((nb, t_new, past_len), F32),
                 jax.ShapeDtypeStruct((nb, t_new, PAGE_SIZE), F32)]
    grid_spec = pltpu.PrefetchScalarGridSpec(
        num_scalar_prefetch=1, grid=(nb, n_pages // npp), in_specs=in_specs, out_specs=out_specs,
        scratch_shapes=[pltpu.VMEM((1, rows, LANES), F32), pltpu.VMEM((1, rows, LANES), F32),
                        pltpu.VMEM((1, rows, MLA_KV_LORA), F32)])
    return pl.pallas_call(
        functools.partial(_smla_kernel, npp, t_new), grid_spec=grid_spec, out_shape=out_shape,
        compiler_params=pltpu.CompilerParams(
            dimension_semantics=("arbitrary", "arbitrary"), vmem_limit_bytes=VMEM_LIMIT),
        name="sample_mla",
    )(page_table, qabs, qpe, qi, wrow, ckvn, kpen, ikn, wuv,
      *([pool_ckv] * npp), *([pool_kpe] * npp), *([pool_ik] * npp))


def _sdsa_call(page_table, qbd, scp, scn, kbn, vbn, pool_k, pool_v, t_new):
    nb, n_pages = page_table.shape
    npp = _pages_per_step(n_pages, 16)
    rows = qbd.shape[1]
    past_len = n_pages * PAGE_SIZE
    topk = min(TOPK_MAX, (past_len + t_new) // 4)
    sel_tk = PAGE_SIZE
    while sel_tk < 2048 and past_len % (2 * sel_tk) == 0:
        sel_tk *= 2
    per_b = lambda shape: pl.BlockSpec((1,) + shape, lambda b, c, pt: (b,) + (0,) * len(shape))
    in_specs = ([per_b((rows, HD)), per_b((t_new, past_len)), per_b((t_new, PAGE_SIZE)),
                 per_b((PAGE_SIZE, HD)), per_b((PAGE_SIZE, HD))]
                + _page_specs(npp, HD, PAGE_SIZE) + _page_specs(npp, HD, PAGE_SIZE))
    grid_spec = pltpu.PrefetchScalarGridSpec(
        num_scalar_prefetch=1, grid=(nb, n_pages // npp), in_specs=in_specs,
        out_specs=[per_b((t_new, HD))],
        scratch_shapes=[pltpu.VMEM((t_new, past_len + sel_tk), F32),
                        pltpu.VMEM((t_new, 1), F32),
                        pltpu.VMEM((1, rows, LANES), F32), pltpu.VMEM((1, rows, LANES), F32),
                        pltpu.VMEM((1, rows, HD), F32)])
    return pl.pallas_call(
        functools.partial(_sdsa_kernel, npp, t_new, topk, past_len, sel_tk), grid_spec=grid_spec,
        out_shape=[jax.ShapeDtypeStruct((nb, t_new, HD), BF16)],
        compiler_params=pltpu.CompilerParams(
            dimension_semantics=("arbitrary", "arbitrary"), vmem_limit_bytes=VMEM_LIMIT),
        name="sample_dsa",
    )(page_table, qbd, scp, scn, kbn, vbn, *([pool_k] * npp), *([pool_v] * npp))[0]


def _pick_tile(n, candidates):
    for c in candidates:
        if n % c == 0:
            return c
    return n


def _pad_rows(a, rows):
    return jnp.pad(a, ((0, 0), (0, rows - a.shape[1]), (0, 0)))


def _head_major(a, nb, t, heads):
    w = a.shape[-1] // heads
    return a.reshape(nb, t, heads, w).transpose(0, 2, 1, 3).reshape(nb, heads * t, w)


def kernel(x_prompt, x_sample, cache_mla_ckv, cache_mla_kpe, cache_dsa_k, cache_dsa_v, cache_idx_k,
           state_ffn_conv, page_table, meta_tokens, norm1_g, w_in, g_q, g_kv, w_uq, w_uk, w_uv,
           w_br_a, w_br_b, w_o, norm2_g, w_ffn_g, w_ffn_u, ffn_conv_w, ffn_conv_b, w_ffn_d, final_g):
    depth = w_in.shape[0]
    assert depth == 1
    nb, seq, d = x_prompt.shape
    nsb, t_new, _ = x_sample.shape
    n_pages = page_table.shape[1]
    past_len = n_pages * PAGE_SIZE
    l_valid = N_META + seq
    tq, tk = 128, 256
    lp = -(-l_valid // tk) * tk
    assert l_valid % SUBLANES == 0 and t_new % SUBLANES == 0 and t_new <= PAGE_SIZE
    d_ff = w_ffn_g.shape[-1]
    l = 0

    wi = w_in[l]
    cuts = np.cumsum([MLA_Q_LORA, MLA_KV_LORA, MLA_ROPE, HD, HD, HD, HD, IDX_DIM, IDX_HEADS, d, d])
    c_q, c_kv, k_pe, q_b, k_b, v_b, q_i, k_i, w_i, g_a, g_b = jnp.split(wi, cuts[:-1], axis=1)
    pad = jnp.zeros((d, LANES - IDX_DIM - MLA_ROPE - IDX_HEADS), wi.dtype)
    win = jnp.concatenate([c_q, c_kv, q_b, k_b, v_b, q_i, g_a, g_b, k_i, k_pe, w_i, pad],
                          axis=1).astype(BF16)
    wuq = jnp.concatenate([w_uq[l][:, :, :MLA_NOPE].reshape(MLA_Q_LORA, -1),
                           w_uq[l][:, :, MLA_NOPE:].reshape(MLA_Q_LORA, -1)], axis=1).astype(BF16)
    wuk2 = w_uk[l].reshape(MLA_KV_LORA, HD)
    wuv2 = w_uv[l].reshape(MLA_KV_LORA, HD)
    wkv_p = jnp.concatenate([wuk2, wuv2], axis=1).astype(BF16)
    ukt = w_uk[l].transpose(1, 2, 0)
    eye = jnp.eye(MLA_HEADS, dtype=ukt.dtype)
    wuk_bd = (ukt[:, :, None, :] * eye[:, None, :, None]).reshape(HD, MLA_HEADS * MLA_KV_LORA)
    wuk_bd = wuk_bd.astype(BF16)
    g1 = norm1_g[l][None]
    gq = g_q[l][None]
    gkv = g_kv[l][None]
    g2 = norm2_g[l][None]
    gf = final_g[None]
    wa, wb, wo = w_br_a[l].astype(BF16), w_br_b[l].astype(BF16), w_o[l].astype(BF16)
    wg, wu, wd = w_ffn_g[l].astype(BF16), w_ffn_u[l].astype(BF16), w_ffn_d[l].astype(BF16)
    cw, cb = ffn_conv_w[l], ffn_conv_b[l][None]

    meta = jnp.broadcast_to(meta_tokens[None].astype(x_prompt.dtype), (nb, N_META, d))
    xp = jnp.concatenate([meta, x_prompt, jnp.zeros((nb, lp - l_valid, d), x_prompt.dtype)], axis=1)
    tm = _pick_tile(lp, (384, 256, 128))
    tabs_p = _rope_tables(np.arange(lp))
    (ckv_p, kb_p, vb_p, small_p, qn, qp, qb, qi, gate_p, kcat, vm, kbb, vbb, ki2) = _proj_call(
        xp, tabs_p, g1, gq, gkv, win, wuq, wkv_p, tm, sample=False)
    mla_p, dsa_p = _attn_call(qn, qp, qb, qi, small_p, kcat, vm, kbb, vbb, ki2, l_valid, tq, tk)
    x2_p, h2_p = _merge_call(xp, mla_p, dsa_p, gate_p, wa, wb, wo, g2, tm, "merge_prompt")
    zstate = jnp.zeros((1, SUBLANES, d_ff), F32)
    y_p, tail_p = _ffn_call(h2_p, x2_p, zstate, zstate, wg, wu, wd, cw, cb, gf, tm, l_valid,
                            False, "ffn_prompt")

    ns = nsb * t_new
    xs = x_sample.reshape(1, ns, d)
    tabs_s = _rope_tables(past_len + (np.arange(ns) % t_new))
    (ckv_s, kb_s, vb_s, small_s, qlat, qp_s, qb_s, qi_s, gate_s) = _proj_call(
        xs, tabs_s, g1, gq, gkv, win, wuq, wuk_bd, ns, sample=True)
    qabs = _head_major(qlat, nsb, t_new, MLA_HEADS)
    qpe_r = _head_major(qp_s, nsb, t_new, MLA_HEADS)
    qi_r = _head_major(qi_s, nsb, t_new, IDX_HEADS)
    w_rows = small_s[0, :, _L_WI:_L_WI + IDX_HEADS].reshape(nsb, t_new, IDX_HEADS)
    w_rows = jnp.broadcast_to(w_rows.transpose(0, 2, 1).reshape(nsb, IDX_HEADS * t_new, 1),
                              (nsb, IDX_HEADS * t_new, LANES))
    qb_r = _head_major(qb_s, nsb, t_new, DSA_HEADS)
    qbd = jnp.where(_np_head_mask(t_new), jnp.tile(qb_r, (1, 1, DSA_HEADS)), jnp.zeros((), BF16))
    new_rows = lambda a: _pad_rows(a.reshape(nsb, t_new, -1), PAGE_SIZE).astype(BF16)
    ckvn = new_rows(ckv_s)
    kpen = new_rows(small_s[..., _L_KPE:_L_KPE + MLA_ROPE])
    ikn = new_rows(small_s[..., _L_KI:_L_KI + IDX_DIM])
    kbn, vbn = new_rows(kb_s), new_rows(vb_s)
    n_pool = cache_dsa_k.shape[1]
    keys_minor = lambda pool: jnp.moveaxis(pool[l], 1, -1).reshape(n_pool, -1, PAGE_SIZE)
    mla_s, scp, scn = _smla_call(page_table, qabs, qpe_r, qi_r, w_rows, ckvn, kpen, ikn,
                                 wuv2.astype(BF16), cache_mla_ckv[l], keys_minor(cache_mla_kpe),
                                 keys_minor(cache_idx_k), t_new)
    dsa_s = _sdsa_call(page_table, qbd, scp, scn, kbn, vbn,
                       keys_minor(cache_dsa_k), keys_minor(cache_dsa_v), t_new)
    x2_s, h2_s = _merge_call(xs, mla_s.reshape(1, ns, HD), dsa_s.reshape(1, ns, HD), gate_s,
                             wa, wb, wo, g2, ns, "merge_sample")
    st = state_ffn_conv[l]
    zrow = jnp.zeros((nsb, 1, d_ff), st.dtype)
    s1 = jnp.concatenate([st[:, 1:2]] + [zrow] * (t_new - 1), axis=1).reshape(1, ns, d_ff)
    s2 = jnp.concatenate([st[:, 0:1], st[:, 1:2]] + [zrow] * (t_new - 2), axis=1).reshape(1, ns, d_ff)
    y_s, tail_s = _ffn_call(h2_s, x2_s, s1, s2, wg, wu, wd, cw, cb, gf, ns, t_new, True,
                            "ffn_sample")

    y_prompt = y_p[:, N_META:l_valid]
    y_sample = y_s.reshape(nsb, t_new, d)
    cut = lambda a: a[:, :l_valid]
    new_ckv_p = cut(ckv_p)[None]
    new_kpe_p = cut(small_p)[..., _L_KPE:_L_KPE + MLA_ROPE][None]
    pos_last = lambda a: jnp.moveaxis(
        a[:, :, :l_valid].reshape(nb, DSA_HEADS, DSA_HEAD_DIM, l_valid), -1, 1)[None]
    new_k_p, new_v_p = pos_last(kb_p), pos_last(vb_p)
    new_ik_p = cut(small_p)[..., _L_KI:_L_KI + IDX_DIM][None]
    new_conv_p = tail_p[:, SUBLANES - (CONV_W - 1):][None]
    per_s = lambda a: a.reshape(nsb, t_new, -1)
    new_ckv_s = per_s(ckv_s)[None]
    new_kpe_s = per_s(small_s)[..., _L_KPE:_L_KPE + MLA_ROPE][None]
    new_k_s = per_s(kb_s).reshape(1, nsb, t_new, DSA_HEADS, DSA_HEAD_DIM)
    new_v_s = per_s(vb_s).reshape(1, nsb, t_new, DSA_HEADS, DSA_HEAD_DIM)
    new_ik_s = per_s(small_s)[..., _L_KI:_L_KI + IDX_DIM][None]
    new_conv_s = per_s(tail_s)[:, t_new - (CONV_W - 1):][None]
    return (y_prompt, y_sample, new_ckv_p, new_kpe_p, new_k_p, new_v_p, new_ik_p, new_conv_p,
            new_ckv_s, new_kpe_s, new_k_s, new_v_s, new_ik_s, new_conv_s)


def _np_head_mask(t_new):
    r = np.arange(DSA_HEADS * t_new)[:, None] // t_new
    c = np.arange(HD)[None, :] // DSA_HEAD_DIM
    return jnp.asarray(r == c)[None]
```

```python
import functools

import numpy as np
import jax
import jax.numpy as jnp
from jax import lax
from jax.experimental import pallas as pl
from jax.experimental.pallas import tpu as pltpu

N_META = 16
MLA_HEADS = 8
MLA_NOPE = 64
MLA_ROPE = 32
MLA_V = 64
MLA_KV_LORA = 256
MLA_Q_LORA = 768
MLA_SCALE = (MLA_NOPE + MLA_ROPE) ** -0.5
DSA_HEADS = 8
DSA_HEAD_DIM = 64
DSA_SCALE = DSA_HEAD_DIM ** -0.5
IDX_HEADS = 8
IDX_DIM = 64
IDX_SCALE = IDX_DIM ** -0.5
LOG2E = 1.4426950408889634
MLA_QSCALE = MLA_SCALE * LOG2E
DSA_QSCALE = DSA_SCALE * LOG2E
TOPK_MAX = 256
CONV_W = 3
ROPE_THETA = 10000.0
EPS = 1e-6
NEG_INF = -1e30
PAGE_SIZE = 128

LANES = 128
SUBLANES = 8
HD = 512
VMEM_LIMIT = 56 * 1024 * 1024

F32 = jnp.float32
BF16 = jnp.bfloat16

_C_Q, _C_KV, _Q_B, _K_B, _V_B, _Q_I, _G_A, _G_B, _SMALL, _D_IN_P = (
    0, 768, 1024, 1536, 2048, 2560, 3072, 4096, 5120, 5248)
_L_KI, _L_KPE, _L_WI = 0, 64, 96


def _dot(a, b):
    return jnp.dot(a, b, preferred_element_type=F32)


def _dot_t(a, b):
    return lax.dot_general(a, b, (((1,), (1,)), ((), ())), preferred_element_type=F32)


def _rms(x, g):
    ms = jnp.mean(x * x, axis=-1, keepdims=True)
    return x * lax.rsqrt(ms + EPS) * g


def _rope_tables(pos):
    pos = jnp.asarray(pos).astype(F32)[:, None]
    lane = np.arange(LANES)
    out = []
    for width in (64, 32):
        half = width // 2
        m = lane % width
        inv_freq = 1.0 / (ROPE_THETA ** (jnp.arange(half, dtype=F32) / half))
        ang = pos * inv_freq[m % half][None, :]
        c, s = jnp.cos(ang), jnp.sin(ang)
        out += [c, jnp.where(m >= half, s, 0.0), jnp.where(m < half, -s, 0.0)]
    return out


def _rope_blk(x, c, sa, sb, half):
    return x * c + pltpu.roll(x, half, 1) * sa + pltpu.roll(x, LANES - half, 1) * sb


def _proj_kernel(sample, x_ref, c64_ref, sa64_ref, sb64_ref, c32_ref, sa32_ref, sb32_ref,
                 g1_ref, gq_ref, gkv_ref, win_ref, wuq_ref, wkv_ref, *outs):
    if sample:
        (ckv_o, kb_o, vb_o, small_o, qlat_o, qp_o, qb_o, qi_o, gate_o) = outs
    else:
        (ckv_o, kb_o, vb_o, small_o, qn_o, qp_o, qb_o, qi_o, gate_o,
         kcat_o, vm_o, kbb_o, vbb_o, ki2_o) = outs
    hb = _rms(x_ref[0], g1_ref[...]).astype(BF16)
    c64, sa64, sb64 = c64_ref[...], sa64_ref[...], sb64_ref[...]
    c32, sa32, sb32 = c32_ref[...], sa32_ref[...], sb32_ref[...]

    def proj(a, b):
        return _dot(hb, win_ref[:, a:b])

    cq = _rms(proj(_C_Q, _C_KV), gq_ref[...]).astype(BF16)
    ckv = _rms(proj(_C_KV, _Q_B), gkv_ref[...])
    ckv_o[0] = ckv

    zq = proj(_Q_B, _K_B)
    zk = proj(_K_B, _V_B)
    zi = proj(_Q_I, _G_A)
    for blk in range(HD // LANES):
        sl = slice(blk * LANES, (blk + 1) * LANES)
        qb_o[0, :, sl] = (_rope_blk(zq[:, sl], c64, sa64, sb64, 32) * DSA_QSCALE).astype(BF16)
        qi_o[0, :, sl] = (_rope_blk(zi[:, sl], c64, sa64, sb64, 32) * IDX_SCALE).astype(BF16)
        kr = _rope_blk(zk[:, sl], c64, sa64, sb64, 32)
        if sample:
            kb_o[0, :, sl] = kr
        else:
            kb_o[0, sl, :] = kr.T
            kbb_o[0, :, sl] = kr.astype(BF16)
    zv = proj(_V_B, _Q_I)
    if sample:
        vb_o[0] = zv
    else:
        for blk in range(HD // LANES):
            sl = slice(blk * LANES, (blk + 1) * LANES)
            zvt = zv[:, sl].T
            vb_o[0, sl, :] = zvt
            vbb_o[0, sl, :] = zvt.astype(BF16)

    gate_o[0] = jax.nn.sigmoid(proj(_G_A, _SMALL)).astype(BF16)

    zs = proj(_SMALL, _D_IN_P)
    lane = lax.broadcasted_iota(jnp.int32, (1, LANES), 1)
    m_ki = (lane < _L_KPE).astype(F32)
    m_kpe = ((lane >= _L_KPE) & (lane < _L_WI)).astype(F32)
    m_wi = ((lane >= _L_WI) & (lane < _L_WI + IDX_HEADS)).astype(F32)
    small = (zs * (c64 * m_ki + c32 * m_kpe + (IDX_HEADS ** -0.5) * m_wi)
             + pltpu.roll(zs, 32, 1) * (sa64 * m_ki) + pltpu.roll(zs, 96, 1) * (sb64 * m_ki)
             + pltpu.roll(zs, 16, 1) * (sa32 * m_kpe) + pltpu.roll(zs, 112, 1) * (sb32 * m_kpe))
    small_o[0] = small
    if not sample:
        ki = small * m_ki
        ki2_o[0] = (ki + pltpu.roll(ki, 64, 1)).astype(BF16)
        kp = pltpu.roll(small * m_kpe, 64, 1)
        kpe4 = (kp + pltpu.roll(kp, 32, 1) + pltpu.roll(kp, 64, 1)
                + pltpu.roll(kp, 96, 1)).astype(BF16)
        for pair in range(MLA_HEADS // 2):
            kcat_o[0, :, (2 * pair + 1) * LANES:(2 * pair + 2) * LANES] = kpe4

    q = _dot(cq, wuq_ref[...])
    qn = (q[:, :HD] * MLA_QSCALE).astype(BF16)
    for blk in range(2):
        qpe = q[:, HD + blk * LANES:HD + (blk + 1) * LANES]
        qp_o[0, :, blk * LANES:(blk + 1) * LANES] = (
            _rope_blk(qpe, c32, sa32, sb32, 16) * MLA_QSCALE).astype(BF16)

    if sample:
        qlat_o[0] = _dot(qn, wkv_ref[...]).astype(BF16)
    else:
        qn_o[0] = qn
        kv = _dot(ckv.astype(BF16), wkv_ref[...])
        for pair in range(MLA_HEADS // 2):
            kcat_o[0, :, 2 * pair * LANES:(2 * pair + 1) * LANES] = (
                kv[:, pair * LANES:(pair + 1) * LANES].astype(BF16))
        for blk in range(HD // LANES):
            vm_o[0, blk * LANES:(blk + 1) * LANES, :] = (
                kv[:, HD + blk * LANES:HD + (blk + 1) * LANES].T.astype(BF16))


def _const_spec(shape):
    nd = len(shape)
    return pl.BlockSpec(shape, lambda *_: (0,) * nd, pipeline_mode=pl.Buffered(1))


def _proj_call(x, tables, g1, gq, gkv, win, wuq, wkv, tm, sample):
    nb, lp, d = x.shape
    grid = (nb, lp // tm)
    row = lambda w: pl.BlockSpec((1, tm, w), lambda b, j: (b, j, 0))
    tab = pl.BlockSpec((tm, LANES), lambda b, j: (j, 0))
    in_specs = ([row(d)] + [tab] * 6
                + [_const_spec(a.shape) for a in (g1, gq, gkv, win, wuq, wkv)])
    f32_w = [MLA_KV_LORA, HD, HD, LANES]
    if sample:
        bf_w = [MLA_HEADS * MLA_KV_LORA, 2 * LANES, HD, HD, 2048]
    else:
        bf_w = [HD, 2 * LANES, HD, HD, 2048, 2 * HD, HD, HD, HD, LANES]
    out_shape = ([jax.ShapeDtypeStruct((nb, lp, w), F32) for w in f32_w]
                 + [jax.ShapeDtypeStruct((nb, lp, w), BF16) for w in bf_w])
    out_specs = [row(w) for w in f32_w + bf_w]
    if not sample:
        n_f32 = len(f32_w)
        for i, dt in ((1, F32), (2, F32), (n_f32 + 6, BF16), (n_f32 + 8, BF16)):
            out_shape[i] = jax.ShapeDtypeStruct((nb, HD, lp), dt)
            out_specs[i] = pl.BlockSpec((1, HD, tm), lambda b, j: (b, 0, j))
    return pl.pallas_call(
        functools.partial(_proj_kernel, sample),
        grid=grid, in_specs=in_specs, out_specs=out_specs, out_shape=out_shape,
        compiler_params=pltpu.CompilerParams(
            dimension_semantics=("arbitrary", "arbitrary"), vmem_limit_bytes=VMEM_LIMIT),
        name="proj_sample" if sample else "proj_prompt",
    )(x, *tables, g1, gq, gkv, win, wuq, wkv)


_MAX_BISECT = 320
_BISECT_UNROLL = 4


def _chunk_ds(c, tk):
    start = c * tk
    return pl.ds(start if isinstance(start, int) else pl.multiple_of(start, tk), tk)


def _key_chunk(ref, c, tk, axis):
    ds = _chunk_ds(c, tk)
    return ref[ds, :] if axis == 0 else ref[:, ds]


def _key_fold(v, op, axis):
    if axis == 0:
        parts = [v[i * SUBLANES:(i + 1) * SUBLANES] for i in range(v.shape[0] // SUBLANES)]
    else:
        parts = [v[:, i * LANES:(i + 1) * LANES] for i in range(v.shape[1] // LANES)]
    while len(parts) > 1:
        parts = [op(a, b) for a, b in zip(parts[::2], parts[1::2])] + (
            [parts[-1]] if len(parts) % 2 else [])
    return parts[0]


def _key_index(c, tk, axis):
    shape = (tk, 1) if axis == 0 else (1, tk)
    return (lax.convert_element_type(c * tk, F32)
            + lax.broadcasted_iota(jnp.int32, shape, axis).astype(F32))


def _count(sc_ref, nck, tk, axis, pred):
    nq = sc_ref.shape[1 - axis]
    part = (SUBLANES, nq) if axis == 0 else (nq, LANES)

    def body(c, acc):
        v = jnp.where(pred(_key_chunk(sc_ref, c, tk, axis), c), 1.0, 0.0)
        return acc + _key_fold(v, jnp.add, axis)

    acc = lax.fori_loop(0, nck, body, jnp.zeros(part, F32))
    return jnp.sum(acc, axis=axis, keepdims=True)


def _select_threshold(sc_ref, nck, tk, k_eff, n_valid, axis):
    nq = sc_ref.shape[1 - axis]
    part = (SUBLANES, nq) if axis == 0 else (nq, LANES)
    big = -NEG_INF

    def mm_body(c, carry):
        mn, mx = carry
        x = _key_chunk(sc_ref, c, tk, axis)
        xv = jnp.where(x > 0.5 * NEG_INF, x, big)
        return (jnp.minimum(mn, _key_fold(xv, jnp.minimum, axis)),
                jnp.maximum(mx, _key_fold(x, jnp.maximum, axis)))

    mn, mx = lax.fori_loop(0, nck, mm_body, (jnp.full(part, big, F32),
                                             jnp.full(part, NEG_INF, F32)))
    lo0 = jnp.min(mn, axis=axis, keepdims=True)
    hi0 = jnp.max(mx, axis=axis, keepdims=True)
    hi0 = hi0 + jnp.maximum(jnp.abs(hi0), 1.0) * 1e-6

    def count_ge(thr):
        return _count(sc_ref, nck, tk, axis, lambda x, c: x >= thr)

    def all_rows(fin):
        return jnp.min(jnp.where(fin, 1.0, 0.0)) > 0.5

    c_gt0 = _count(sc_ref, nck, tk, axis, lambda x, c: x > 0.0)
    c_ge0 = count_ge(jnp.zeros_like(lo0))
    at_zero = (c_gt0 < k_eff) & (c_ge0 >= k_eff)
    above = c_gt0 >= k_eff
    inside = (lo0 < 0.0) & (hi0 > 0.0)
    lo_at_zero = at_zero | (above & inside)
    c_lo0 = jnp.where(lo_at_zero, c_ge0, n_valid)
    lo0, hi0 = (jnp.where(lo_at_zero, 0.0, lo0),
                jnp.where(at_zero | (~above & inside), 0.0, hi0))

    def cond(st):
        it, _, _, _, done = st
        return jnp.logical_and(it < _MAX_BISECT, jnp.logical_not(done))

    def body(st):
        it, lo, hi, c_lo, _ = st
        for _ in range(_BISECT_UNROLL):
            mid = 0.5 * lo + 0.5 * hi
            c = count_ge(mid)
            ge = c >= k_eff
            fin = (mid <= lo) | (mid >= hi)
            lo, hi, c_lo = jnp.where(ge, mid, lo), jnp.where(ge, hi, mid), jnp.where(ge, c, c_lo)
            fin = fin | (c_lo == k_eff)
        return it + _BISECT_UNROLL, lo, hi, c_lo, all_rows(fin)

    fin0 = (c_lo0 == k_eff) | (lo0 >= hi0)
    _, t, _, c_t, _ = lax.while_loop(
        cond, body, (jnp.int32(0), lo0, hi0, c_lo0, all_rows(fin0)))

    tie_rows = c_t > k_eff

    @pl.when(jnp.max(jnp.where(tie_rows, 1.0, 0.0)) > 0.5)
    def _():
        n_keys = sc_ref.shape[axis]
        need = k_eff - _count(sc_ref, nck, tk, axis, lambda x, c: x > t)

        def tie_count(m):
            return _count(sc_ref, nck, tk, axis,
                          lambda x, c: (x == t) & (_key_index(c, tk, axis) <= m))

        def ibody(_, st):
            lo_i, hi_i = st
            mid = jnp.floor(0.5 * (lo_i + hi_i))
            ok = tie_count(mid) >= need
            return jnp.where(ok, lo_i, mid), jnp.where(ok, mid, hi_i)

        steps = int(np.ceil(np.log2(n_keys))) + 1
        _, m_idx = lax.fori_loop(0, steps, ibody,
                                 (jnp.full(t.shape, -1.0, F32),
                                  jnp.full(t.shape, float(n_keys - 1), F32)))

        def fix(c, carry):
            x = _key_chunk(sc_ref, c, tk, axis)
            drop = (x == t) & (_key_index(c, tk, axis) > m_idx) & tie_rows
            ds = _chunk_ds(c, tk)
            if axis == 0:
                sc_ref[ds, :] = jnp.where(drop, NEG_INF, x)
            else:
                sc_ref[:, ds] = jnp.where(drop, NEG_INF, x)
            return carry

        lax.fori_loop(0, nck, fix, 0)

    return t


def _lane_blocks(s):
    return [s[:, i * LANES:(i + 1) * LANES] for i in range(s.shape[1] // LANES)]


def _scale_cols(alpha, x):
    return jnp.concatenate([alpha * b for b in _lane_blocks(x)], axis=1)


def _online_update(m_ref, l_ref, acc_ref, idx, blocks, pv):
    m = m_ref[idx]
    bm = blocks[0]
    for b in blocks[1:]:
        bm = jnp.maximum(bm, b)
    m_new = jnp.maximum(m, jnp.max(bm, axis=1, keepdims=True))
    alpha = jnp.exp2(m - m_new)
    ps = [jnp.exp2(b - m_new) for b in blocks]
    rs = ps[0]
    for p in ps[1:]:
        rs = rs + p
    l_ref[idx] = alpha * l_ref[idx] + jnp.sum(rs, axis=1, keepdims=True)
    acc_ref[idx] = _scale_cols(alpha, acc_ref[idx]) + pv([p.astype(BF16) for p in ps])
    m_ref[idx] = m_new


def _reset_state(m_ref, l_ref, acc_ref):
    m_ref[...] = jnp.full_like(m_ref, NEG_INF)
    l_ref[...] = jnp.zeros_like(l_ref)
    acc_ref[...] = jnp.zeros_like(acc_ref)


def _softmax_keys_major(m_ref, l_ref, idx, s):
    m = m_ref[idx]
    m_new = jnp.maximum(m, jnp.max(_key_fold(s, jnp.maximum, 0), axis=0, keepdims=True))
    alpha = jnp.exp2(m - m_new)
    p = jnp.exp2(s - m_new)
    l_ref[idx] = alpha * l_ref[idx] + jnp.sum(_key_fold(p, jnp.add, 0), axis=0, keepdims=True)
    m_ref[idx] = m_new
    return alpha, p.astype(BF16)


def _attn_kernel(tq, tk, n_keep, topk,
                 qn_ref, qp_ref, qb_ref, qi_ref, small_ref,
                 kcat_ref, vm_ref, kb_ref, vb_ref, ki2_ref,
                 mla_ref, dsa_ref, sc_ref, qc_ref, qd_ref, qx_ref, w_ref, m_ref, l_ref, acc_ref):
    j = pl.program_id(1)
    nh = MLA_HEADS
    n_pairs = nh // 2

    @pl.when(j >= n_keep)
    def _():
        mla_ref[...] = jnp.zeros_like(mla_ref)
        dsa_ref[...] = jnp.zeros_like(dsa_ref)

    @pl.when(j < n_keep)
    def _():
        nck = (j * tq + tq - 1) // tk + 1
        last = nck - 1
        lane = lax.broadcasted_iota(jnp.int32, (1, LANES), 1)
        q_row = j * tq + lax.broadcasted_iota(jnp.int32, (1, tq), 1)
        k_row = lax.broadcasted_iota(jnp.int32, (tk, 1), 0)
        zero_b = jnp.zeros((tq, LANES), BF16)

        def ksl(c):
            return _chunk_ds(c, tk)

        def pair_sl(pair):
            return slice(pair * LANES, (pair + 1) * LANES)

        def two(x):
            return jnp.concatenate([x, x], axis=1)

        def write_heads(out_ref):
            half = LANES // 2
            for pair in range(n_pairs):
                o = acc_ref[pair] / l_ref[pair]
                o = jnp.concatenate([o[:half, :tq], o[half:, tq:]], axis=0)
                out_ref[0, :, pair_sl(pair)] = o.T.astype(BF16)

        small = small_ref[0]
        for h in range(nh):
            pair, sub = divmod(h, 2)
            grp, gsub = divmod(h, 4)
            rows = slice(sub * tq, (sub + 1) * tq)
            own = (lane >= 64) == bool(sub)
            qc_ref[pair, rows, :LANES] = jnp.where(own, qn_ref[0, :, pair_sl(pair)], zero_b)
            qc_ref[pair, rows, LANES:] = jnp.where((lane // 32) == gsub,
                                                   qp_ref[0, :, pair_sl(grp)], zero_b)
            qd_ref[pair, rows, :] = jnp.where(own, qb_ref[0, :, pair_sl(pair)], zero_b)
            qx_ref[h * tq:(h + 1) * tq, :] = jnp.where(own, qi_ref[0, :, pair_sl(pair)], zero_b)
        w_ref[...] = small.T[_L_WI:_L_WI + IDX_HEADS]

        def attend(c, score_fn, fix_fn, vt_ref):
            ks = ksl(c)
            s = {p: score_fn(p, ks) for p in range(min(2, n_pairs))}
            for p in range(n_pairs):
                alpha, pr = _softmax_keys_major(m_ref, l_ref, p, fix_fn(s.pop(p)))
                if p + 2 < n_pairs:
                    s[p + 2] = score_fn(p + 2, ks)
                acc_ref[p] = alpha * acc_ref[p] + _dot(vt_ref[0, pair_sl(p), ks], pr)

        _reset_state(m_ref, l_ref, acc_ref)

        def mla_scores(p, ks):
            return _dot_t(kcat_ref[0, ks, 2 * p * LANES:(2 * p + 2) * LANES], qc_ref[p])

        def mla_idx_chunk(c, masked):
            ks = ksl(c)
            r = _dot_t(ki2_ref[0, ks, :], qx_ref[...])
            sc = None
            for h in range(IDX_HEADS):
                term = jnp.maximum(r[:, h * tq:(h + 1) * tq], 0.0) * w_ref[h:h + 1, :]
                sc = term if sc is None else sc + term
            if masked:
                sc = jnp.where((c * tk + k_row) <= q_row, sc, NEG_INF)
                vis2 = (c * tk + k_row) <= two(q_row)
                fix = lambda s: jnp.where(vis2, s, NEG_INF)
            else:
                fix = lambda s: s
            sc_ref[ks, :] = sc
            attend(c, mla_scores, fix, vm_ref)

        def mla_idx_step(c, carry):
            mla_idx_chunk(c, False)
            return carry

        lax.fori_loop(0, last, mla_idx_step, 0)
        mla_idx_chunk(last, True)
        write_heads(mla_ref)

        n_valid = (q_row + 1).astype(F32)
        k_eff = jnp.minimum(n_valid, float(topk))
        t2 = two(_select_threshold(sc_ref, nck, tk, k_eff, n_valid, 0))

        _reset_state(m_ref, l_ref, acc_ref)

        def dsa_scores(p, ks):
            return _dot_t(kb_ref[0, ks, pair_sl(p)], qd_ref[p])

        def dsa_step(c, carry):
            sel = two(sc_ref[ksl(c), :]) >= t2
            attend(c, dsa_scores, lambda s: jnp.where(sel, s, NEG_INF), vb_ref)
            return carry

        lax.fori_loop(0, nck, dsa_step, 0)
        write_heads(dsa_ref)


def _attn_call(qn, qp, qb, qi, small, kcat, vm, kbb, vbb, ki2, l_valid, tq, tk):
    nb, lp, _ = qn.shape
    n_keep = -(-l_valid // tq)
    topk = min(TOPK_MAX, l_valid // 4)
    n_pairs = MLA_HEADS // 2
    qrow = lambda w: pl.BlockSpec((1, tq, w), lambda b, j: (b, j, 0))
    krow = lambda w: pl.BlockSpec((1, lp, w), lambda b, j: (b, 0, 0))
    vcol = pl.BlockSpec((1, HD, lp), lambda b, j: (b, 0, 0))
    in_specs = [qrow(HD), qrow(2 * LANES), qrow(HD), qrow(HD), qrow(LANES),
                krow(2 * HD), vcol, krow(HD), vcol, krow(LANES)]
    return pl.pallas_call(
        functools.partial(_attn_kernel, tq, tk, n_keep, topk),
        grid=(nb, lp // tq), in_specs=in_specs,
        out_specs=[qrow(HD), qrow(HD)],
        out_shape=[jax.ShapeDtypeStruct((nb, lp, HD), BF16)] * 2,
        scratch_shapes=[pltpu.VMEM((lp, tq), F32),
                        pltpu.VMEM((n_pairs, 2 * tq, 2 * LANES), BF16),
                        pltpu.VMEM((n_pairs, 2 * tq, LANES), BF16),
                        pltpu.VMEM((IDX_HEADS * tq, LANES), BF16),
                        pltpu.VMEM((IDX_HEADS, tq), F32),
                        pltpu.VMEM((n_pairs, 1, 2 * tq), F32),
                        pltpu.VMEM((n_pairs, 1, 2 * tq), F32),
                        pltpu.VMEM((n_pairs, LANES, 2 * tq), F32)],
        compiler_params=pltpu.CompilerParams(
            dimension_semantics=("arbitrary", "arbitrary"), vmem_limit_bytes=VMEM_LIMIT),
        name="attn_prompt",
    )(qn, qp, qb, qi, small, kcat, vm, kbb, vbb, ki2)


def _merge_kernel(x_ref, mla_ref, dsa_ref, gate_ref, wa_ref, wb_ref, wo_ref, g2_ref,
                  x2_ref, h2_ref):
    d = x_ref.shape[-1]
    a = _dot(mla_ref[0], wa_ref[...])
    b = _dot(dsa_ref[0], wb_ref[...])
    g = gate_ref[0]
    o = g[:, :d].astype(F32) * a + g[:, d:].astype(F32) * b
    x2 = x_ref[0] + _dot(o.astype(BF16), wo_ref[...])
    x2_ref[0] = x2
    h2_ref[0] = _rms(x2, g2_ref[...]).astype(BF16)


def _merge_call(x, mla, dsa, gate, wa, wb, wo, g2, tm, name):
    nb, lp, d = x.shape
    row = lambda w: pl.BlockSpec((1, tm, w), lambda b, j: (b, j, 0))
    return pl.pallas_call(
        _merge_kernel, grid=(nb, lp // tm),
        in_specs=[row(d), row(HD), row(HD), row(2 * d)]
        + [_const_spec(a.shape) for a in (wa, wb, wo, g2)],
        out_specs=[row(d), row(d)],
        out_shape=[jax.ShapeDtypeStruct((nb, lp, d), F32), jax.ShapeDtypeStruct((nb, lp, d), BF16)],
        compiler_params=pltpu.CompilerParams(
            dimension_semantics=("arbitrary", "arbitrary"), vmem_limit_bytes=VMEM_LIMIT),
        name=name,
    )(x, mla, dsa, gate, wa, wb, wo, g2)


_FF_CHUNK = 512
_FF_AHEAD = 1


def _ffn_kernel(seq_rows, tail_tile, tail_off, h_ref, x_ref, s1_ref, s2_ref,
                wg_ref, wu_ref, wd_ref, cw_ref, cb_ref, gf_ref, y_ref, tail_ref, prev_ref):
    j = pl.program_id(1)
    tm = h_ref.shape[1]
    d_ff = wg_ref.shape[1]
    h = h_ref[0]
    row = lax.broadcasted_iota(jnp.int32, (tm, 1), 0)
    if seq_rows is None:
        @pl.when(j == 0)
        def _():
            prev_ref[...] = jnp.zeros_like(prev_ref)
        first1, first2 = row < 1, row < 2
    else:
        first1, first2 = (row % seq_rows) < 1, (row % seq_rows) < 2

    acc = jnp.zeros((tm, x_ref.shape[-1]), F32)
    chunks = [slice(c0, min(c0 + _FF_CHUNK, d_ff)) for c0 in range(0, d_ff, _FF_CHUNK)]

    def up(sl):
        return _dot(h, wg_ref[:, sl]), _dot(h, wu_ref[:, sl])

    ahead = [up(sl) for sl in chunks[:_FF_AHEAD]]
    for i, sl in enumerate(chunks):
        g, u = ahead.pop(0)
        if i + _FF_AHEAD < len(chunks):
            ahead.append(up(chunks[i + _FF_AHEAD]))
        if seq_rows is None:
            p = prev_ref[:, sl]
            hist1 = jnp.broadcast_to(p[SUBLANES - 1:SUBLANES], g.shape)
            hist2 = jnp.where(row < 1, jnp.broadcast_to(p[SUBLANES - 2:SUBLANES - 1], g.shape),
                              hist1)
            prev_ref[:, sl] = g[tm - SUBLANES:]
        else:
            hist1, hist2 = s1_ref[0, :, sl], s2_ref[0, :, sl]
        g1 = jnp.where(first1, hist1, pltpu.roll(g, 1, 0))
        g2 = jnp.where(first2, hist2, pltpu.roll(g, 2, 0))
        cw = cw_ref[:, sl]
        gc = cb_ref[:, sl] + cw[0:1] * g2 + cw[1:2] * g1 + cw[2:3] * g
        act = (gc * jax.nn.sigmoid(gc) * u).astype(BF16)
        acc = acc + _dot(act, wd_ref[sl, :])
        if seq_rows is None:
            @pl.when(j == tail_tile)
            def _(g=g, sl=sl):
                tail_ref[0, :, sl] = g[tail_off:tail_off + SUBLANES]
        else:
            tail_ref[0, :, sl] = g
    y_ref[0] = _rms(x_ref[0] + acc, gf_ref[...])


def _ffn_call(h2, x2, s1, s2, wg, wu, wd, cw, cb, gf, tm, l_valid, sample, name):
    nb, lp, d = x2.shape
    d_ff = wg.shape[1]
    row = lambda w: pl.BlockSpec((1, tm, w), lambda b, j: (b, j, 0))
    if sample:
        seq_rows, tail_tile, tail_off = l_valid, 0, 0
        tail_shape, tail_spec = (nb, lp, d_ff), row(d_ff)
        s_spec = row(d_ff)
    else:
        seq_rows = None
        tail_tile, tail_off = divmod(l_valid - SUBLANES, tm)
        tail_shape = (nb, SUBLANES, d_ff)
        tail_spec = pl.BlockSpec((1, SUBLANES, d_ff), lambda b, j: (b, 0, 0))
        s_spec = pl.BlockSpec((1, SUBLANES, d_ff), lambda b, j: (0, 0, 0))
    return pl.pallas_call(
        functools.partial(_ffn_kernel, seq_rows, tail_tile, tail_off),
        grid=(nb, lp // tm),
        in_specs=[row(d), row(d), s_spec, s_spec]
        + [_const_spec(a.shape) for a in (wg, wu, wd, cw, cb, gf)],
        out_specs=[row(d), tail_spec],
        out_shape=[jax.ShapeDtypeStruct((nb, lp, d), F32), jax.ShapeDtypeStruct(tail_shape, F32)],
        scratch_shapes=[pltpu.VMEM((SUBLANES, d_ff), F32)],
        compiler_params=pltpu.CompilerParams(
            dimension_semantics=("arbitrary", "arbitrary"), vmem_limit_bytes=VMEM_LIMIT),
        name=name,
    )(h2, x2, s1, s2, wg, wu, wd, cw, cb, gf)


def _head_rows_mask(rows_per_head, n_heads, width_per_head):
    r = lax.broadcasted_iota(jnp.int32, (n_heads * rows_per_head, 1), 0) // rows_per_head
    c = lax.broadcasted_iota(jnp.int32, (1, n_heads * width_per_head), 1) // width_per_head
    return r == c


def _diag_heads(full, t, n_heads, width):
    col_head = lax.broadcasted_iota(jnp.int32, (1, n_heads * width), 1) // width
    out = jnp.zeros((t, n_heads * width), F32)
    for h in range(n_heads):
        out = jnp.where(col_head == h, full[h * t:(h + 1) * t], out)
    return out


def _smla_kernel(npp, t_new, pt_ref, qabs_ref, qpe_ref, qi_ref, wrow_ref,
                 ckvn_ref, kpen_ref, ikn_ref, wuv_ref, *rest):
    ckv_pages = rest[:npp]
    kpe_pages = rest[npp:2 * npp]
    ik_pages = rest[2 * npp:3 * npp]
    mla_ref, scp_ref, scn_ref, m_ref, l_ref, acc_ref, ck_ref, kpe_ref, ik_ref = rest[3 * npp:]
    c = pl.program_id(1)
    rows = qabs_ref.shape[1]
    qabs, qpe, qi, wrow = qabs_ref[0], qpe_ref[0], qi_ref[0], wrow_ref[0]

    @pl.when(c == 0)
    def _():
        _reset_state(m_ref, l_ref, acc_ref)

    def update(s, vals):
        _online_update(m_ref, l_ref, acc_ref, 0, _lane_blocks(s),
                       lambda ps: _dot(jnp.concatenate(ps, axis=1), vals))

    def idx_score(s):
        s = _scale_cols(wrow, jnp.maximum(s, 0.0))
        out = s[:t_new]
        for h in range(1, IDX_HEADS):
            out = out + s[h * t_new:(h + 1) * t_new]
        return out

    for i in range(npp):
        ck_ref[i * PAGE_SIZE:(i + 1) * PAGE_SIZE, :] = ckv_pages[i][...].astype(BF16)
        kpe_ref[:, i * PAGE_SIZE:(i + 1) * PAGE_SIZE] = kpe_pages[i][...].astype(BF16)
        ik_ref[:, i * PAGE_SIZE:(i + 1) * PAGE_SIZE] = ik_pages[i][...].astype(BF16)
    ck_all = ck_ref[...]
    update(_dot_t(qabs, ck_all) + _dot(qpe, kpe_ref[...]), ck_all)
    scp_ref[0] = idx_score(_dot(qi, ik_ref[...]))

    @pl.when(c == pl.num_programs(1) - 1)
    def _():
        ck = ckvn_ref[0]
        tok = lax.broadcasted_iota(jnp.int32, (rows, 1), 0) % t_new
        key = lax.broadcasted_iota(jnp.int32, (1, PAGE_SIZE), 1)
        vis = key <= tok
        s = _dot_t(qabs, ck) + _dot_t(qpe, kpen_ref[0])
        update(jnp.where(vis, s, NEG_INF), ck)
        scn_ref[0] = jnp.where(vis[:t_new], idx_score(_dot_t(qi, ikn_ref[0])), NEG_INF)
        o_lat = _scale_cols(1.0 / l_ref[0], acc_ref[0]).astype(BF16)
        mla_ref[0] = _diag_heads(_dot(o_lat, wuv_ref[...]), t_new, MLA_HEADS, MLA_V).astype(BF16)


def _sdsa_kernel(npp, t_new, topk, past_len, sel_tk, pt_ref, qbd_ref, scp_ref, scn_ref, kbn_ref,
                 vbn_ref, *rest):
    k_pages = rest[:npp]
    v_pages = rest[npp:2 * npp]
    dsa_ref, sc_ref, t_ref, m_ref, l_ref, acc_ref, kt_ref, vt_ref = rest[2 * npp:]
    c = pl.program_id(1)
    rows = qbd_ref.shape[1]
    qbd = qbd_ref[0]
    step_keys = npp * PAGE_SIZE

    @pl.when(c == 0)
    def _():
        _reset_state(m_ref, l_ref, acc_ref)
        sc_ref[:, :past_len] = scp_ref[0]
        sc_ref[:, past_len:past_len + PAGE_SIZE] = scn_ref[0]
        if sel_tk > PAGE_SIZE:
            sc_ref[:, past_len + PAGE_SIZE:] = jnp.full((t_new, sel_tk - PAGE_SIZE), NEG_INF, F32)
        tok = lax.broadcasted_iota(jnp.int32, (t_new, 1), 0)
        n_valid = (past_len + 1 + tok).astype(F32)
        k_eff = jnp.minimum(n_valid, float(topk))
        t_ref[...] = _select_threshold(sc_ref, (past_len + sel_tk) // sel_tk, sel_tk,
                                       k_eff, n_valid, 1)

    t = t_ref[...]

    def masked(s, sc):
        pen = jnp.where(sc >= t, 0.0, NEG_INF)
        sel = jnp.concatenate([pen] * DSA_HEADS, axis=0) == 0.0
        return jnp.where(sel, s, NEG_INF)

    def update(s, pv_dot, vals):
        _online_update(m_ref, l_ref, acc_ref, 0, _lane_blocks(s),
                       lambda ps: pv_dot(jnp.concatenate(ps, axis=1), vals))

    for i in range(npp):
        kt_ref[:, i * PAGE_SIZE:(i + 1) * PAGE_SIZE] = k_pages[i][...].astype(BF16)
        vt_ref[:, i * PAGE_SIZE:(i + 1) * PAGE_SIZE] = v_pages[i][...].astype(BF16)
    start = c * step_keys
    start = start if isinstance(start, int) else pl.multiple_of(start, step_keys)
    update(masked(_dot(qbd, kt_ref[...]), sc_ref[:, pl.ds(start, step_keys)]), _dot_t, vt_ref[...])

    @pl.when(c == pl.num_programs(1) - 1)
    def _():
        s = masked(_dot_t(qbd, kbn_ref[0]), sc_ref[:, past_len:past_len + PAGE_SIZE])
        update(s, _dot, vbn_ref[0])
        o = _scale_cols(1.0 / l_ref[0], acc_ref[0])
        dsa_ref[0] = _diag_heads(o, t_new, DSA_HEADS, DSA_HEAD_DIM).astype(BF16)


def _pages_per_step(n_pages, want):
    p = min(want, n_pages)
    while n_pages % p:
        p -= 1
    return p


def _page_specs(npp, rows, width):
    return [pl.BlockSpec((None, rows, width),
                         functools.partial(lambda i, b, c, pt: (pt[b, c * npp + i], 0, 0), i))
            for i in range(npp)]


def _smla_call(page_table, qabs, qpe, qi, wrow, ckvn, kpen, ikn, wuv, pool_ckv, pool_kpe, pool_ik,
               t_new):
    nb, n_pages = page_table.shape
    npp = _pages_per_step(n_pages, 16)
    rows = qabs.shape[1]
    past_len = n_pages * PAGE_SIZE
    per_b = lambda shape: pl.BlockSpec((1,) + shape, lambda b, c, pt: (b,) + (0,) * len(shape))
    in_specs = ([per_b((rows, MLA_KV_LORA)), per_b((rows, MLA_ROPE)), per_b((rows, IDX_DIM)),
                 per_b((rows, LANES)), per_b((PAGE_SIZE, MLA_KV_LORA)),
                 per_b((PAGE_SIZE, MLA_ROPE)), per_b((PAGE_SIZE, IDX_DIM)),
                 pl.BlockSpec(wuv.shape, lambda b, c, pt: (0, 0))]
                + _page_specs(npp, PAGE_SIZE, MLA_KV_LORA) + _page_specs(npp, MLA_ROPE, PAGE_SIZE)
                + _page_specs(npp, IDX_DIM, PAGE_SIZE))
    out_specs = [per_b((t_new, HD)),
                 pl.BlockSpec((1, t_new, npp * PAGE_SIZE), lambda b, c, pt: (b, 0, c)),
                 per_b((t_new, PAGE_SIZE))]
    out_shape = [jax.ShapeDtypeStruct((nb, t_new, HD), BF16),
                 jax.ShapeDtypeStruct((nb, t_new, past_len), F32),
                 jax.ShapeDtypeStruct((nb, t_new, PAGE_SIZE), F32)]
    grid_spec = pltpu.PrefetchScalarGridSpec(
        num_scalar_prefetch=1, grid=(nb, n_pages // npp), in_specs=in_specs, out_specs=out_specs,
        scratch_shapes=[pltpu.VMEM((1, rows, LANES), F32), pltpu.VMEM((1, rows, LANES), F32),
                        pltpu.VMEM((1, rows, MLA_KV_LORA), F32),
                        pltpu.VMEM((npp * PAGE_SIZE, MLA_KV_LORA), BF16),
                        pltpu.VMEM((MLA_ROPE, npp * PAGE_SIZE), BF16),
                        pltpu.VMEM((IDX_DIM, npp * PAGE_SIZE), BF16)])
    return pl.pallas_call(
        functools.partial(_smla_kernel, npp, t_new), grid_spec=grid_spec, out_shape=out_shape,
        compiler_params=pltpu.CompilerParams(
            dimension_semantics=("arbitrary", "arbitrary"), vmem_limit_bytes=VMEM_LIMIT),
        name="sample_mla",
    )(page_table, qabs, qpe, qi, wrow, ckvn, kpen, ikn, wuv,
      *([pool_ckv] * npp), *([pool_kpe] * npp), *([pool_ik] * npp))


def _sdsa_call(page_table, qbd, scp, scn, kbn, vbn, pool_k, pool_v, t_new):
    nb, n_pages = page_table.shape
    npp = _pages_per_step(n_pages, 16)
    rows = qbd.shape[1]
    past_len = n_pages * PAGE_SIZE
    topk = min(TOPK_MAX, (past_len + t_new) // 4)
    sel_tk = PAGE_SIZE
    while sel_tk < 2048 and past_len % (2 * sel_tk) == 0:
        sel_tk *= 2
    per_b = lambda shape: pl.BlockSpec((1,) + shape, lambda b, c, pt: (b,) + (0,) * len(shape))
    in_specs = ([per_b((rows, HD)), per_b((t_new, past_len)), per_b((t_new, PAGE_SIZE)),
                 per_b((PAGE_SIZE, HD)), per_b((PAGE_SIZE, HD))]
                + _page_specs(npp, HD, PAGE_SIZE) + _page_specs(npp, HD, PAGE_SIZE))
    grid_spec = pltpu.PrefetchScalarGridSpec(
        num_scalar_prefetch=1, grid=(nb, n_pages // npp), in_specs=in_specs,
        out_specs=[per_b((t_new, HD))],
        scratch_shapes=[pltpu.VMEM((t_new, past_len + sel_tk), F32),
                        pltpu.VMEM((t_new, 1), F32),
                        pltpu.VMEM((1, rows, LANES), F32), pltpu.VMEM((1, rows, LANES), F32),
                        pltpu.VMEM((1, rows, HD), F32),
                        pltpu.VMEM((HD, npp * PAGE_SIZE), BF16),
                        pltpu.VMEM((HD, npp * PAGE_SIZE), BF16)])
    return pl.pallas_call(
        functools.partial(_sdsa_kernel, npp, t_new, topk, past_len, sel_tk), grid_spec=grid_spec,
        out_shape=[jax.ShapeDtypeStruct((nb, t_new, HD), BF16)],
        compiler_params=pltpu.CompilerParams(
            dimension_semantics=("arbitrary", "arbitrary"), vmem_limit_bytes=VMEM_LIMIT),
        name="sample_dsa",
    )(page_table, qbd, scp, scn, kbn, vbn, *([pool_k] * npp), *([pool_v] * npp))[0]


def _pick_tile(n, candidates):
    for c in candidates:
        if n % c == 0:
            return c
    return n


def _pad_rows(a, rows):
    return jnp.pad(a, ((0, 0), (0, rows - a.shape[1]), (0, 0)))


def _head_major(a, nb, t, heads):
    w = a.shape[-1] // heads
    return a.reshape(nb, t, heads, w).transpose(0, 2, 1, 3).reshape(nb, heads * t, w)


def kernel(x_prompt, x_sample, cache_mla_ckv, cache_mla_kpe, cache_dsa_k, cache_dsa_v, cache_idx_k,
           state_ffn_conv, page_table, meta_tokens, norm1_g, w_in, g_q, g_kv, w_uq, w_uk, w_uv,
           w_br_a, w_br_b, w_o, norm2_g, w_ffn_g, w_ffn_u, ffn_conv_w, ffn_conv_b, w_ffn_d, final_g):
    depth = w_in.shape[0]
    assert depth == 1
    nb, seq, d = x_prompt.shape
    nsb, t_new, _ = x_sample.shape
    n_pages = page_table.shape[1]
    past_len = n_pages * PAGE_SIZE
    l_valid = N_META + seq
    tq, tk = 256, 256
    lp = -(-l_valid // tk) * tk
    assert l_valid % SUBLANES == 0 and t_new % SUBLANES == 0 and t_new <= PAGE_SIZE
    d_ff = w_ffn_g.shape[-1]
    l = 0

    wi = w_in[l]
    cuts = np.cumsum([MLA_Q_LORA, MLA_KV_LORA, MLA_ROPE, HD, HD, HD, HD, IDX_DIM, IDX_HEADS, d, d])
    c_q, c_kv, k_pe, q_b, k_b, v_b, q_i, k_i, w_i, g_a, g_b = jnp.split(wi, cuts[:-1], axis=1)
    pad = jnp.zeros((d, LANES - IDX_DIM - MLA_ROPE - IDX_HEADS), wi.dtype)
    win = jnp.concatenate([c_q, c_kv, q_b, k_b, v_b, q_i, g_a, g_b, k_i, k_pe, w_i, pad],
                          axis=1).astype(BF16)
    wuq = jnp.concatenate([w_uq[l][:, :, :MLA_NOPE].reshape(MLA_Q_LORA, -1),
                           w_uq[l][:, :, MLA_NOPE:].reshape(MLA_Q_LORA, -1)], axis=1).astype(BF16)
    wuk2 = w_uk[l].reshape(MLA_KV_LORA, HD)
    wuv2 = w_uv[l].reshape(MLA_KV_LORA, HD)
    wkv_p = jnp.concatenate([wuk2, wuv2], axis=1).astype(BF16)
    ukt = w_uk[l].transpose(1, 2, 0)
    eye = jnp.eye(MLA_HEADS, dtype=ukt.dtype)
    wuk_bd = (ukt[:, :, None, :] * eye[:, None, :, None]).reshape(HD, MLA_HEADS * MLA_KV_LORA)
    wuk_bd = wuk_bd.astype(BF16)
    g1 = norm1_g[l][None]
    gq = g_q[l][None]
    gkv = g_kv[l][None]
    g2 = norm2_g[l][None]
    gf = final_g[None]
    wa, wb, wo = w_br_a[l].astype(BF16), w_br_b[l].astype(BF16), w_o[l].astype(BF16)
    wg, wu, wd = w_ffn_g[l].astype(BF16), w_ffn_u[l].astype(BF16), w_ffn_d[l].astype(BF16)
    cw, cb = ffn_conv_w[l], ffn_conv_b[l][None]

    meta = jnp.broadcast_to(meta_tokens[None].astype(x_prompt.dtype), (nb, N_META, d))
    xp = jnp.concatenate([meta, x_prompt, jnp.zeros((nb, lp - l_valid, d), x_prompt.dtype)], axis=1)
    tm = _pick_tile(lp, (384, 256, 128))
    tabs_p = _rope_tables(np.arange(lp))
    (ckv_p, kb_p, vb_p, small_p, qn, qp, qb, qi, gate_p, kcat, vm, kbb, vbb, ki2) = _proj_call(
        xp, tabs_p, g1, gq, gkv, win, wuq, wkv_p, tm, sample=False)
    mla_p, dsa_p = _attn_call(qn, qp, qb, qi, small_p, kcat, vm, kbb, vbb, ki2, l_valid, tq, tk)
    x2_p, h2_p = _merge_call(xp, mla_p, dsa_p, gate_p, wa, wb, wo, g2, tm, "merge_prompt")
    zstate = jnp.zeros((1, SUBLANES, d_ff), F32)
    tm_ffn = _pick_tile(lp, (768, 384, 256, 128))
    y_p, tail_p = _ffn_call(h2_p, x2_p, zstate, zstate, wg, wu, wd, cw, cb, gf, tm_ffn, l_valid,
                            False, "ffn_prompt")

    ns = nsb * t_new
    xs = x_sample.reshape(1, ns, d)
    tabs_s = _rope_tables(past_len + (np.arange(ns) % t_new))
    (ckv_s, kb_s, vb_s, small_s, qlat, qp_s, qb_s, qi_s, gate_s) = _proj_call(
        xs, tabs_s, g1, gq, gkv, win, wuq, wuk_bd, ns, sample=True)
    qabs = _head_major(qlat, nsb, t_new, MLA_HEADS)
    qpe_r = _head_major(qp_s, nsb, t_new, MLA_HEADS)
    qi_r = _head_major(qi_s, nsb, t_new, IDX_HEADS)
    w_rows = small_s[0, :, _L_WI:_L_WI + IDX_HEADS].reshape(nsb, t_new, IDX_HEADS)
    w_rows = jnp.broadcast_to(w_rows.transpose(0, 2, 1).reshape(nsb, IDX_HEADS * t_new, 1),
                              (nsb, IDX_HEADS * t_new, LANES))
    qb_r = _head_major(qb_s, nsb, t_new, DSA_HEADS)
    qbd = jnp.where(_np_head_mask(t_new), jnp.tile(qb_r, (1, 1, DSA_HEADS)), jnp.zeros((), BF16))
    new_rows = lambda a: _pad_rows(a.reshape(nsb, t_new, -1), PAGE_SIZE).astype(BF16)
    ckvn = new_rows(ckv_s)
    kpen = new_rows(small_s[..., _L_KPE:_L_KPE + MLA_ROPE])
    ikn = new_rows(small_s[..., _L_KI:_L_KI + IDX_DIM])
    kbn, vbn = new_rows(kb_s), new_rows(vb_s)
    n_pool = cache_dsa_k.shape[1]
    keys_minor = lambda pool: jnp.moveaxis(pool[l], 1, -1).reshape(n_pool, -1, PAGE_SIZE)
    mla_s, scp, scn = _smla_call(page_table, qabs, qpe_r, qi_r, w_rows, ckvn, kpen, ikn,
                                 wuv2.astype(BF16), cache_mla_ckv[l], keys_minor(cache_mla_kpe),
                                 keys_minor(cache_idx_k), t_new)
    dsa_s = _sdsa_call(page_table, qbd, scp, scn, kbn, vbn,
                       keys_minor(cache_dsa_k), keys_minor(cache_dsa_v), t_new)
    x2_s, h2_s = _merge_call(xs, mla_s.reshape(1, ns, HD), dsa_s.reshape(1, ns, HD), gate_s,
                             wa, wb, wo, g2, ns, "merge_sample")
    st = state_ffn_conv[l]
    zrow = jnp.zeros((nsb, 1, d_ff), st.dtype)
    s1 = jnp.concatenate([st[:, 1:2]] + [zrow] * (t_new - 1), axis=1).reshape(1, ns, d_ff)
    s2 = jnp.concatenate([st[:, 0:1], st[:, 1:2]] + [zrow] * (t_new - 2), axis=1).reshape(1, ns, d_ff)
    y_s, tail_s = _ffn_call(h2_s, x2_s, s1, s2, wg, wu, wd, cw, cb, gf, ns, t_new, True,
                            "ffn_sample")

    y_prompt = y_p[:, N_META:l_valid]
    y_sample = y_s.reshape(nsb, t_new, d)
    cut = lambda a: a[:, :l_valid]
    new_ckv_p = cut(ckv_p)[None]
    new_kpe_p = cut(small_p)[..., _L_KPE:_L_KPE + MLA_ROPE][None]
    pos_last = lambda a: jnp.moveaxis(
        a[:, :, :l_valid].reshape(nb, DSA_HEADS, DSA_HEAD_DIM, l_valid), -1, 1)[None]
    new_k_p, new_v_p = pos_last(kb_p), pos_last(vb_p)
    new_ik_p = cut(small_p)[..., _L_KI:_L_KI + IDX_DIM][None]
    new_conv_p = tail_p[:, SUBLANES - (CONV_W - 1):][None]
    per_s = lambda a: a.reshape(nsb, t_new, -1)
    new_ckv_s = per_s(ckv_s)[None]
    new_kpe_s = per_s(small_s)[..., _L_KPE:_L_KPE + MLA_ROPE][None]
    new_k_s = per_s(kb_s).reshape(1, nsb, t_new, DSA_HEADS, DSA_HEAD_DIM)
    new_v_s = per_s(vb_s).reshape(1, nsb, t_new, DSA_HEADS, DSA_HEAD_DIM)
    new_ik_s = per_s(small_s)[..., _L_KI:_L_KI + IDX_DIM][None]
    new_conv_s = per_s(tail_s)[:, t_new - (CONV_W - 1):][None]
    return (y_prompt, y_sample, new_ckv_p, new_kpe_p, new_k_p, new_v_p, new_ik_p, new_conv_p,
            new_ckv_s, new_kpe_s, new_k_s, new_v_s, new_ik_s, new_conv_s)


def _np_head_mask(t_new):
    r = np.arange(DSA_HEADS * t_new)[:, None] // t_new
    c = np.arange(HD)[None, :] // DSA_HEAD_DIM
    return jnp.asarray(r == c)[None]
```

```python
import functools

import numpy as np
import jax
import jax.numpy as jnp
from jax import lax
from jax.experimental import pallas as pl
from jax.experimental.pallas import tpu as pltpu

N_META = 16
MLA_HEADS = 8
MLA_NOPE = 64
MLA_ROPE = 32
MLA_V = 64
MLA_KV_LORA = 256
MLA_Q_LORA = 768
MLA_SCALE = (MLA_NOPE + MLA_ROPE) ** -0.5
DSA_HEADS = 8
DSA_HEAD_DIM = 64
DSA_SCALE = DSA_HEAD_DIM ** -0.5
IDX_HEADS = 8
IDX_DIM = 64
IDX_SCALE = IDX_DIM ** -0.5
LOG2E = 1.4426950408889634
MLA_QSCALE = MLA_SCALE * LOG2E
DSA_QSCALE = DSA_SCALE * LOG2E
TOPK_MAX = 256
CONV_W = 3
ROPE_THETA = 10000.0
EPS = 1e-6
NEG_INF = -1e30
PAGE_SIZE = 128

LANES = 128
SUBLANES = 8
HD = 512
VMEM_LIMIT = 56 * 1024 * 1024

F32 = jnp.float32
BF16 = jnp.bfloat16

_C_Q, _C_KV, _Q_B, _K_B, _V_B, _Q_I, _G_A, _G_B, _SMALL, _D_IN_P = (
    0, 768, 1024, 1536, 2048, 2560, 3072, 4096, 5120, 5248)
_L_KI, _L_KPE, _L_WI = 0, 64, 96


def _dot(a, b):
    return jnp.dot(a, b, preferred_element_type=F32)


def _dot_t(a, b):
    return lax.dot_general(a, b, (((1,), (1,)), ((), ())), preferred_element_type=F32)


def _rms(x, g):
    ms = jnp.mean(x * x, axis=-1, keepdims=True)
    return x * lax.rsqrt(ms + EPS) * g


def _rope_tables(pos):
    pos = jnp.asarray(pos).astype(F32)[:, None]
    lane = np.arange(LANES)
    out = []
    for width in (64, 32):
        half = width // 2
        m = lane % width
        inv_freq = 1.0 / (ROPE_THETA ** (jnp.arange(half, dtype=F32) / half))
        ang = pos * inv_freq[m % half][None, :]
        c, s = jnp.cos(ang), jnp.sin(ang)
        out += [c, jnp.where(m >= half, s, 0.0), jnp.where(m < half, -s, 0.0)]
    return out


def _rope_blk(x, c, sa, sb, half):
    return x * c + pltpu.roll(x, half, 1) * sa + pltpu.roll(x, LANES - half, 1) * sb


def _proj_kernel(sample, x_ref, c64_ref, sa64_ref, sb64_ref, c32_ref, sa32_ref, sb32_ref,
                 g1_ref, gq_ref, gkv_ref, win_ref, wuq_ref, wkv_ref, *outs):
    if sample:
        (ckv_o, kb_o, vb_o, small_o, qlat_o, qp_o, qb_o, qi_o, gate_o) = outs
    else:
        (ckv_o, kb_o, vb_o, small_o, qn_o, qp_o, qb_o, qi_o, gate_o,
         kcat_o, vm_o, kbb_o, vbb_o, ki2_o) = outs
    hb = _rms(x_ref[0], g1_ref[...]).astype(BF16)
    c64, sa64, sb64 = c64_ref[...], sa64_ref[...], sb64_ref[...]
    c32, sa32, sb32 = c32_ref[...], sa32_ref[...], sb32_ref[...]

    def proj(a, b):
        return _dot(hb, win_ref[:, a:b])

    cq = _rms(proj(_C_Q, _C_KV), gq_ref[...]).astype(BF16)
    ckv = _rms(proj(_C_KV, _Q_B), gkv_ref[...])
    ckv_o[0] = ckv

    zq = proj(_Q_B, _K_B)
    zk = proj(_K_B, _V_B)
    zi = proj(_Q_I, _G_A)
    for blk in range(HD // LANES):
        sl = slice(blk * LANES, (blk + 1) * LANES)
        qb_o[0, :, sl] = (_rope_blk(zq[:, sl], c64, sa64, sb64, 32) * DSA_QSCALE).astype(BF16)
        qi_o[0, :, sl] = (_rope_blk(zi[:, sl], c64, sa64, sb64, 32) * IDX_SCALE).astype(BF16)
        kr = _rope_blk(zk[:, sl], c64, sa64, sb64, 32)
        if sample:
            kb_o[0, :, sl] = kr
        else:
            kb_o[0, sl, :] = kr.T
            kbb_o[0, :, sl] = kr.astype(BF16)
    zv = proj(_V_B, _Q_I)
    if sample:
        vb_o[0] = zv
    else:
        for blk in range(HD // LANES):
            sl = slice(blk * LANES, (blk + 1) * LANES)
            zvt = zv[:, sl].T
            vb_o[0, sl, :] = zvt
            vbb_o[0, sl, :] = zvt.astype(BF16)

    gate_o[0] = jax.nn.sigmoid(proj(_G_A, _SMALL)).astype(BF16)

    zs = proj(_SMALL, _D_IN_P)
    lane = lax.broadcasted_iota(jnp.int32, (1, LANES), 1)
    m_ki = (lane < _L_KPE).astype(F32)
    m_kpe = ((lane >= _L_KPE) & (lane < _L_WI)).astype(F32)
    m_wi = ((lane >= _L_WI) & (lane < _L_WI + IDX_HEADS)).astype(F32)
    small = (zs * (c64 * m_ki + c32 * m_kpe + (IDX_HEADS ** -0.5) * m_wi)
             + pltpu.roll(zs, 32, 1) * (sa64 * m_ki) + pltpu.roll(zs, 96, 1) * (sb64 * m_ki)
             + pltpu.roll(zs, 16, 1) * (sa32 * m_kpe) + pltpu.roll(zs, 112, 1) * (sb32 * m_kpe))
    small_o[0] = small
    if not sample:
        ki = small * m_ki
        ki2_o[0] = (ki + pltpu.roll(ki, 64, 1)).astype(BF16)
        kp = pltpu.roll(small * m_kpe, 64, 1)
        kpe4 = (kp + pltpu.roll(kp, 32, 1) + pltpu.roll(kp, 64, 1)
                + pltpu.roll(kp, 96, 1)).astype(BF16)
        for pair in range(MLA_HEADS // 2):
            kcat_o[0, :, (2 * pair + 1) * LANES:(2 * pair + 2) * LANES] = kpe4

    q = _dot(cq, wuq_ref[...])
    qn = (q[:, :HD] * MLA_QSCALE).astype(BF16)
    for blk in range(2):
        qpe = q[:, HD + blk * LANES:HD + (blk + 1) * LANES]
        qp_o[0, :, blk * LANES:(blk + 1) * LANES] = (
            _rope_blk(qpe, c32, sa32, sb32, 16) * MLA_QSCALE).astype(BF16)

    if sample:
        qlat_o[0] = _dot(qn, wkv_ref[...]).astype(BF16)
    else:
        qn_o[0] = qn
        kv = _dot(ckv.astype(BF16), wkv_ref[...])
        for pair in range(MLA_HEADS // 2):
            kcat_o[0, :, 2 * pair * LANES:(2 * pair + 1) * LANES] = (
                kv[:, pair * LANES:(pair + 1) * LANES].astype(BF16))
        for blk in range(HD // LANES):
            vm_o[0, blk * LANES:(blk + 1) * LANES, :] = (
                kv[:, HD + blk * LANES:HD + (blk + 1) * LANES].T.astype(BF16))


def _const_spec(shape):
    nd = len(shape)
    return pl.BlockSpec(shape, lambda *_: (0,) * nd, pipeline_mode=pl.Buffered(1))


def _proj_call(x, tables, g1, gq, gkv, win, wuq, wkv, tm, sample, l_out=None):
    nb, lp, d = x.shape
    l_out = lp if l_out is None else l_out
    grid = (nb, lp // tm)
    row = lambda w: pl.BlockSpec((1, tm, w), lambda b, j: (b, j, 0))
    tab = pl.BlockSpec((tm, LANES), lambda b, j: (j, 0))
    in_specs = ([row(d)] + [tab] * 6
                + [_const_spec(a.shape) for a in (g1, gq, gkv, win, wuq, wkv)])
    f32_w = [MLA_KV_LORA, HD, HD, LANES]
    if sample:
        bf_w = [MLA_HEADS * MLA_KV_LORA, 2 * LANES, HD, HD, 2048]
    else:
        bf_w = [HD, 2 * LANES, HD, HD, 2048, 2 * HD, HD, HD, HD, LANES]
    out_shape = ([jax.ShapeDtypeStruct((nb, lp, w), F32) for w in f32_w]
                 + [jax.ShapeDtypeStruct((nb, lp, w), BF16) for w in bf_w])
    out_specs = [row(w) for w in f32_w + bf_w]
    out_shape[0] = jax.ShapeDtypeStruct((nb, l_out, MLA_KV_LORA), F32)
    if not sample:
        n_f32 = len(f32_w)
        for i, dt, rows in ((1, F32, l_out), (2, F32, l_out), (n_f32 + 6, BF16, lp),
                            (n_f32 + 8, BF16, lp)):
            out_shape[i] = jax.ShapeDtypeStruct((nb, HD, rows), dt)
            out_specs[i] = pl.BlockSpec((1, HD, tm), lambda b, j: (b, 0, j))
    return pl.pallas_call(
        functools.partial(_proj_kernel, sample),
        grid=grid, in_specs=in_specs, out_specs=out_specs, out_shape=out_shape,
        compiler_params=pltpu.CompilerParams(
            dimension_semantics=("arbitrary", "arbitrary"), vmem_limit_bytes=VMEM_LIMIT),
        name="proj_sample" if sample else "proj_prompt",
    )(x, *tables, g1, gq, gkv, win, wuq, wkv)


_MAX_BISECT = 320
_BISECT_UNROLL = 4


def _chunk_ds(c, tk):
    start = c * tk
    return pl.ds(start if isinstance(start, int) else pl.multiple_of(start, tk), tk)


def _key_chunk(ref, c, tk, axis):
    ds = _chunk_ds(c, tk)
    return ref[ds, :] if axis == 0 else ref[:, ds]


def _key_fold(v, op, axis):
    if axis == 0:
        parts = [v[i * SUBLANES:(i + 1) * SUBLANES] for i in range(v.shape[0] // SUBLANES)]
    else:
        parts = [v[:, i * LANES:(i + 1) * LANES] for i in range(v.shape[1] // LANES)]
    while len(parts) > 1:
        parts = [op(a, b) for a, b in zip(parts[::2], parts[1::2])] + (
            [parts[-1]] if len(parts) % 2 else [])
    return parts[0]


def _key_index(c, tk, axis):
    shape = (tk, 1) if axis == 0 else (1, tk)
    return (lax.convert_element_type(c * tk, F32)
            + lax.broadcasted_iota(jnp.int32, shape, axis).astype(F32))


def _count(sc_ref, nck, tk, axis, pred):
    nq = sc_ref.shape[1 - axis]
    part = (SUBLANES, nq) if axis == 0 else (nq, LANES)

    def body(c, acc):
        v = jnp.where(pred(_key_chunk(sc_ref, c, tk, axis), c), 1.0, 0.0)
        return acc + _key_fold(v, jnp.add, axis)

    acc = lax.fori_loop(0, nck, body, jnp.zeros(part, F32))
    return jnp.sum(acc, axis=axis, keepdims=True)


def _stats_init(part):
    return (jnp.full(part, -NEG_INF, F32), jnp.full(part, NEG_INF, F32),
            jnp.zeros(part, F32), jnp.zeros(part, F32))


def _stats_update(stats, x, axis, has_masked):
    mn, mx, gt0, ge0 = stats
    xv = jnp.where(x > 0.5 * NEG_INF, x, -NEG_INF) if has_masked else x
    return (jnp.minimum(mn, _key_fold(xv, jnp.minimum, axis)),
            jnp.maximum(mx, _key_fold(x, jnp.maximum, axis)),
            gt0 + _key_fold(jnp.where(x > 0.0, 1.0, 0.0), jnp.add, axis),
            ge0 + _key_fold(jnp.where(x >= 0.0, 1.0, 0.0), jnp.add, axis))


def _select_threshold(sc_ref, nck, tk, k_eff, n_valid, axis, stats=None):
    nq = sc_ref.shape[1 - axis]
    part = (SUBLANES, nq) if axis == 0 else (nq, LANES)

    if stats is None:
        stats = lax.fori_loop(
            0, nck, lambda c, st: _stats_update(st, _key_chunk(sc_ref, c, tk, axis), axis, True),
            _stats_init(part))
    mn, mx, gt0, ge0 = stats
    lo0 = jnp.min(mn, axis=axis, keepdims=True)
    hi0 = jnp.max(mx, axis=axis, keepdims=True)
    hi0 = hi0 + jnp.maximum(jnp.abs(hi0), 1.0) * 1e-6

    def count_ge(thr):
        return _count(sc_ref, nck, tk, axis, lambda x, c: x >= thr)

    def all_rows(fin):
        return jnp.min(jnp.where(fin, 1.0, 0.0)) > 0.5

    c_gt0 = jnp.sum(gt0, axis=axis, keepdims=True)
    c_ge0 = jnp.sum(ge0, axis=axis, keepdims=True)
    at_zero = (c_gt0 < k_eff) & (c_ge0 >= k_eff)
    above = c_gt0 >= k_eff
    inside = (lo0 < 0.0) & (hi0 > 0.0)
    lo_at_zero = at_zero | (above & inside)
    c_lo0 = jnp.where(lo_at_zero, c_ge0, n_valid)
    lo0, hi0 = (jnp.where(lo_at_zero, 0.0, lo0),
                jnp.where(at_zero | (~above & inside), 0.0, hi0))

    def cond(st):
        it, _, _, _, done = st
        return jnp.logical_and(it < _MAX_BISECT, jnp.logical_not(done))

    def body(st):
        it, lo, hi, c_lo, _ = st
        for _ in range(_BISECT_UNROLL):
            mid = 0.5 * lo + 0.5 * hi
            c = count_ge(mid)
            ge = c >= k_eff
            fin = (mid <= lo) | (mid >= hi)
            lo, hi, c_lo = jnp.where(ge, mid, lo), jnp.where(ge, hi, mid), jnp.where(ge, c, c_lo)
            fin = fin | (c_lo == k_eff)
        return it + _BISECT_UNROLL, lo, hi, c_lo, all_rows(fin)

    fin0 = (c_lo0 == k_eff) | (lo0 >= hi0)
    _, t, _, c_t, _ = lax.while_loop(
        cond, body, (jnp.int32(0), lo0, hi0, c_lo0, all_rows(fin0)))

    tie_rows = c_t > k_eff

    @pl.when(jnp.max(jnp.where(tie_rows, 1.0, 0.0)) > 0.5)
    def _():
        n_keys = sc_ref.shape[axis]
        need = k_eff - _count(sc_ref, nck, tk, axis, lambda x, c: x > t)

        def tie_count(m):
            return _count(sc_ref, nck, tk, axis,
                          lambda x, c: (x == t) & (_key_index(c, tk, axis) <= m))

        def ibody(_, st):
            lo_i, hi_i = st
            mid = jnp.floor(0.5 * (lo_i + hi_i))
            ok = tie_count(mid) >= need
            return jnp.where(ok, lo_i, mid), jnp.where(ok, mid, hi_i)

        steps = int(np.ceil(np.log2(n_keys))) + 1
        _, m_idx = lax.fori_loop(0, steps, ibody,
                                 (jnp.full(t.shape, -1.0, F32),
                                  jnp.full(t.shape, float(n_keys - 1), F32)))

        def fix(c, carry):
            x = _key_chunk(sc_ref, c, tk, axis)
            drop = (x == t) & (_key_index(c, tk, axis) > m_idx) & tie_rows
            ds = _chunk_ds(c, tk)
            if axis == 0:
                sc_ref[ds, :] = jnp.where(drop, NEG_INF, x)
            else:
                sc_ref[:, ds] = jnp.where(drop, NEG_INF, x)
            return carry

        lax.fori_loop(0, nck, fix, 0)

    return t


def _lane_blocks(s):
    return [s[:, i * LANES:(i + 1) * LANES] for i in range(s.shape[1] // LANES)]


def _scale_cols(alpha, x):
    return jnp.concatenate([alpha * b for b in _lane_blocks(x)], axis=1)


def _online_update(m_ref, l_ref, acc_ref, idx, blocks, pv):
    m = m_ref[idx]
    bm = blocks[0]
    for b in blocks[1:]:
        bm = jnp.maximum(bm, b)
    m_new = jnp.maximum(m, jnp.max(bm, axis=1, keepdims=True))
    alpha = jnp.exp2(m - m_new)
    ps = [jnp.exp2(b - m_new) for b in blocks]
    rs = ps[0]
    for p in ps[1:]:
        rs = rs + p
    l_ref[idx] = alpha * l_ref[idx] + jnp.sum(rs, axis=1, keepdims=True)
    acc_ref[idx] = _scale_cols(alpha, acc_ref[idx]) + pv([p.astype(BF16) for p in ps])
    m_ref[idx] = m_new


def _reset_state(m_ref, l_ref, acc_ref):
    m_ref[...] = jnp.full_like(m_ref, NEG_INF)
    l_ref[...] = jnp.zeros_like(l_ref)
    acc_ref[...] = jnp.zeros_like(acc_ref)


def _softmax_keys_major(m_ref, l_ref, idx, s):
    m = m_ref[idx]
    m_new = jnp.maximum(m, jnp.max(_key_fold(s, jnp.maximum, 0), axis=0, keepdims=True))
    alpha = jnp.exp2(m - m_new)
    p = jnp.exp2(s - m_new)
    l_ref[idx] = alpha * l_ref[idx] + jnp.sum(_key_fold(p, jnp.add, 0), axis=0, keepdims=True)
    m_ref[idx] = m_new
    return alpha, p.astype(BF16)


def _attn_kernel(tq, tk, n_keep, topk,
                 qn_ref, qp_ref, qb_ref, qi_ref, small_ref,
                 kcat_ref, vm_ref, kb_ref, vb_ref, ki2_ref,
                 mla_ref, dsa_ref, sc_ref, qc_ref, qd_ref, qx_ref, w_ref, st_ref, m_ref, l_ref,
                 acc_ref):
    j = pl.program_id(1)
    nh = MLA_HEADS
    n_pairs = nh // 2

    @pl.when(j >= n_keep)
    def _():
        mla_ref[...] = jnp.zeros_like(mla_ref)
        dsa_ref[...] = jnp.zeros_like(dsa_ref)

    @pl.when(j < n_keep)
    def _():
        nck = (j * tq + tq - 1) // tk + 1
        last = nck - 1
        lane = lax.broadcasted_iota(jnp.int32, (1, LANES), 1)
        q_row = j * tq + lax.broadcasted_iota(jnp.int32, (1, tq), 1)
        k_row = lax.broadcasted_iota(jnp.int32, (tk, 1), 0)
        zero_b = jnp.zeros((tq, LANES), BF16)

        def ksl(c):
            return _chunk_ds(c, tk)

        def pair_sl(pair):
            return slice(pair * LANES, (pair + 1) * LANES)

        def two(x):
            return jnp.concatenate([x, x], axis=1)

        def write_heads(out_ref):
            half = LANES // 2
            for pair in range(n_pairs):
                o = acc_ref[pair] / l_ref[pair]
                o = jnp.concatenate([o[:half, :tq], o[half:, tq:]], axis=0)
                out_ref[0, :, pair_sl(pair)] = o.T.astype(BF16)

        small = small_ref[0]
        for h in range(nh):
            pair, sub = divmod(h, 2)
            grp, gsub = divmod(h, 4)
            rows = slice(sub * tq, (sub + 1) * tq)
            own = (lane >= 64) == bool(sub)
            qc_ref[pair, rows, :LANES] = jnp.where(own, qn_ref[0, :, pair_sl(pair)], zero_b)
            qc_ref[pair, rows, LANES:] = jnp.where((lane // 32) == gsub,
                                                   qp_ref[0, :, pair_sl(grp)], zero_b)
            qd_ref[pair, rows, :] = jnp.where(own, qb_ref[0, :, pair_sl(pair)], zero_b)
            qx_ref[h * tq:(h + 1) * tq, :] = jnp.where(own, qi_ref[0, :, pair_sl(pair)], zero_b)
        w_ref[...] = small.T[_L_WI:_L_WI + IDX_HEADS]

        def attend(c, score_fn, fix_fn, vt_ref):
            ks = ksl(c)
            s = {p: score_fn(p, ks) for p in range(min(2, n_pairs))}
            for p in range(n_pairs):
                alpha, pr = _softmax_keys_major(m_ref, l_ref, p, fix_fn(s.pop(p)))
                if p + 2 < n_pairs:
                    s[p + 2] = score_fn(p + 2, ks)
                acc_ref[p] = alpha * acc_ref[p] + _dot(vt_ref[0, pair_sl(p), ks], pr)

        _reset_state(m_ref, l_ref, acc_ref)
        for i, part in enumerate(_stats_init((SUBLANES, tq))):
            st_ref[i] = part

        def mla_scores(p, ks):
            return _dot_t(kcat_ref[0, ks, 2 * p * LANES:(2 * p + 2) * LANES], qc_ref[p])

        def mla_idx_chunk(c, masked):
            ks = ksl(c)
            r = _dot_t(ki2_ref[0, ks, :], qx_ref[...])
            sc = None
            for h in range(IDX_HEADS):
                term = jnp.maximum(r[:, h * tq:(h + 1) * tq], 0.0) * w_ref[h:h + 1, :]
                sc = term if sc is None else sc + term
            if masked:
                sc = jnp.where((c * tk + k_row) <= q_row, sc, NEG_INF)
                vis2 = (c * tk + k_row) <= two(q_row)
                fix = lambda s: jnp.where(vis2, s, NEG_INF)
            else:
                fix = lambda s: s
            sc_ref[ks, :] = sc
            for i, part in enumerate(_stats_update(tuple(st_ref[i] for i in range(4)), sc, 0,
                                                   masked)):
                st_ref[i] = part
            attend(c, mla_scores, fix, vm_ref)

        def mla_idx_step(i, carry):
            mla_idx_chunk(2 * i, False)
            mla_idx_chunk(2 * i + 1, False)
            return carry

        lax.fori_loop(0, last // 2, mla_idx_step, 0)

        @pl.when(last % 2 == 1)
        def _():
            mla_idx_chunk(last - 1, False)

        mla_idx_chunk(last, True)
        write_heads(mla_ref)

        n_valid = (q_row + 1).astype(F32)
        k_eff = jnp.minimum(n_valid, float(topk))
        t2 = two(_select_threshold(sc_ref, nck, tk, k_eff, n_valid, 0,
                                   tuple(st_ref[i] for i in range(4))))

        _reset_state(m_ref, l_ref, acc_ref)

        def dsa_scores(p, ks):
            return _dot_t(kb_ref[0, ks, pair_sl(p)], qd_ref[p])

        def dsa_chunk(c):
            sel = two(sc_ref[ksl(c), :]) >= t2
            attend(c, dsa_scores, lambda s: jnp.where(sel, s, NEG_INF), vb_ref)

        def dsa_step(i, carry):
            dsa_chunk(2 * i)
            dsa_chunk(2 * i + 1)
            return carry

        lax.fori_loop(0, nck // 2, dsa_step, 0)

        @pl.when(nck % 2 == 1)
        def _():
            dsa_chunk(nck - 1)

        write_heads(dsa_ref)


def _attn_call(qn, qp, qb, qi, small, kcat, vm, kbb, vbb, ki2, l_valid, tq, tk):
    nb, lp, _ = qn.shape
    n_keep = -(-l_valid // tq)
    topk = min(TOPK_MAX, l_valid // 4)
    n_pairs = MLA_HEADS // 2
    qrow = lambda w: pl.BlockSpec((1, tq, w), lambda b, j: (b, j, 0))
    krow = lambda w: pl.BlockSpec((1, lp, w), lambda b, j: (b, 0, 0))
    vcol = pl.BlockSpec((1, HD, lp), lambda b, j: (b, 0, 0))
    in_specs = [qrow(HD), qrow(2 * LANES), qrow(HD), qrow(HD), qrow(LANES),
                krow(2 * HD), vcol, krow(HD), vcol, krow(LANES)]
    return pl.pallas_call(
        functools.partial(_attn_kernel, tq, tk, n_keep, topk),
        grid=(nb, lp // tq), in_specs=in_specs,
        out_specs=[qrow(HD), qrow(HD)],
        out_shape=[jax.ShapeDtypeStruct((nb, lp, HD), BF16)] * 2,
        scratch_shapes=[pltpu.VMEM((lp, tq), F32),
                        pltpu.VMEM((n_pairs, 2 * tq, 2 * LANES), BF16),
                        pltpu.VMEM((n_pairs, 2 * tq, LANES), BF16),
                        pltpu.VMEM((IDX_HEADS * tq, LANES), BF16),
                        pltpu.VMEM((IDX_HEADS, tq), F32),
                        pltpu.VMEM((4, SUBLANES, tq), F32),
                        pltpu.VMEM((n_pairs, 1, 2 * tq), F32),
                        pltpu.VMEM((n_pairs, 1, 2 * tq), F32),
                        pltpu.VMEM((n_pairs, LANES, 2 * tq), F32)],
        compiler_params=pltpu.CompilerParams(
            dimension_semantics=("arbitrary", "arbitrary"), vmem_limit_bytes=VMEM_LIMIT),
        name="attn_prompt",
    )(qn, qp, qb, qi, small, kcat, vm, kbb, vbb, ki2)


def _merge_kernel(x_ref, mla_ref, dsa_ref, gate_ref, wa_ref, wb_ref, wo_ref, g2_ref,
                  x2_ref, h2_ref):
    d = x_ref.shape[-1]
    a = _dot(mla_ref[0], wa_ref[...])
    b = _dot(dsa_ref[0], wb_ref[...])
    g = gate_ref[0]
    o = g[:, :d].astype(F32) * a + g[:, d:].astype(F32) * b
    x2 = x_ref[0] + _dot(o.astype(BF16), wo_ref[...])
    x2_ref[0] = x2
    h2_ref[0] = _rms(x2, g2_ref[...]).astype(BF16)


def _merge_call(x, mla, dsa, gate, wa, wb, wo, g2, tm, name):
    nb, lp, d = x.shape
    row = lambda w: pl.BlockSpec((1, tm, w), lambda b, j: (b, j, 0))
    return pl.pallas_call(
        _merge_kernel, grid=(nb, lp // tm),
        in_specs=[row(d), row(HD), row(HD), row(2 * d)]
        + [_const_spec(a.shape) for a in (wa, wb, wo, g2)],
        out_specs=[row(d), row(d)],
        out_shape=[jax.ShapeDtypeStruct((nb, lp, d), F32), jax.ShapeDtypeStruct((nb, lp, d), BF16)],
        compiler_params=pltpu.CompilerParams(
            dimension_semantics=("arbitrary", "arbitrary"), vmem_limit_bytes=VMEM_LIMIT),
        name=name,
    )(x, mla, dsa, gate, wa, wb, wo, g2)


_FF_CHUNK = 512
_FF_AHEAD = 1


def _ffn_kernel(seq_rows, tail_tile, tail_off, h_ref, x_ref, s1_ref, s2_ref,
                wg_ref, wu_ref, wd_ref, cw_ref, cb_ref, gf_ref, y_ref, tail_ref, prev_ref):
    j = pl.program_id(1)
    tm = h_ref.shape[1]
    d_ff = wg_ref.shape[1]
    h = h_ref[0]
    row = lax.broadcasted_iota(jnp.int32, (tm, 1), 0)
    if seq_rows is None:
        @pl.when(j == 0)
        def _():
            prev_ref[...] = jnp.zeros_like(prev_ref)
        first1, first2 = row < 1, row < 2
    else:
        first1, first2 = (row % seq_rows) < 1, (row % seq_rows) < 2

    acc = jnp.zeros((tm, x_ref.shape[-1]), F32)
    chunks = [slice(c0, min(c0 + _FF_CHUNK, d_ff)) for c0 in range(0, d_ff, _FF_CHUNK)]

    def up(sl):
        return _dot(h, wg_ref[:, sl]), _dot(h, wu_ref[:, sl])

    ahead = [up(sl) for sl in chunks[:_FF_AHEAD]]
    for i, sl in enumerate(chunks):
        g, u = ahead.pop(0)
        if i + _FF_AHEAD < len(chunks):
            ahead.append(up(chunks[i + _FF_AHEAD]))
        if seq_rows is None:
            p = prev_ref[:, sl]
            hist1 = jnp.broadcast_to(p[SUBLANES - 1:SUBLANES], g.shape)
            hist2 = jnp.where(row < 1, jnp.broadcast_to(p[SUBLANES - 2:SUBLANES - 1], g.shape),
                              hist1)
            prev_ref[:, sl] = g[tm - SUBLANES:]
        else:
            hist1, hist2 = s1_ref[0, :, sl], s2_ref[0, :, sl]
        g1 = jnp.where(first1, hist1, pltpu.roll(g, 1, 0))
        g2 = jnp.where(first2, hist2, pltpu.roll(g, 2, 0))
        cw = cw_ref[:, sl]
        gc = cb_ref[:, sl] + cw[0:1] * g2 + cw[1:2] * g1 + cw[2:3] * g
        act = (gc * jax.nn.sigmoid(gc) * u).astype(BF16)
        acc = acc + _dot(act, wd_ref[sl, :])
        if seq_rows is None:
            @pl.when(j == tail_tile)
            def _(g=g, sl=sl):
                tail_ref[0, :, sl] = g[tail_off:tail_off + SUBLANES]
        else:
            tail_ref[0, :, sl] = g
    y_ref[0] = _rms(x_ref[0] + acc, gf_ref[...])


def _ffn_call(h2, x2, s1, s2, wg, wu, wd, cw, cb, gf, tm, l_valid, sample, name):
    nb, lp, d = x2.shape
    d_ff = wg.shape[1]
    row = lambda w: pl.BlockSpec((1, tm, w), lambda b, j: (b, j, 0))
    if sample:
        seq_rows, tail_tile, tail_off = l_valid, 0, 0
        tail_shape, tail_spec = (nb, lp, d_ff), row(d_ff)
        s_spec = row(d_ff)
    else:
        seq_rows = None
        tail_tile, tail_off = divmod(l_valid - SUBLANES, tm)
        tail_shape = (nb, SUBLANES, d_ff)
        tail_spec = pl.BlockSpec((1, SUBLANES, d_ff), lambda b, j: (b, 0, 0))
        s_spec = pl.BlockSpec((1, SUBLANES, d_ff), lambda b, j: (0, 0, 0))
    return pl.pallas_call(
        functools.partial(_ffn_kernel, seq_rows, tail_tile, tail_off),
        grid=(nb, lp // tm),
        in_specs=[row(d), row(d), s_spec, s_spec]
        + [_const_spec(a.shape) for a in (wg, wu, wd, cw, cb, gf)],
        out_specs=[row(d), tail_spec],
        out_shape=[jax.ShapeDtypeStruct((nb, lp, d), F32), jax.ShapeDtypeStruct(tail_shape, F32)],
        scratch_shapes=[pltpu.VMEM((SUBLANES, d_ff), F32)],
        compiler_params=pltpu.CompilerParams(
            dimension_semantics=("arbitrary", "arbitrary"), vmem_limit_bytes=VMEM_LIMIT),
        name=name,
    )(h2, x2, s1, s2, wg, wu, wd, cw, cb, gf)


def _head_rows_mask(rows_per_head, n_heads, width_per_head):
    r = lax.broadcasted_iota(jnp.int32, (n_heads * rows_per_head, 1), 0) // rows_per_head
    c = lax.broadcasted_iota(jnp.int32, (1, n_heads * width_per_head), 1) // width_per_head
    return r == c


def _diag_heads(full, t, n_heads, width):
    col_head = lax.broadcasted_iota(jnp.int32, (1, n_heads * width), 1) // width
    out = jnp.zeros((t, n_heads * width), F32)
    for h in range(n_heads):
        out = jnp.where(col_head == h, full[h * t:(h + 1) * t], out)
    return out


def _smla_kernel(npp, t_new, pt_ref, qabs_ref, qpe_ref, qi_ref, wrow_ref,
                 ckvn_ref, kpen_ref, ikn_ref, wuv_ref, *rest):
    ckv_pages = rest[:npp]
    kpe_pages = rest[npp:2 * npp]
    ik_pages = rest[2 * npp:3 * npp]
    mla_ref, scp_ref, scn_ref, m_ref, l_ref, acc_ref, ck_ref, kpe_ref, ik_ref = rest[3 * npp:]
    c = pl.program_id(1)
    rows = qabs_ref.shape[1]
    qabs, qpe, qi, wrow = qabs_ref[0], qpe_ref[0], qi_ref[0], wrow_ref[0]

    @pl.when(c == 0)
    def _():
        _reset_state(m_ref, l_ref, acc_ref)

    def update(s, vals):
        _online_update(m_ref, l_ref, acc_ref, 0, _lane_blocks(s),
                       lambda ps: _dot(jnp.concatenate(ps, axis=1), vals))

    def idx_score(s):
        s = _scale_cols(wrow, jnp.maximum(s, 0.0))
        out = s[:t_new]
        for h in range(1, IDX_HEADS):
            out = out + s[h * t_new:(h + 1) * t_new]
        return out

    for i in range(npp):
        ck_ref[i * PAGE_SIZE:(i + 1) * PAGE_SIZE, :] = ckv_pages[i][...].astype(BF16)
        kpe_ref[:, i * PAGE_SIZE:(i + 1) * PAGE_SIZE] = kpe_pages[i][...].astype(BF16)
        ik_ref[:, i * PAGE_SIZE:(i + 1) * PAGE_SIZE] = ik_pages[i][...].astype(BF16)
    ck_all = ck_ref[...]
    update(_dot_t(qabs, ck_all) + _dot(qpe, kpe_ref[...]), ck_all)
    scp_ref[0] = idx_score(_dot(qi, ik_ref[...]))

    @pl.when(c == pl.num_programs(1) - 1)
    def _():
        ck = ckvn_ref[0]
        tok = lax.broadcasted_iota(jnp.int32, (rows, 1), 0) % t_new
        key = lax.broadcasted_iota(jnp.int32, (1, PAGE_SIZE), 1)
        vis = key <= tok
        s = _dot_t(qabs, ck) + _dot_t(qpe, kpen_ref[0])
        update(jnp.where(vis, s, NEG_INF), ck)
        scn_ref[0] = jnp.where(vis[:t_new], idx_score(_dot_t(qi, ikn_ref[0])), NEG_INF)
        o_lat = _scale_cols(1.0 / l_ref[0], acc_ref[0]).astype(BF16)
        mla_ref[0] = _diag_heads(_dot(o_lat, wuv_ref[...]), t_new, MLA_HEADS, MLA_V).astype(BF16)


def _sdsa_kernel(npp, t_new, topk, past_len, sel_tk, pt_ref, qbd_ref, scp_ref, scn_ref, kbn_ref,
                 vbn_ref, *rest):
    k_pages = rest[:npp]
    v_pages = rest[npp:2 * npp]
    dsa_ref, sc_ref, t_ref, m_ref, l_ref, acc_ref, kt_ref, vt_ref = rest[2 * npp:]
    c = pl.program_id(1)
    rows = qbd_ref.shape[1]
    qbd = qbd_ref[0]
    step_keys = npp * PAGE_SIZE

    @pl.when(c == 0)
    def _():
        _reset_state(m_ref, l_ref, acc_ref)
        sc_ref[:, :past_len] = scp_ref[0]
        sc_ref[:, past_len:past_len + PAGE_SIZE] = scn_ref[0]
        if sel_tk > PAGE_SIZE:
            sc_ref[:, past_len + PAGE_SIZE:] = jnp.full((t_new, sel_tk - PAGE_SIZE), NEG_INF, F32)
        tok = lax.broadcasted_iota(jnp.int32, (t_new, 1), 0)
        n_valid = (past_len + 1 + tok).astype(F32)
        k_eff = jnp.minimum(n_valid, float(topk))
        t_ref[...] = _select_threshold(sc_ref, (past_len + sel_tk) // sel_tk, sel_tk,
                                       k_eff, n_valid, 1)

    t = t_ref[...]

    def masked(s, sc):
        pen = jnp.where(sc >= t, 0.0, NEG_INF)
        sel = jnp.concatenate([pen] * DSA_HEADS, axis=0) == 0.0
        return jnp.where(sel, s, NEG_INF)

    def update(s, pv_dot, vals):
        _online_update(m_ref, l_ref, acc_ref, 0, _lane_blocks(s),
                       lambda ps: pv_dot(jnp.concatenate(ps, axis=1), vals))

    for i in range(npp):
        kt_ref[:, i * PAGE_SIZE:(i + 1) * PAGE_SIZE] = k_pages[i][...].astype(BF16)
        vt_ref[:, i * PAGE_SIZE:(i + 1) * PAGE_SIZE] = v_pages[i][...].astype(BF16)
    start = c * step_keys
    start = start if isinstance(start, int) else pl.multiple_of(start, step_keys)
    update(masked(_dot(qbd, kt_ref[...]), sc_ref[:, pl.ds(start, step_keys)]), _dot_t, vt_ref[...])

    @pl.when(c == pl.num_programs(1) - 1)
    def _():
        s = masked(_dot_t(qbd, kbn_ref[0]), sc_ref[:, past_len:past_len + PAGE_SIZE])
        update(s, _dot, vbn_ref[0])
        o = _scale_cols(1.0 / l_ref[0], acc_ref[0])
        dsa_ref[0] = _diag_heads(o, t_new, DSA_HEADS, DSA_HEAD_DIM).astype(BF16)


def _pages_per_step(n_pages, want):
    p = min(want, n_pages)
    while n_pages % p:
        p -= 1
    return p


def _page_specs(npp, rows, width):
    return [pl.BlockSpec((None, rows, width),
                         functools.partial(lambda i, b, c, pt: (pt[b, c * npp + i], 0, 0), i))
            for i in range(npp)]


def _smla_call(page_table, qabs, qpe, qi, wrow, ckvn, kpen, ikn, wuv, pool_ckv, pool_kpe, pool_ik,
               t_new):
    nb, n_pages = page_table.shape
    npp = _pages_per_step(n_pages, 16)
    rows = qabs.shape[1]
    past_len = n_pages * PAGE_SIZE
    per_b = lambda shape: pl.BlockSpec((1,) + shape, lambda b, c, pt: (b,) + (0,) * len(shape))
    in_specs = ([per_b((rows, MLA_KV_LORA)), per_b((rows, MLA_ROPE)), per_b((rows, IDX_DIM)),
                 per_b((rows, LANES)), per_b((PAGE_SIZE, MLA_KV_LORA)),
                 per_b((PAGE_SIZE, MLA_ROPE)), per_b((PAGE_SIZE, IDX_DIM)),
                 pl.BlockSpec(wuv.shape, lambda b, c, pt: (0, 0))]
                + _page_specs(npp, PAGE_SIZE, MLA_KV_LORA) + _page_specs(npp, MLA_ROPE, PAGE_SIZE)
                + _page_specs(npp, IDX_DIM, PAGE_SIZE))
    out_specs = [per_b((t_new, HD)),
                 pl.BlockSpec((1, t_new, npp * PAGE_SIZE), lambda b, c, pt: (b, 0, c)),
                 per_b((t_new, PAGE_SIZE))]
    out_shape = [jax.ShapeDtypeStruct((nb, t_new, HD), BF16),
                 jax.ShapeDtypeStruct((nb, t_new, past_len), F32),
                 jax.ShapeDtypeStruct((nb, t_new, PAGE_SIZE), F32)]
    grid_spec = pltpu.PrefetchScalarGridSpec(
        num_scalar_prefetch=1, grid=(nb, n_pages // npp), in_specs=in_specs, out_specs=out_specs,
        scratch_shapes=[pltpu.VMEM((1, rows, LANES), F32), pltpu.VMEM((1, rows, LANES), F32),
                        pltpu.VMEM((1, rows, MLA_KV_LORA), F32),
                        pltpu.VMEM((npp * PAGE_SIZE, MLA_KV_LORA), BF16),
                        pltpu.VMEM((MLA_ROPE, npp * PAGE_SIZE), BF16),
                        pltpu.VMEM((IDX_DIM, npp * PAGE_SIZE), BF16)])
    return pl.pallas_call(
        functools.partial(_smla_kernel, npp, t_new), grid_spec=grid_spec, out_shape=out_shape,
        compiler_params=pltpu.CompilerParams(
            dimension_semantics=("arbitrary", "arbitrary"), vmem_limit_bytes=VMEM_LIMIT),
        name="sample_mla",
    )(page_table, qabs, qpe, qi, wrow, ckvn, kpen, ikn, wuv,
      *([pool_ckv] * npp), *([pool_kpe] * npp), *([pool_ik] * npp))


def _sdsa_call(page_table, qbd, scp, scn, kbn, vbn, pool_k, pool_v, t_new):
    nb, n_pages = page_table.shape
    npp = _pages_per_step(n_pages, 16)
    rows = qbd.shape[1]
    past_len = n_pages * PAGE_SIZE
    topk = min(TOPK_MAX, (past_len + t_new) // 4)
    sel_tk = PAGE_SIZE
    while sel_tk < 2048 and past_len % (2 * sel_tk) == 0:
        sel_tk *= 2
    per_b = lambda shape: pl.BlockSpec((1,) + shape, lambda b, c, pt: (b,) + (0,) * len(shape))
    in_specs = ([per_b((rows, HD)), per_b((t_new, past_len)), per_b((t_new, PAGE_SIZE)),
                 per_b((PAGE_SIZE, HD)), per_b((PAGE_SIZE, HD))]
                + _page_specs(npp, HD, PAGE_SIZE) + _page_specs(npp, HD, PAGE_SIZE))
    grid_spec = pltpu.PrefetchScalarGridSpec(
        num_scalar_prefetch=1, grid=(nb, n_pages // npp), in_specs=in_specs,
        out_specs=[per_b((t_new, HD))],
        scratch_shapes=[pltpu.VMEM((t_new, past_len + sel_tk), F32),
                        pltpu.VMEM((t_new, 1), F32),
                        pltpu.VMEM((1, rows, LANES), F32), pltpu.VMEM((1, rows, LANES), F32),
                        pltpu.VMEM((1, rows, HD), F32),
                        pltpu.VMEM((HD, npp * PAGE_SIZE), BF16),
                        pltpu.VMEM((HD, npp * PAGE_SIZE), BF16)])
    return pl.pallas_call(
        functools.partial(_sdsa_kernel, npp, t_new, topk, past_len, sel_tk), grid_spec=grid_spec,
        out_shape=[jax.ShapeDtypeStruct((nb, t_new, HD), BF16)],
        compiler_params=pltpu.CompilerParams(
            dimension_semantics=("arbitrary", "arbitrary"), vmem_limit_bytes=VMEM_LIMIT),
        name="sample_dsa",
    )(page_table, qbd, scp, scn, kbn, vbn, *([pool_k] * npp), *([pool_v] * npp))[0]


def _pick_tile(n, candidates):
    for c in candidates:
        if n % c == 0:
            return c
    return n


def _pad_rows(a, rows):
    return jnp.pad(a, ((0, 0), (0, rows - a.shape[1]), (0, 0)))


def _head_major(a, nb, t, heads):
    w = a.shape[-1] // heads
    return a.reshape(nb, t, heads, w).transpose(0, 2, 1, 3).reshape(nb, heads * t, w)


def kernel(x_prompt, x_sample, cache_mla_ckv, cache_mla_kpe, cache_dsa_k, cache_dsa_v, cache_idx_k,
           state_ffn_conv, page_table, meta_tokens, norm1_g, w_in, g_q, g_kv, w_uq, w_uk, w_uv,
           w_br_a, w_br_b, w_o, norm2_g, w_ffn_g, w_ffn_u, ffn_conv_w, ffn_conv_b, w_ffn_d, final_g):
    depth = w_in.shape[0]
    assert depth == 1
    nb, seq, d = x_prompt.shape
    nsb, t_new, _ = x_sample.shape
    n_pages = page_table.shape[1]
    past_len = n_pages * PAGE_SIZE
    l_valid = N_META + seq
    tq, tk = 256, 256
    lp = -(-l_valid // tk) * tk
    assert l_valid % SUBLANES == 0 and t_new % SUBLANES == 0 and t_new <= PAGE_SIZE
    d_ff = w_ffn_g.shape[-1]
    l = 0

    wi = w_in[l]
    cuts = np.cumsum([MLA_Q_LORA, MLA_KV_LORA, MLA_ROPE, HD, HD, HD, HD, IDX_DIM, IDX_HEADS, d, d])
    c_q, c_kv, k_pe, q_b, k_b, v_b, q_i, k_i, w_i, g_a, g_b = jnp.split(wi, cuts[:-1], axis=1)
    pad = jnp.zeros((d, LANES - IDX_DIM - MLA_ROPE - IDX_HEADS), wi.dtype)
    win = jnp.concatenate([c_q, c_kv, q_b, k_b, v_b, q_i, g_a, g_b, k_i, k_pe, w_i, pad],
                          axis=1).astype(BF16)
    wuq = jnp.concatenate([w_uq[l][:, :, :MLA_NOPE].reshape(MLA_Q_LORA, -1),
                           w_uq[l][:, :, MLA_NOPE:].reshape(MLA_Q_LORA, -1)], axis=1).astype(BF16)
    wuk2 = w_uk[l].reshape(MLA_KV_LORA, HD)
    wuv2 = w_uv[l].reshape(MLA_KV_LORA, HD)
    wkv_p = jnp.concatenate([wuk2, wuv2], axis=1).astype(BF16)
    ukt = w_uk[l].transpose(1, 2, 0)
    eye = jnp.eye(MLA_HEADS, dtype=ukt.dtype)
    wuk_bd = (ukt[:, :, None, :] * eye[:, None, :, None]).reshape(HD, MLA_HEADS * MLA_KV_LORA)
    wuk_bd = wuk_bd.astype(BF16)
    g1 = norm1_g[l][None]
    gq = g_q[l][None]
    gkv = g_kv[l][None]
    g2 = norm2_g[l][None]
    gf = final_g[None]
    wa, wb, wo = w_br_a[l].astype(BF16), w_br_b[l].astype(BF16), w_o[l].astype(BF16)
    wg, wu, wd = w_ffn_g[l].astype(BF16), w_ffn_u[l].astype(BF16), w_ffn_d[l].astype(BF16)
    cw, cb = ffn_conv_w[l], ffn_conv_b[l][None]

    meta = jnp.broadcast_to(meta_tokens[None].astype(x_prompt.dtype), (nb, N_META, d))
    xp = jnp.concatenate([meta, x_prompt, jnp.zeros((nb, lp - l_valid, d), x_prompt.dtype)], axis=1)
    tm = _pick_tile(lp, (384, 256, 128))
    tabs_p = _rope_tables(np.arange(lp))
    (ckv_p, kb_p, vb_p, small_p, qn, qp, qb, qi, gate_p, kcat, vm, kbb, vbb, ki2) = _proj_call(
        xp, tabs_p, g1, gq, gkv, win, wuq, wkv_p, tm, sample=False, l_out=l_valid)
    mla_p, dsa_p = _attn_call(qn, qp, qb, qi, small_p, kcat, vm, kbb, vbb, ki2, l_valid, tq, tk)
    x2_p, h2_p = _merge_call(xp, mla_p, dsa_p, gate_p, wa, wb, wo, g2, tm, "merge_prompt")
    zstate = jnp.zeros((1, SUBLANES, d_ff), F32)
    tm_ffn = _pick_tile(lp, (768, 384, 256, 128))
    y_p, tail_p = _ffn_call(h2_p, x2_p, zstate, zstate, wg, wu, wd, cw, cb, gf, tm_ffn, l_valid,
                            False, "ffn_prompt")

    ns = nsb * t_new
    xs = x_sample.reshape(1, ns, d)
    tabs_s = _rope_tables(past_len + (np.arange(ns) % t_new))
    (ckv_s, kb_s, vb_s, small_s, qlat, qp_s, qb_s, qi_s, gate_s) = _proj_call(
        xs, tabs_s, g1, gq, gkv, win, wuq, wuk_bd, ns, sample=True)
    qabs = _head_major(qlat, nsb, t_new, MLA_HEADS)
    qpe_r = _head_major(qp_s, nsb, t_new, MLA_HEADS)
    qi_r = _head_major(qi_s, nsb, t_new, IDX_HEADS)
    w_rows = small_s[0, :, _L_WI:_L_WI + IDX_HEADS].reshape(nsb, t_new, IDX_HEADS)
    w_rows = jnp.broadcast_to(w_rows.transpose(0, 2, 1).reshape(nsb, IDX_HEADS * t_new, 1),
                              (nsb, IDX_HEADS * t_new, LANES))
    qb_r = _head_major(qb_s, nsb, t_new, DSA_HEADS)
    qbd = jnp.where(_np_head_mask(t_new), jnp.tile(qb_r, (1, 1, DSA_HEADS)), jnp.zeros((), BF16))
    new_rows = lambda a: _pad_rows(a.reshape(nsb, t_new, -1), PAGE_SIZE).astype(BF16)
    ckvn = new_rows(ckv_s)
    kpen = new_rows(small_s[..., _L_KPE:_L_KPE + MLA_ROPE])
    ikn = new_rows(small_s[..., _L_KI:_L_KI + IDX_DIM])
    kbn, vbn = new_rows(kb_s), new_rows(vb_s)
    n_pool = cache_dsa_k.shape[1]
    keys_minor = lambda pool: jnp.moveaxis(pool[l], 1, -1).reshape(n_pool, -1, PAGE_SIZE)
    mla_s, scp, scn = _smla_call(page_table, qabs, qpe_r, qi_r, w_rows, ckvn, kpen, ikn,
                                 wuv2.astype(BF16), cache_mla_ckv[l], keys_minor(cache_mla_kpe),
                                 keys_minor(cache_idx_k), t_new)
    dsa_s = _sdsa_call(page_table, qbd, scp, scn, kbn, vbn,
                       keys_minor(cache_dsa_k), keys_minor(cache_dsa_v), t_new)
    x2_s, h2_s = _merge_call(xs, mla_s.reshape(1, ns, HD), dsa_s.reshape(1, ns, HD), gate_s,
                             wa, wb, wo, g2, ns, "merge_sample")
    st = state_ffn_conv[l]
    zrow = jnp.zeros((nsb, 1, d_ff), st.dtype)
    s1 = jnp.concatenate([st[:, 1:2]] + [zrow] * (t_new - 1), axis=1).reshape(1, ns, d_ff)
    s2 = jnp.concatenate([st[:, 0:1], st[:, 1:2]] + [zrow] * (t_new - 2), axis=1).reshape(1, ns, d_ff)
    y_s, tail_s = _ffn_call(h2_s, x2_s, s1, s2, wg, wu, wd, cw, cb, gf, ns, t_new, True,
                            "ffn_sample")

    y_prompt = y_p[:, N_META:l_valid]
    y_sample = y_s.reshape(nsb, t_new, d)
    cut = lambda a: a[:, :l_valid]
    new_ckv_p = ckv_p[None]
    new_kpe_p = cut(small_p)[..., _L_KPE:_L_KPE + MLA_ROPE][None]
    pos_last = lambda a: jnp.moveaxis(
        a.reshape(nb, DSA_HEADS, DSA_HEAD_DIM, l_valid), -1, 1)[None]
    new_k_p, new_v_p = pos_last(kb_p), pos_last(vb_p)
    new_ik_p = cut(small_p)[..., _L_KI:_L_KI + IDX_DIM][None]
    new_conv_p = tail_p[:, SUBLANES - (CONV_W - 1):][None]
    per_s = lambda a: a.reshape(nsb, t_new, -1)
    new_ckv_s = per_s(ckv_s)[None]
    new_kpe_s = per_s(small_s)[..., _L_KPE:_L_KPE + MLA_ROPE][None]
    new_k_s = per_s(kb_s).reshape(1, nsb, t_new, DSA_HEADS, DSA_HEAD_DIM)
    new_v_s = per_s(vb_s).reshape(1, nsb, t_new, DSA_HEADS, DSA_HEAD_DIM)
    new_ik_s = per_s(small_s)[..., _L_KI:_L_KI + IDX_DIM][None]
    new_conv_s = per_s(tail_s)[:, t_new - (CONV_W - 1):][None]
    return (y_prompt, y_sample, new_ckv_p, new_kpe_p, new_k_p, new_v_p, new_ik_p, new_conv_p,
            new_ckv_s, new_kpe_s, new_k_s, new_v_s, new_ik_s, new_conv_s)


def _np_head_mask(t_new):
    r = np.arange(DSA_HEADS * t_new)[:, None] // t_new
    c = np.arange(HD)[None, :] // DSA_HEAD_DIM
    return jnp.asarray(r == c)[None]
```

```python
import functools

import numpy as np
import jax
import jax.numpy as jnp
from jax import lax
from jax.experimental import pallas as pl
from jax.experimental.pallas import tpu as pltpu

N_META = 16
MLA_HEADS = 8
MLA_NOPE = 64
MLA_ROPE = 32
MLA_V = 64
MLA_KV_LORA = 256
MLA_Q_LORA = 768
MLA_SCALE = (MLA_NOPE + MLA_ROPE) ** -0.5
DSA_HEADS = 8
DSA_HEAD_DIM = 64
DSA_SCALE = DSA_HEAD_DIM ** -0.5
IDX_HEADS = 8
IDX_DIM = 64
IDX_SCALE = IDX_DIM ** -0.5
LOG2E = 1.4426950408889634
MLA_QSCALE = MLA_SCALE * LOG2E
DSA_QSCALE = DSA_SCALE * LOG2E
TOPK_MAX = 256
CONV_W = 3
ROPE_THETA = 10000.0
EPS = 1e-6
NEG_INF = -1e30
PAGE_SIZE = 128

LANES = 128
SUBLANES = 8
HD = 512
VMEM_LIMIT = 56 * 1024 * 1024

F32 = jnp.float32
BF16 = jnp.bfloat16

_C_Q, _C_KV, _Q_B, _K_B, _V_B, _Q_I, _G_A, _G_B, _SMALL, _D_IN_P = (
    0, 768, 1024, 1536, 2048, 2560, 3072, 4096, 5120, 5248)
_L_KI, _L_KPE, _L_WI = 0, 64, 96


def _dot(a, b):
    return jnp.dot(a, b, preferred_element_type=F32)


def _dot_t(a, b):
    return lax.dot_general(a, b, (((1,), (1,)), ((), ())), preferred_element_type=F32)


def _rms(x, g):
    ms = jnp.mean(x * x, axis=-1, keepdims=True)
    return x * lax.rsqrt(ms + EPS) * g


def _rope_tables(pos):
    pos = jnp.asarray(pos).astype(F32)[:, None]
    lane = np.arange(LANES)
    out = []
    for width in (64, 32):
        half = width // 2
        m = lane % width
        inv_freq = 1.0 / (ROPE_THETA ** (jnp.arange(half, dtype=F32) / half))
        ang = pos * inv_freq[m % half][None, :]
        c, s = jnp.cos(ang), jnp.sin(ang)
        out += [c, jnp.where(m >= half, s, 0.0), jnp.where(m < half, -s, 0.0)]
    return out


def _rope_blk(x, c, sa, sb, half):
    return x * c + pltpu.roll(x, half, 1) * sa + pltpu.roll(x, LANES - half, 1) * sb


def _proj_kernel(sample, x_ref, c64_ref, sa64_ref, sb64_ref, c32_ref, sa32_ref, sb32_ref,
                 g1_ref, gq_ref, gkv_ref, win_ref, wuq_ref, wkv_ref, *outs):
    if sample:
        (ckv_o, kb_o, vb_o, small_o, qlat_o, qp_o, qb_o, qi_o, gate_o) = outs
    else:
        (ckv_o, kb_o, vb_o, small_o, qn_o, qp_o, qb_o, qi_o, gate_o,
         kcat_o, vm_o, kbb_o, vbb_o, ki2_o) = outs
    hb = _rms(x_ref[0], g1_ref[...]).astype(BF16)
    c64, sa64, sb64 = c64_ref[...], sa64_ref[...], sb64_ref[...]
    c32, sa32, sb32 = c32_ref[...], sa32_ref[...], sb32_ref[...]

    def proj(a, b):
        return _dot(hb, win_ref[:, a:b])

    cq = _rms(proj(_C_Q, _C_KV), gq_ref[...]).astype(BF16)
    ckv = _rms(proj(_C_KV, _Q_B), gkv_ref[...])
    ckv_o[0] = ckv

    zq = proj(_Q_B, _K_B)
    zk = proj(_K_B, _V_B)
    zi = proj(_Q_I, _G_A)
    for blk in range(HD // LANES):
        sl = slice(blk * LANES, (blk + 1) * LANES)
        qb_o[0, :, sl] = (_rope_blk(zq[:, sl], c64, sa64, sb64, 32) * DSA_QSCALE).astype(BF16)
        qi_o[0, :, sl] = (_rope_blk(zi[:, sl], c64, sa64, sb64, 32) * IDX_SCALE).astype(BF16)
        kr = _rope_blk(zk[:, sl], c64, sa64, sb64, 32)
        if sample:
            kb_o[0, :, sl] = kr
        else:
            kb_o[0, sl, :] = kr.T
            kbb_o[0, :, sl] = kr.astype(BF16)
    zv = proj(_V_B, _Q_I)
    if sample:
        vb_o[0] = zv
    else:
        for blk in range(HD // LANES):
            sl = slice(blk * LANES, (blk + 1) * LANES)
            zvt = zv[:, sl].T
            vb_o[0, sl, :] = zvt
            vbb_o[0, sl, :] = zvt.astype(BF16)

    gate_o[0] = jax.nn.sigmoid(proj(_G_A, _SMALL)).astype(BF16)

    zs = proj(_SMALL, _D_IN_P)
    lane = lax.broadcasted_iota(jnp.int32, (1, LANES), 1)
    m_ki = (lane < _L_KPE).astype(F32)
    m_kpe = ((lane >= _L_KPE) & (lane < _L_WI)).astype(F32)
    m_wi = ((lane >= _L_WI) & (lane < _L_WI + IDX_HEADS)).astype(F32)
    small = (zs * (c64 * m_ki + c32 * m_kpe + (IDX_HEADS ** -0.5) * m_wi)
             + pltpu.roll(zs, 32, 1) * (sa64 * m_ki) + pltpu.roll(zs, 96, 1) * (sb64 * m_ki)
             + pltpu.roll(zs, 16, 1) * (sa32 * m_kpe) + pltpu.roll(zs, 112, 1) * (sb32 * m_kpe))
    small_o[0] = small
    if not sample:
        ki = small * m_ki
        ki2_o[0] = (ki + pltpu.roll(ki, 64, 1)).astype(BF16)
        kp = pltpu.roll(small * m_kpe, 64, 1)
        kpe4 = (kp + pltpu.roll(kp, 32, 1) + pltpu.roll(kp, 64, 1)
                + pltpu.roll(kp, 96, 1)).astype(BF16)
        for pair in range(MLA_HEADS // 2):
            kcat_o[0, :, (2 * pair + 1) * LANES:(2 * pair + 2) * LANES] = kpe4

    q = _dot(cq, wuq_ref[...])
    qn = (q[:, :HD] * MLA_QSCALE).astype(BF16)
    for blk in range(2):
        qpe = q[:, HD + blk * LANES:HD + (blk + 1) * LANES]
        qp_o[0, :, blk * LANES:(blk + 1) * LANES] = (
            _rope_blk(qpe, c32, sa32, sb32, 16) * MLA_QSCALE).astype(BF16)

    if sample:
        qlat_o[0] = _dot(qn, wkv_ref[...]).astype(BF16)
    else:
        qn_o[0] = qn
        kv = _dot(ckv.astype(BF16), wkv_ref[...])
        for pair in range(MLA_HEADS // 2):
            kcat_o[0, :, 2 * pair * LANES:(2 * pair + 1) * LANES] = (
                kv[:, pair * LANES:(pair + 1) * LANES].astype(BF16))
        for blk in range(HD // LANES):
            vm_o[0, blk * LANES:(blk + 1) * LANES, :] = (
                kv[:, HD + blk * LANES:HD + (blk + 1) * LANES].T.astype(BF16))


def _const_spec(shape):
    nd = len(shape)
    return pl.BlockSpec(shape, lambda *_: (0,) * nd, pipeline_mode=pl.Buffered(1))


def _proj_call(x, tables, g1, gq, gkv, win, wuq, wkv, tm, sample, l_out=None):
    nb, lp, d = x.shape
    l_out = lp if l_out is None else l_out
    grid = (nb, lp // tm)
    row = lambda w: pl.BlockSpec((1, tm, w), lambda b, j: (b, j, 0))
    tab = pl.BlockSpec((tm, LANES), lambda b, j: (j, 0))
    in_specs = ([row(d)] + [tab] * 6
                + [_const_spec(a.shape) for a in (g1, gq, gkv, win, wuq, wkv)])
    f32_w = [MLA_KV_LORA, HD, HD, LANES]
    if sample:
        bf_w = [MLA_HEADS * MLA_KV_LORA, 2 * LANES, HD, HD, 2048]
    else:
        bf_w = [HD, 2 * LANES, HD, HD, 2048, 2 * HD, HD, HD, HD, LANES]
    out_shape = ([jax.ShapeDtypeStruct((nb, lp, w), F32) for w in f32_w]
                 + [jax.ShapeDtypeStruct((nb, lp, w), BF16) for w in bf_w])
    out_specs = [row(w) for w in f32_w + bf_w]
    out_shape[0] = jax.ShapeDtypeStruct((nb, l_out, MLA_KV_LORA), F32)
    if not sample:
        n_f32 = len(f32_w)
        for i, dt, rows in ((1, F32, l_out), (2, F32, l_out), (n_f32 + 6, BF16, lp),
                            (n_f32 + 8, BF16, lp)):
            out_shape[i] = jax.ShapeDtypeStruct((nb, HD, rows), dt)
            out_specs[i] = pl.BlockSpec((1, HD, tm), lambda b, j: (b, 0, j))
    return pl.pallas_call(
        functools.partial(_proj_kernel, sample),
        grid=grid, in_specs=in_specs, out_specs=out_specs, out_shape=out_shape,
        compiler_params=pltpu.CompilerParams(
            dimension_semantics=("arbitrary", "arbitrary"), vmem_limit_bytes=VMEM_LIMIT),
        name="proj_sample" if sample else "proj_prompt",
    )(x, *tables, g1, gq, gkv, win, wuq, wkv)


_MAX_BISECT = 320
_BISECT_UNROLL = 4


def _chunk_ds(c, tk):
    start = c * tk
    return pl.ds(start if isinstance(start, int) else pl.multiple_of(start, tk), tk)


def _key_chunk(ref, c, tk, axis):
    ds = _chunk_ds(c, tk)
    return ref[ds, :] if axis == 0 else ref[:, ds]


def _key_fold(v, op, axis):
    if axis == 0:
        parts = [v[i * SUBLANES:(i + 1) * SUBLANES] for i in range(v.shape[0] // SUBLANES)]
    else:
        parts = [v[:, i * LANES:(i + 1) * LANES] for i in range(v.shape[1] // LANES)]
    while len(parts) > 1:
        parts = [op(a, b) for a, b in zip(parts[::2], parts[1::2])] + (
            [parts[-1]] if len(parts) % 2 else [])
    return parts[0]


def _key_index(c, tk, axis):
    shape = (tk, 1) if axis == 0 else (1, tk)
    return (lax.convert_element_type(c * tk, F32)
            + lax.broadcasted_iota(jnp.int32, shape, axis).astype(F32))


def _count(sc_ref, nck, tk, axis, pred):
    nq = sc_ref.shape[1 - axis]
    part = (SUBLANES, nq) if axis == 0 else (nq, LANES)

    def body(c, acc):
        v = jnp.where(pred(_key_chunk(sc_ref, c, tk, axis), c), 1.0, 0.0)
        return acc + _key_fold(v, jnp.add, axis)

    acc = lax.fori_loop(0, nck, body, jnp.zeros(part, F32))
    return jnp.sum(acc, axis=axis, keepdims=True)


def _stats_init(part):
    return (jnp.full(part, -NEG_INF, F32), jnp.full(part, NEG_INF, F32),
            jnp.zeros(part, F32), jnp.zeros(part, F32))


def _stats_update(stats, x, axis, has_masked):
    mn, mx, gt0, ge0 = stats
    xv = jnp.where(x > 0.5 * NEG_INF, x, -NEG_INF) if has_masked else x
    return (jnp.minimum(mn, _key_fold(xv, jnp.minimum, axis)),
            jnp.maximum(mx, _key_fold(x, jnp.maximum, axis)),
            gt0 + _key_fold(jnp.where(x > 0.0, 1.0, 0.0), jnp.add, axis),
            ge0 + _key_fold(jnp.where(x >= 0.0, 1.0, 0.0), jnp.add, axis))


def _select_threshold(sc_ref, nck, tk, k_eff, n_valid, axis, stats=None):
    nq = sc_ref.shape[1 - axis]
    part = (SUBLANES, nq) if axis == 0 else (nq, LANES)

    if stats is None:
        stats = lax.fori_loop(
            0, nck, lambda c, st: _stats_update(st, _key_chunk(sc_ref, c, tk, axis), axis, True),
            _stats_init(part))
    mn, mx, gt0, ge0 = stats
    lo0 = jnp.min(mn, axis=axis, keepdims=True)
    hi0 = jnp.max(mx, axis=axis, keepdims=True)
    hi0 = hi0 + jnp.maximum(jnp.abs(hi0), 1.0) * 1e-6

    def count_ge(thr):
        return _count(sc_ref, nck, tk, axis, lambda x, c: x >= thr)

    def all_rows(fin):
        return jnp.min(jnp.where(fin, 1.0, 0.0)) > 0.5

    c_gt0 = jnp.sum(gt0, axis=axis, keepdims=True)
    c_ge0 = jnp.sum(ge0, axis=axis, keepdims=True)
    at_zero = (c_gt0 < k_eff) & (c_ge0 >= k_eff)
    above = c_gt0 >= k_eff
    inside = (lo0 < 0.0) & (hi0 > 0.0)
    lo_at_zero = at_zero | (above & inside)
    c_lo0 = jnp.where(lo_at_zero, c_ge0, n_valid)
    lo0, hi0 = (jnp.where(lo_at_zero, 0.0, lo0),
                jnp.where(at_zero | (~above & inside), 0.0, hi0))

    def cond(st):
        it, _, _, _, done = st
        return jnp.logical_and(it < _MAX_BISECT, jnp.logical_not(done))

    def body(st):
        it, lo, hi, c_lo, _ = st
        for _ in range(_BISECT_UNROLL):
            mid = 0.5 * lo + 0.5 * hi
            c = count_ge(mid)
            ge = c >= k_eff
            fin = (mid <= lo) | (mid >= hi)
            lo, hi, c_lo = jnp.where(ge, mid, lo), jnp.where(ge, hi, mid), jnp.where(ge, c, c_lo)
            fin = fin | (c_lo == k_eff)
        return it + _BISECT_UNROLL, lo, hi, c_lo, all_rows(fin)

    fin0 = (c_lo0 == k_eff) | (lo0 >= hi0)
    _, t, _, c_t, _ = lax.while_loop(
        cond, body, (jnp.int32(0), lo0, hi0, c_lo0, all_rows(fin0)))

    tie_rows = c_t > k_eff

    @pl.when(jnp.max(jnp.where(tie_rows, 1.0, 0.0)) > 0.5)
    def _():
        n_keys = sc_ref.shape[axis]
        need = k_eff - _count(sc_ref, nck, tk, axis, lambda x, c: x > t)

        def tie_count(m):
            return _count(sc_ref, nck, tk, axis,
                          lambda x, c: (x == t) & (_key_index(c, tk, axis) <= m))

        def ibody(_, st):
            lo_i, hi_i = st
            mid = jnp.floor(0.5 * (lo_i + hi_i))
            ok = tie_count(mid) >= need
            return jnp.where(ok, lo_i, mid), jnp.where(ok, mid, hi_i)

        steps = int(np.ceil(np.log2(n_keys))) + 1
        _, m_idx = lax.fori_loop(0, steps, ibody,
                                 (jnp.full(t.shape, -1.0, F32),
                                  jnp.full(t.shape, float(n_keys - 1), F32)))

        def fix(c, carry):
            x = _key_chunk(sc_ref, c, tk, axis)
            drop = (x == t) & (_key_index(c, tk, axis) > m_idx) & tie_rows
            ds = _chunk_ds(c, tk)
            if axis == 0:
                sc_ref[ds, :] = jnp.where(drop, NEG_INF, x)
            else:
                sc_ref[:, ds] = jnp.where(drop, NEG_INF, x)
            return carry

        lax.fori_loop(0, nck, fix, 0)

    return t


def _lane_blocks(s):
    return [s[:, i * LANES:(i + 1) * LANES] for i in range(s.shape[1] // LANES)]


def _scale_cols(alpha, x):
    return jnp.concatenate([alpha * b for b in _lane_blocks(x)], axis=1)


def _online_update(m_ref, l_ref, acc_ref, idx, blocks, pv):
    m = m_ref[idx]
    bm = blocks[0]
    for b in blocks[1:]:
        bm = jnp.maximum(bm, b)
    m_new = jnp.maximum(m, jnp.max(bm, axis=1, keepdims=True))
    alpha = jnp.exp2(m - m_new)
    ps = [jnp.exp2(b - m_new) for b in blocks]
    rs = ps[0]
    for p in ps[1:]:
        rs = rs + p
    l_ref[idx] = alpha * l_ref[idx] + jnp.sum(rs, axis=1, keepdims=True)
    acc_ref[idx] = _scale_cols(alpha, acc_ref[idx]) + pv([p.astype(BF16) for p in ps])
    m_ref[idx] = m_new


def _reset_state(m_ref, l_ref, acc_ref):
    m_ref[...] = jnp.full_like(m_ref, NEG_INF)
    l_ref[...] = jnp.zeros_like(l_ref)
    acc_ref[...] = jnp.zeros_like(acc_ref)


def _softmax_keys_major(m_ref, l_ref, idx, s):
    m = m_ref[idx]
    m_new = jnp.maximum(m, jnp.max(_key_fold(s, jnp.maximum, 0), axis=0, keepdims=True))
    alpha = jnp.exp2(m - m_new)
    p = jnp.exp2(s - m_new)
    l_ref[idx] = alpha * l_ref[idx] + jnp.sum(_key_fold(p, jnp.add, 0), axis=0, keepdims=True)
    m_ref[idx] = m_new
    return alpha, p.astype(BF16)


def _attn_kernel(tq, tk, n_keep, topk,
                 qn_ref, qp_ref, qb_ref, qi_ref, small_ref,
                 kcat_ref, vm_ref, kb_ref, vb_ref, ki2_ref,
                 mla_ref, dsa_ref, sc_ref, qc_ref, qd_ref, qx_ref, w_ref, st_ref, m_ref, l_ref,
                 acc_ref):
    j = pl.program_id(1)
    nh = MLA_HEADS
    n_pairs = nh // 2

    @pl.when(j >= n_keep)
    def _():
        mla_ref[...] = jnp.zeros_like(mla_ref)
        dsa_ref[...] = jnp.zeros_like(dsa_ref)

    @pl.when(j < n_keep)
    def _():
        nck = (j * tq + tq - 1) // tk + 1
        last = nck - 1
        lane = lax.broadcasted_iota(jnp.int32, (1, LANES), 1)
        q_row = j * tq + lax.broadcasted_iota(jnp.int32, (1, tq), 1)
        k_row = lax.broadcasted_iota(jnp.int32, (tk, 1), 0)
        zero_b = jnp.zeros((tq, LANES), BF16)

        def ksl(c):
            return _chunk_ds(c, tk)

        def pair_sl(pair):
            return slice(pair * LANES, (pair + 1) * LANES)

        def two(x):
            return jnp.concatenate([x, x], axis=1)

        def write_heads(out_ref):
            half = LANES // 2
            for pair in range(n_pairs):
                o = acc_ref[pair] / l_ref[pair]
                o = jnp.concatenate([o[:half, :tq], o[half:, tq:]], axis=0)
                out_ref[0, :, pair_sl(pair)] = o.T.astype(BF16)

        small = small_ref[0]
        for h in range(nh):
            pair, sub = divmod(h, 2)
            grp, gsub = divmod(h, 4)
            rows = slice(sub * tq, (sub + 1) * tq)
            own = (lane >= 64) == bool(sub)
            qc_ref[pair, rows, :LANES] = jnp.where(own, qn_ref[0, :, pair_sl(pair)], zero_b)
            qc_ref[pair, rows, LANES:] = jnp.where((lane // 32) == gsub,
                                                   qp_ref[0, :, pair_sl(grp)], zero_b)
            qd_ref[pair, rows, :] = jnp.where(own, qb_ref[0, :, pair_sl(pair)], zero_b)
            qx_ref[h * tq:(h + 1) * tq, :] = jnp.where(own, qi_ref[0, :, pair_sl(pair)], zero_b)
        w_ref[...] = small.T[_L_WI:_L_WI + IDX_HEADS]

        def attend(c, score_fn, fix_fn, vt_ref):
            ks = ksl(c)
            s = {p: score_fn(p, ks) for p in range(min(2, n_pairs))}
            for p in range(n_pairs):
                alpha, pr = _softmax_keys_major(m_ref, l_ref, p, fix_fn(s.pop(p)))
                if p + 2 < n_pairs:
                    s[p + 2] = score_fn(p + 2, ks)
                acc_ref[p] = alpha * acc_ref[p] + _dot(vt_ref[0, pair_sl(p), ks], pr)

        _reset_state(m_ref, l_ref, acc_ref)
        for i, part in enumerate(_stats_init((SUBLANES, tq))):
            st_ref[i] = part

        def mla_scores(p, ks):
            return _dot_t(kcat_ref[0, ks, 2 * p * LANES:(2 * p + 2) * LANES], qc_ref[p])

        def mla_idx_chunk(c, masked):
            ks = ksl(c)
            r = _dot_t(ki2_ref[0, ks, :], qx_ref[...])
            sc = None
            for h in range(IDX_HEADS):
                term = jnp.maximum(r[:, h * tq:(h + 1) * tq], 0.0) * w_ref[h:h + 1, :]
                sc = term if sc is None else sc + term
            if masked:
                sc = jnp.where((c * tk + k_row) <= q_row, sc, NEG_INF)
                vis2 = (c * tk + k_row) <= two(q_row)
                fix = lambda s: jnp.where(vis2, s, NEG_INF)
            else:
                fix = lambda s: s
            sc_ref[ks, :] = sc
            for i, part in enumerate(_stats_update(tuple(st_ref[i] for i in range(4)), sc, 0,
                                                   masked)):
                st_ref[i] = part
            attend(c, mla_scores, fix, vm_ref)

        def mla_idx_step(i, carry):
            mla_idx_chunk(2 * i, False)
            mla_idx_chunk(2 * i + 1, False)
            return carry

        lax.fori_loop(0, last // 2, mla_idx_step, 0)

        @pl.when(last % 2 == 1)
        def _():
            mla_idx_chunk(last - 1, False)

        mla_idx_chunk(last, True)
        write_heads(mla_ref)

        n_valid = (q_row + 1).astype(F32)
        k_eff = jnp.minimum(n_valid, float(topk))
        t2 = two(_select_threshold(sc_ref, nck, tk, k_eff, n_valid, 0,
                                   tuple(st_ref[i] for i in range(4))))

        _reset_state(m_ref, l_ref, acc_ref)

        def dsa_scores(p, ks):
            return _dot_t(kb_ref[0, ks, pair_sl(p)], qd_ref[p])

        def dsa_chunk(c):
            sel = two(sc_ref[ksl(c), :]) >= t2
            attend(c, dsa_scores, lambda s: jnp.where(sel, s, NEG_INF), vb_ref)

        def dsa_step(i, carry):
            dsa_chunk(2 * i)
            dsa_chunk(2 * i + 1)
            return carry

        lax.fori_loop(0, nck // 2, dsa_step, 0)

        @pl.when(nck % 2 == 1)
        def _():
            dsa_chunk(nck - 1)

        write_heads(dsa_ref)


def _attn_call(qn, qp, qb, qi, small, kcat, vm, kbb, vbb, ki2, l_valid, tq, tk):
    nb, lp, _ = qn.shape
    n_keep = -(-l_valid // tq)
    topk = min(TOPK_MAX, l_valid // 4)
    n_pairs = MLA_HEADS // 2
    qrow = lambda w: pl.BlockSpec((1, tq, w), lambda b, j: (b, j, 0))
    krow = lambda w: pl.BlockSpec((1, lp, w), lambda b, j: (b, 0, 0))
    vcol = pl.BlockSpec((1, HD, lp), lambda b, j: (b, 0, 0))
    in_specs = [qrow(HD), qrow(2 * LANES), qrow(HD), qrow(HD), qrow(LANES),
                krow(2 * HD), vcol, krow(HD), vcol, krow(LANES)]
    return pl.pallas_call(
        functools.partial(_attn_kernel, tq, tk, n_keep, topk),
        grid=(nb, lp // tq), in_specs=in_specs,
        out_specs=[qrow(HD), qrow(HD)],
        out_shape=[jax.ShapeDtypeStruct((nb, lp, HD), BF16)] * 2,
        scratch_shapes=[pltpu.VMEM((lp, tq), F32),
                        pltpu.VMEM((n_pairs, 2 * tq, 2 * LANES), BF16),
                        pltpu.VMEM((n_pairs, 2 * tq, LANES), BF16),
                        pltpu.VMEM((IDX_HEADS * tq, LANES), BF16),
                        pltpu.VMEM((IDX_HEADS, tq), F32),
                        pltpu.VMEM((4, SUBLANES, tq), F32),
                        pltpu.VMEM((n_pairs, 1, 2 * tq), F32),
                        pltpu.VMEM((n_pairs, 1, 2 * tq), F32),
                        pltpu.VMEM((n_pairs, LANES, 2 * tq), F32)],
        compiler_params=pltpu.CompilerParams(
            dimension_semantics=("arbitrary", "arbitrary"), vmem_limit_bytes=VMEM_LIMIT),
        name="attn_prompt",
    )(qn, qp, qb, qi, small, kcat, vm, kbb, vbb, ki2)


def _merge_kernel(x_ref, mla_ref, dsa_ref, gate_ref, wa_ref, wb_ref, wo_ref, g2_ref,
                  x2_ref, h2_ref):
    d = x_ref.shape[-1]
    a = _dot(mla_ref[0], wa_ref[...])
    b = _dot(dsa_ref[0], wb_ref[...])
    g = gate_ref[0]
    o = g[:, :d].astype(F32) * a + g[:, d:].astype(F32) * b
    x2 = x_ref[0] + _dot(o.astype(BF16), wo_ref[...])
    x2_ref[0] = x2
    h2_ref[0] = _rms(x2, g2_ref[...]).astype(BF16)


def _merge_call(x, mla, dsa, gate, wa, wb, wo, g2, tm, name):
    nb, lp, d = x.shape
    row = lambda w: pl.BlockSpec((1, tm, w), lambda b, j: (b, j, 0))
    return pl.pallas_call(
        _merge_kernel, grid=(nb, lp // tm),
        in_specs=[row(d), row(HD), row(HD), row(2 * d)]
        + [_const_spec(a.shape) for a in (wa, wb, wo, g2)],
        out_specs=[row(d), row(d)],
        out_shape=[jax.ShapeDtypeStruct((nb, lp, d), F32), jax.ShapeDtypeStruct((nb, lp, d), BF16)],
        compiler_params=pltpu.CompilerParams(
            dimension_semantics=("arbitrary", "arbitrary"), vmem_limit_bytes=VMEM_LIMIT),
        name=name,
    )(x, mla, dsa, gate, wa, wb, wo, g2)


_FF_CHUNK = 512
_FF_AHEAD = 1


def _ffn_kernel(seq_rows, tail_tile, tail_off, h_ref, x_ref, s1_ref, s2_ref,
                wg_ref, wu_ref, wd_ref, cw_ref, cb_ref, gf_ref, y_ref, tail_ref, prev_ref):
    j = pl.program_id(1)
    tm = h_ref.shape[1]
    d_ff = wg_ref.shape[1]
    h = h_ref[0]
    row = lax.broadcasted_iota(jnp.int32, (tm, 1), 0)
    if seq_rows is None:
        @pl.when(j == 0)
        def _():
            prev_ref[...] = jnp.zeros_like(prev_ref)
        first1, first2 = row < 1, row < 2
    else:
        first1, first2 = (row % seq_rows) < 1, (row % seq_rows) < 2

    acc = jnp.zeros((tm, x_ref.shape[-1]), F32)
    chunks = [slice(c0, min(c0 + _FF_CHUNK, d_ff)) for c0 in range(0, d_ff, _FF_CHUNK)]

    def up(sl):
        return _dot(h, wg_ref[:, sl]), _dot(h, wu_ref[:, sl])

    ahead = [up(sl) for sl in chunks[:_FF_AHEAD]]
    for i, sl in enumerate(chunks):
        g, u = ahead.pop(0)
        if i + _FF_AHEAD < len(chunks):
            ahead.append(up(chunks[i + _FF_AHEAD]))
        if seq_rows is None:
            p = prev_ref[:, sl]
            hist1 = jnp.broadcast_to(p[SUBLANES - 1:SUBLANES], g.shape)
            hist2 = jnp.where(row < 1, jnp.broadcast_to(p[SUBLANES - 2:SUBLANES - 1], g.shape),
                              hist1)
            prev_ref[:, sl] = g[tm - SUBLANES:]
        else:
            hist1, hist2 = s1_ref[0, :, sl], s2_ref[0, :, sl]
        g1 = jnp.where(first1, hist1, pltpu.roll(g, 1, 0))
        g2 = jnp.where(first2, hist2, pltpu.roll(g, 2, 0))
        cw = cw_ref[:, sl]
        gc = cb_ref[:, sl] + cw[0:1] * g2 + cw[1:2] * g1 + cw[2:3] * g
        act = (gc * jax.nn.sigmoid(gc) * u).astype(BF16)
        acc = acc + _dot(act, wd_ref[sl, :])
        if seq_rows is None:
            @pl.when(j == tail_tile)
            def _(g=g, sl=sl):
                tail_ref[0, :, sl] = g[tail_off:tail_off + SUBLANES]
        else:
            tail_ref[0, :, sl] = g
    y_ref[0] = _rms(x_ref[0] + acc, gf_ref[...])


def _ffn_call(h2, x2, s1, s2, wg, wu, wd, cw, cb, gf, tm, l_valid, sample, name):
    nb, lp, d = x2.shape
    d_ff = wg.shape[1]
    row = lambda w: pl.BlockSpec((1, tm, w), lambda b, j: (b, j, 0))
    if sample:
        seq_rows, tail_tile, tail_off = l_valid, 0, 0
        tail_shape, tail_spec = (nb, lp, d_ff), row(d_ff)
        s_spec = row(d_ff)
    else:
        seq_rows = None
        tail_tile, tail_off = divmod(l_valid - SUBLANES, tm)
        tail_shape = (nb, SUBLANES, d_ff)
        tail_spec = pl.BlockSpec((1, SUBLANES, d_ff), lambda b, j: (b, 0, 0))
        s_spec = pl.BlockSpec((1, SUBLANES, d_ff), lambda b, j: (0, 0, 0))
    return pl.pallas_call(
        functools.partial(_ffn_kernel, seq_rows, tail_tile, tail_off),
        grid=(nb, lp // tm),
        in_specs=[row(d), row(d), s_spec, s_spec]
        + [_const_spec(a.shape) for a in (wg, wu, wd, cw, cb, gf)],
        out_specs=[row(d), tail_spec],
        out_shape=[jax.ShapeDtypeStruct((nb, lp, d), F32), jax.ShapeDtypeStruct(tail_shape, F32)],
        scratch_shapes=[pltpu.VMEM((SUBLANES, d_ff), F32)],
        compiler_params=pltpu.CompilerParams(
            dimension_semantics=("arbitrary", "arbitrary"), vmem_limit_bytes=VMEM_LIMIT),
        name=name,
    )(h2, x2, s1, s2, wg, wu, wd, cw, cb, gf)


def _head_rows_mask(rows_per_head, n_heads, width_per_head):
    r = lax.broadcasted_iota(jnp.int32, (n_heads * rows_per_head, 1), 0) // rows_per_head
    c = lax.broadcasted_iota(jnp.int32, (1, n_heads * width_per_head), 1) // width_per_head
    return r == c


def _diag_heads(full, t, n_heads, width):
    col_head = lax.broadcasted_iota(jnp.int32, (1, n_heads * width), 1) // width
    out = jnp.zeros((t, n_heads * width), F32)
    for h in range(n_heads):
        out = jnp.where(col_head == h, full[h * t:(h + 1) * t], out)
    return out


def _smla_kernel(npp, t_new, pt_ref, qabs_ref, qpe_ref, qi_ref, wrow_ref,
                 ckvn_ref, kpen_ref, ikn_ref, wuv_ref, *rest):
    ckv_pages = rest[:npp]
    kpe_pages = rest[npp:2 * npp]
    ik_pages = rest[2 * npp:3 * npp]
    mla_ref, scp_ref, scn_ref, m_ref, l_ref, acc_ref, ck_ref, kpe_ref, ik_ref = rest[3 * npp:]
    c = pl.program_id(1)
    rows = qabs_ref.shape[1]
    qabs, qpe, qi, wrow = qabs_ref[0], qpe_ref[0], qi_ref[0], wrow_ref[0]

    @pl.when(c == 0)
    def _():
        _reset_state(m_ref, l_ref, acc_ref)

    def update(s, vals):
        _online_update(m_ref, l_ref, acc_ref, 0, _lane_blocks(s),
                       lambda ps: _dot(jnp.concatenate(ps, axis=1), vals))

    def idx_score(s):
        s = _scale_cols(wrow, jnp.maximum(s, 0.0))
        out = s[:t_new]
        for h in range(1, IDX_HEADS):
            out = out + s[h * t_new:(h + 1) * t_new]
        return out

    for i in range(npp):
        ck_ref[i * PAGE_SIZE:(i + 1) * PAGE_SIZE, :] = ckv_pages[i][...].astype(BF16)
        kpe_ref[:, i * PAGE_SIZE:(i + 1) * PAGE_SIZE] = kpe_pages[i][...].astype(BF16)
        ik_ref[:, i * PAGE_SIZE:(i + 1) * PAGE_SIZE] = ik_pages[i][...].astype(BF16)
    ck_all = ck_ref[...]
    update(_dot_t(qabs, ck_all) + _dot(qpe, kpe_ref[...]), ck_all)
    scp_ref[0] = idx_score(_dot(qi, ik_ref[...]))

    @pl.when(c == pl.num_programs(1) - 1)
    def _():
        ck = ckvn_ref[0]
        tok = lax.broadcasted_iota(jnp.int32, (rows, 1), 0) % t_new
        key = lax.broadcasted_iota(jnp.int32, (1, PAGE_SIZE), 1)
        vis = key <= tok
        s = _dot_t(qabs, ck) + _dot_t(qpe, kpen_ref[0])
        update(jnp.where(vis, s, NEG_INF), ck)
        scn_ref[0] = jnp.where(vis[:t_new], idx_score(_dot_t(qi, ikn_ref[0])), NEG_INF)
        o_lat = _scale_cols(1.0 / l_ref[0], acc_ref[0]).astype(BF16)
        mla_ref[0] = _diag_heads(_dot(o_lat, wuv_ref[...]), t_new, MLA_HEADS, MLA_V).astype(BF16)


def _sdsa_kernel(npp, t_new, topk, past_len, sel_tk, pt_ref, qbd_ref, scp_ref, scn_ref, kbn_ref,
                 vbn_ref, *rest):
    k_pages = rest[:npp]
    v_pages = rest[npp:2 * npp]
    dsa_ref, sc_ref, t_ref, m_ref, l_ref, acc_ref, kt_ref, vt_ref = rest[2 * npp:]
    c = pl.program_id(1)
    rows = qbd_ref.shape[1]
    qbd = qbd_ref[0]
    step_keys = npp * PAGE_SIZE

    @pl.when(c == 0)
    def _():
        _reset_state(m_ref, l_ref, acc_ref)
        sc_ref[:, :past_len] = scp_ref[0]
        sc_ref[:, past_len:past_len + PAGE_SIZE] = scn_ref[0]
        if sel_tk > PAGE_SIZE:
            sc_ref[:, past_len + PAGE_SIZE:] = jnp.full((t_new, sel_tk - PAGE_SIZE), NEG_INF, F32)
        tok = lax.broadcasted_iota(jnp.int32, (t_new, 1), 0)
        n_valid = (past_len + 1 + tok).astype(F32)
        k_eff = jnp.minimum(n_valid, float(topk))
        t_ref[...] = _select_threshold(sc_ref, (past_len + sel_tk) // sel_tk, sel_tk,
                                       k_eff, n_valid, 1)

    t = t_ref[...]

    def masked(s, sc):
        pen = jnp.where(sc >= t, 0.0, NEG_INF)
        sel = jnp.concatenate([pen] * DSA_HEADS, axis=0) == 0.0
        return jnp.where(sel, s, NEG_INF)

    def update(s, pv_dot, vals):
        _online_update(m_ref, l_ref, acc_ref, 0, _lane_blocks(s),
                       lambda ps: pv_dot(jnp.concatenate(ps, axis=1), vals))

    for i in range(npp):
        kt_ref[:, i * PAGE_SIZE:(i + 1) * PAGE_SIZE] = k_pages[i][...].astype(BF16)
        vt_ref[:, i * PAGE_SIZE:(i + 1) * PAGE_SIZE] = v_pages[i][...].astype(BF16)
    start = c * step_keys
    start = start if isinstance(start, int) else pl.multiple_of(start, step_keys)
    update(masked(_dot(qbd, kt_ref[...]), sc_ref[:, pl.ds(start, step_keys)]), _dot_t, vt_ref[...])

    @pl.when(c == pl.num_programs(1) - 1)
    def _():
        s = masked(_dot_t(qbd, kbn_ref[0]), sc_ref[:, past_len:past_len + PAGE_SIZE])
        update(s, _dot, vbn_ref[0])
        o = _scale_cols(1.0 / l_ref[0], acc_ref[0])
        dsa_ref[0] = _diag_heads(o, t_new, DSA_HEADS, DSA_HEAD_DIM).astype(BF16)


def _pages_per_step(n_pages, want):
    p = min(want, n_pages)
    while n_pages % p:
        p -= 1
    return p


def _page_specs(npp, rows, width):
    return [pl.BlockSpec((None, rows, width),
                         functools.partial(lambda i, b, c, pt: (pt[b, c * npp + i], 0, 0), i))
            for i in range(npp)]


def _smla_call(page_table, qabs, qpe, qi, wrow, ckvn, kpen, ikn, wuv, pool_ckv, pool_kpe, pool_ik,
               t_new):
    nb, n_pages = page_table.shape
    npp = _pages_per_step(n_pages, 64)
    rows = qabs.shape[1]
    past_len = n_pages * PAGE_SIZE
    per_b = lambda shape: pl.BlockSpec((1,) + shape, lambda b, c, pt: (b,) + (0,) * len(shape))
    in_specs = ([per_b((rows, MLA_KV_LORA)), per_b((rows, MLA_ROPE)), per_b((rows, IDX_DIM)),
                 per_b((rows, LANES)), per_b((PAGE_SIZE, MLA_KV_LORA)),
                 per_b((PAGE_SIZE, MLA_ROPE)), per_b((PAGE_SIZE, IDX_DIM)),
                 pl.BlockSpec(wuv.shape, lambda b, c, pt: (0, 0))]
                + _page_specs(npp, PAGE_SIZE, MLA_KV_LORA) + _page_specs(npp, MLA_ROPE, PAGE_SIZE)
                + _page_specs(npp, IDX_DIM, PAGE_SIZE))
    out_specs = [per_b((t_new, HD)),
                 pl.BlockSpec((1, t_new, npp * PAGE_SIZE), lambda b, c, pt: (b, 0, c)),
                 per_b((t_new, PAGE_SIZE))]
    out_shape = [jax.ShapeDtypeStruct((nb, t_new, HD), BF16),
                 jax.ShapeDtypeStruct((nb, t_new, past_len), F32),
                 jax.ShapeDtypeStruct((nb, t_new, PAGE_SIZE), F32)]
    grid_spec = pltpu.PrefetchScalarGridSpec(
        num_scalar_prefetch=1, grid=(nb, n_pages // npp), in_specs=in_specs, out_specs=out_specs,
        scratch_shapes=[pltpu.VMEM((1, rows, LANES), F32), pltpu.VMEM((1, rows, LANES), F32),
                        pltpu.VMEM((1, rows, MLA_KV_LORA), F32),
                        pltpu.VMEM((npp * PAGE_SIZE, MLA_KV_LORA), BF16),
                        pltpu.VMEM((MLA_ROPE, npp * PAGE_SIZE), BF16),
                        pltpu.VMEM((IDX_DIM, npp * PAGE_SIZE), BF16)])
    return pl.pallas_call(
        functools.partial(_smla_kernel, npp, t_new), grid_spec=grid_spec, out_shape=out_shape,
        compiler_params=pltpu.CompilerParams(
            dimension_semantics=("arbitrary", "arbitrary"), vmem_limit_bytes=VMEM_LIMIT),
        name="sample_mla",
    )(page_table, qabs, qpe, qi, wrow, ckvn, kpen, ikn, wuv,
      *([pool_ckv] * npp), *([pool_kpe] * npp), *([pool_ik] * npp))


def _sdsa_call(page_table, qbd, scp, scn, kbn, vbn, pool_k, pool_v, t_new):
    nb, n_pages = page_table.shape
    npp = _pages_per_step(n_pages, 32)
    rows = qbd.shape[1]
    past_len = n_pages * PAGE_SIZE
    topk = min(TOPK_MAX, (past_len + t_new) // 4)
    sel_tk = PAGE_SIZE
    while sel_tk < 2048 and past_len % (2 * sel_tk) == 0:
        sel_tk *= 2
    per_b = lambda shape: pl.BlockSpec((1,) + shape, lambda b, c, pt: (b,) + (0,) * len(shape))
    in_specs = ([per_b((rows, HD)), per_b((t_new, past_len)), per_b((t_new, PAGE_SIZE)),
                 per_b((PAGE_SIZE, HD)), per_b((PAGE_SIZE, HD))]
                + _page_specs(npp, HD, PAGE_SIZE) + _page_specs(npp, HD, PAGE_SIZE))
    grid_spec = pltpu.PrefetchScalarGridSpec(
        num_scalar_prefetch=1, grid=(nb, n_pages // npp), in_specs=in_specs,
        out_specs=[per_b((t_new, HD))],
        scratch_shapes=[pltpu.VMEM((t_new, past_len + sel_tk), F32),
                        pltpu.VMEM((t_new, 1), F32),
                        pltpu.VMEM((1, rows, LANES), F32), pltpu.VMEM((1, rows, LANES), F32),
                        pltpu.VMEM((1, rows, HD), F32),
                        pltpu.VMEM((HD, npp * PAGE_SIZE), BF16),
                        pltpu.VMEM((HD, npp * PAGE_SIZE), BF16)])
    return pl.pallas_call(
        functools.partial(_sdsa_kernel, npp, t_new, topk, past_len, sel_tk), grid_spec=grid_spec,
        out_shape=[jax.ShapeDtypeStruct((nb, t_new, HD), BF16)],
        compiler_params=pltpu.CompilerParams(
            dimension_semantics=("arbitrary", "arbitrary"), vmem_limit_bytes=VMEM_LIMIT),
        name="sample_dsa",
    )(page_table, qbd, scp, scn, kbn, vbn, *([pool_k] * npp), *([pool_v] * npp))[0]


def _pick_tile(n, candidates):
    for c in candidates:
        if n % c == 0:
            return c
    return n


def _pad_rows(a, rows):
    return jnp.pad(a, ((0, 0), (0, rows - a.shape[1]), (0, 0)))


def _head_major(a, nb, t, heads):
    w = a.shape[-1] // heads
    return a.reshape(nb, t, heads, w).transpose(0, 2, 1, 3).reshape(nb, heads * t, w)


def kernel(x_prompt, x_sample, cache_mla_ckv, cache_mla_kpe, cache_dsa_k, cache_dsa_v, cache_idx_k,
           state_ffn_conv, page_table, meta_tokens, norm1_g, w_in, g_q, g_kv, w_uq, w_uk, w_uv,
           w_br_a, w_br_b, w_o, norm2_g, w_ffn_g, w_ffn_u, ffn_conv_w, ffn_conv_b, w_ffn_d, final_g):
    depth = w_in.shape[0]
    assert depth == 1
    nb, seq, d = x_prompt.shape
    nsb, t_new, _ = x_sample.shape
    n_pages = page_table.shape[1]
    past_len = n_pages * PAGE_SIZE
    l_valid = N_META + seq
    tq, tk = 256, 256
    lp = -(-l_valid // tk) * tk
    assert l_valid % SUBLANES == 0 and t_new % SUBLANES == 0 and t_new <= PAGE_SIZE
    d_ff = w_ffn_g.shape[-1]
    l = 0

    wi = w_in[l]
    cuts = np.cumsum([MLA_Q_LORA, MLA_KV_LORA, MLA_ROPE, HD, HD, HD, HD, IDX_DIM, IDX_HEADS, d, d])
    c_q, c_kv, k_pe, q_b, k_b, v_b, q_i, k_i, w_i, g_a, g_b = jnp.split(wi, cuts[:-1], axis=1)
    pad = jnp.zeros((d, LANES - IDX_DIM - MLA_ROPE - IDX_HEADS), wi.dtype)
    win = jnp.concatenate([c_q, c_kv, q_b, k_b, v_b, q_i, g_a, g_b, k_i, k_pe, w_i, pad],
                          axis=1).astype(BF16)
    wuq = jnp.concatenate([w_uq[l][:, :, :MLA_NOPE].reshape(MLA_Q_LORA, -1),
                           w_uq[l][:, :, MLA_NOPE:].reshape(MLA_Q_LORA, -1)], axis=1).astype(BF16)
    wuk2 = w_uk[l].reshape(MLA_KV_LORA, HD)
    wuv2 = w_uv[l].reshape(MLA_KV_LORA, HD)
    wkv_p = jnp.concatenate([wuk2, wuv2], axis=1).astype(BF16)
    ukt = w_uk[l].transpose(1, 2, 0)
    eye = jnp.eye(MLA_HEADS, dtype=ukt.dtype)
    wuk_bd = (ukt[:, :, None, :] * eye[:, None, :, None]).reshape(HD, MLA_HEADS * MLA_KV_LORA)
    wuk_bd = wuk_bd.astype(BF16)
    g1 = norm1_g[l][None]
    gq = g_q[l][None]
    gkv = g_kv[l][None]
    g2 = norm2_g[l][None]
    gf = final_g[None]
    wa, wb, wo = w_br_a[l].astype(BF16), w_br_b[l].astype(BF16), w_o[l].astype(BF16)
    wg, wu, wd = w_ffn_g[l].astype(BF16), w_ffn_u[l].astype(BF16), w_ffn_d[l].astype(BF16)
    cw, cb = ffn_conv_w[l], ffn_conv_b[l][None]

    meta = jnp.broadcast_to(meta_tokens[None].astype(x_prompt.dtype), (nb, N_META, d))
    xp = jnp.concatenate([meta, x_prompt, jnp.zeros((nb, lp - l_valid, d), x_prompt.dtype)], axis=1)
    tm = _pick_tile(lp, (384, 256, 128))
    tabs_p = _rope_tables(np.arange(lp))
    (ckv_p, kb_p, vb_p, small_p, qn, qp, qb, qi, gate_p, kcat, vm, kbb, vbb, ki2) = _proj_call(
        xp, tabs_p, g1, gq, gkv, win, wuq, wkv_p, tm, sample=False, l_out=l_valid)
    mla_p, dsa_p = _attn_call(qn, qp, qb, qi, small_p, kcat, vm, kbb, vbb, ki2, l_valid, tq, tk)
    x2_p, h2_p = _merge_call(xp, mla_p, dsa_p, gate_p, wa, wb, wo, g2, tm, "merge_prompt")
    zstate = jnp.zeros((1, SUBLANES, d_ff), F32)
    tm_ffn = _pick_tile(lp, (768, 384, 256, 128))
    y_p, tail_p = _ffn_call(h2_p, x2_p, zstate, zstate, wg, wu, wd, cw, cb, gf, tm_ffn, l_valid,
                            False, "ffn_prompt")

    ns = nsb * t_new
    xs = x_sample.reshape(1, ns, d)
    tabs_s = _rope_tables(past_len + (np.arange(ns) % t_new))
    (ckv_s, kb_s, vb_s, small_s, qlat, qp_s, qb_s, qi_s, gate_s) = _proj_call(
        xs, tabs_s, g1, gq, gkv, win, wuq, wuk_bd, ns, sample=True)
    qabs = _head_major(qlat, nsb, t_new, MLA_HEADS)
    qpe_r = _head_major(qp_s, nsb, t_new, MLA_HEADS)
    qi_r = _head_major(qi_s, nsb, t_new, IDX_HEADS)
    w_rows = small_s[0, :, _L_WI:_L_WI + IDX_HEADS].reshape(nsb, t_new, IDX_HEADS)
    w_rows = jnp.broadcast_to(w_rows.transpose(0, 2, 1).reshape(nsb, IDX_HEADS * t_new, 1),
                              (nsb, IDX_HEADS * t_new, LANES))
    qb_r = _head_major(qb_s, nsb, t_new, DSA_HEADS)
    qbd = jnp.where(_np_head_mask(t_new), jnp.tile(qb_r, (1, 1, DSA_HEADS)), jnp.zeros((), BF16))
    new_rows = lambda a: _pad_rows(a.reshape(nsb, t_new, -1), PAGE_SIZE).astype(BF16)
    ckvn = new_rows(ckv_s)
    kpen = new_rows(small_s[..., _L_KPE:_L_KPE + MLA_ROPE])
    ikn = new_rows(small_s[..., _L_KI:_L_KI + IDX_DIM])
    kbn, vbn = new_rows(kb_s), new_rows(vb_s)
    n_pool = cache_dsa_k.shape[1]
    keys_minor = lambda pool: jnp.moveaxis(pool[l], 1, -1).reshape(n_pool, -1, PAGE_SIZE)
    mla_s, scp, scn = _smla_call(page_table, qabs, qpe_r, qi_r, w_rows, ckvn, kpen, ikn,
                                 wuv2.astype(BF16), cache_mla_ckv[l], keys_minor(cache_mla_kpe),
                                 keys_minor(cache_idx_k), t_new)
    dsa_s = _sdsa_call(page_table, qbd, scp, scn, kbn, vbn,
                       keys_minor(cache_dsa_k), keys_minor(cache_dsa_v), t_new)
    x2_s, h2_s = _merge_call(xs, mla_s.reshape(1, ns, HD), dsa_s.reshape(1, ns, HD), gate_s,
                             wa, wb, wo, g2, ns, "merge_sample")
    st = state_ffn_conv[l]
    zrow = jnp.zeros((nsb, 1, d_ff), st.dtype)
    s1 = jnp.concatenate([st[:, 1:2]] + [zrow] * (t_new - 1), axis=1).reshape(1, ns, d_ff)
    s2 = jnp.concatenate([st[:, 0:1], st[:, 1:2]] + [zrow] * (t_new - 2), axis=1).reshape(1, ns, d_ff)
    y_s, tail_s = _ffn_call(h2_s, x2_s, s1, s2, wg, wu, wd, cw, cb, gf, ns, t_new, True,
                            "ffn_sample")

    y_prompt = y_p[:, N_META:l_valid]
    y_sample = y_s.reshape(nsb, t_new, d)
    cut = lambda a: a[:, :l_valid]
    new_ckv_p = ckv_p[None]
    new_kpe_p = cut(small_p)[..., _L_KPE:_L_KPE + MLA_ROPE][None]
    pos_last = lambda a: jnp.moveaxis(
        a.reshape(nb, DSA_HEADS, DSA_HEAD_DIM, l_valid), -1, 1)[None]
    new_k_p, new_v_p = pos_last(kb_p), pos_last(vb_p)
    new_ik_p = cut(small_p)[..., _L_KI:_L_KI + IDX_DIM][None]
    new_conv_p = tail_p[:, SUBLANES - (CONV_W - 1):][None]
    per_s = lambda a: a.reshape(nsb, t_new, -1)
    new_ckv_s = per_s(ckv_s)[None]
    new_kpe_s = per_s(small_s)[..., _L_KPE:_L_KPE + MLA_ROPE][None]
    new_k_s = per_s(kb_s).reshape(1, nsb, t_new, DSA_HEADS, DSA_HEAD_DIM)
    new_v_s = per_s(vb_s).reshape(1, nsb, t_new, DSA_HEADS, DSA_HEAD_DIM)
    new_ik_s = per_s(small_s)[..., _L_KI:_L_KI + IDX_DIM][None]
    new_conv_s = per_s(tail_s)[:, t_new - (CONV_W - 1):][None]
    return (y_prompt, y_sample, new_ckv_p, new_kpe_p, new_k_p, new_v_p, new_ik_p, new_conv_p,
            new_ckv_s, new_kpe_s, new_k_s, new_v_s, new_ik_s, new_conv_s)


def _np_head_mask(t_new):
    r = np.arange(DSA_HEADS * t_new)[:, None] // t_new
    c = np.arange(HD)[None, :] // DSA_HEAD_DIM
    return jnp.asarray(r == c)[None]
```

```python
import functools

import numpy as np
import jax
import jax.numpy as jnp
from jax import lax
from jax.experimental import pallas as pl
from jax.experimental.pallas import tpu as pltpu

N_META = 16
MLA_HEADS = 8
MLA_NOPE = 64
MLA_ROPE = 32
MLA_V = 64
MLA_KV_LORA = 256
MLA_Q_LORA = 768
MLA_SCALE = (MLA_NOPE + MLA_ROPE) ** -0.5
DSA_HEADS = 8
DSA_HEAD_DIM = 64
DSA_SCALE = DSA_HEAD_DIM ** -0.5
IDX_HEADS = 8
IDX_DIM = 64
IDX_SCALE = IDX_DIM ** -0.5
LOG2E = 1.4426950408889634
MLA_QSCALE = MLA_SCALE * LOG2E
DSA_QSCALE = DSA_SCALE * LOG2E
TOPK_MAX = 256
CONV_W = 3
ROPE_THETA = 10000.0
EPS = 1e-6
NEG_INF = -1e30
PAGE_SIZE = 128

LANES = 128
SUBLANES = 8
HD = 512
VMEM_LIMIT = 56 * 1024 * 1024

F32 = jnp.float32
BF16 = jnp.bfloat16

_C_Q, _C_KV, _Q_B, _K_B, _V_B, _Q_I, _G_A, _G_B, _SMALL, _D_IN_P = (
    0, 768, 1024, 1536, 2048, 2560, 3072, 4096, 5120, 5248)
_L_KI, _L_KPE, _L_WI = 0, 64, 96


def _dot(a, b):
    return jnp.dot(a, b, preferred_element_type=F32)


def _dot_t(a, b):
    return lax.dot_general(a, b, (((1,), (1,)), ((), ())), preferred_element_type=F32)


def _rms(x, g):
    ms = jnp.mean(x * x, axis=-1, keepdims=True)
    return x * lax.rsqrt(ms + EPS) * g


def _rope_tables(pos):
    pos = jnp.asarray(pos).astype(F32)[:, None]
    lane = np.arange(LANES)
    out = []
    for width in (64, 32):
        half = width // 2
        m = lane % width
        inv_freq = 1.0 / (ROPE_THETA ** (jnp.arange(half, dtype=F32) / half))
        ang = pos * inv_freq[m % half][None, :]
        c, s = jnp.cos(ang), jnp.sin(ang)
        out += [c, jnp.where(m >= half, s, 0.0), jnp.where(m < half, -s, 0.0)]
    return out


def _rope_blk(x, c, sa, sb, half):
    return x * c + pltpu.roll(x, half, 1) * sa + pltpu.roll(x, LANES - half, 1) * sb


def _proj_kernel(sample, x_ref, c64_ref, sa64_ref, sb64_ref, c32_ref, sa32_ref, sb32_ref,
                 g1_ref, gq_ref, gkv_ref, win_ref, wuq_ref, wkv_ref, *outs):
    if sample:
        (ckv_o, kb_o, vb_o, small_o, qlat_o, qp_o, qb_o, qi_o, gate_o) = outs
    else:
        (ckv_o, kb_o, vb_o, small_o, qn_o, qp_o, qb_o, qi_o, gate_o,
         kcat_o, vm_o, kbb_o, vbb_o, ki2_o) = outs
    hb = _rms(x_ref[0], g1_ref[...]).astype(BF16)
    c64, sa64, sb64 = c64_ref[...], sa64_ref[...], sb64_ref[...]
    c32, sa32, sb32 = c32_ref[...], sa32_ref[...], sb32_ref[...]

    def proj(a, b):
        return _dot(hb, win_ref[:, a:b])

    cq = _rms(proj(_C_Q, _C_KV), gq_ref[...]).astype(BF16)
    ckv = _rms(proj(_C_KV, _Q_B), gkv_ref[...])
    ckv_o[0] = ckv

    zq = proj(_Q_B, _K_B)
    zk = proj(_K_B, _V_B)
    zi = proj(_Q_I, _G_A)
    for blk in range(HD // LANES):
        sl = slice(blk * LANES, (blk + 1) * LANES)
        qb_o[0, :, sl] = (_rope_blk(zq[:, sl], c64, sa64, sb64, 32) * DSA_QSCALE).astype(BF16)
        qi_o[0, :, sl] = (_rope_blk(zi[:, sl], c64, sa64, sb64, 32) * IDX_SCALE).astype(BF16)
        kr = _rope_blk(zk[:, sl], c64, sa64, sb64, 32)
        if sample:
            kb_o[0, :, sl] = kr
        else:
            kb_o[0, sl, :] = kr.T
            kbb_o[0, :, sl] = kr.astype(BF16)
    zv = proj(_V_B, _Q_I)
    if sample:
        vb_o[0] = zv
    else:
        for blk in range(HD // LANES):
            sl = slice(blk * LANES, (blk + 1) * LANES)
            zvt = zv[:, sl].T
            vb_o[0, sl, :] = zvt
            vbb_o[0, sl, :] = zvt.astype(BF16)

    gate_o[0] = jax.nn.sigmoid(proj(_G_A, _SMALL)).astype(BF16)

    zs = proj(_SMALL, _D_IN_P)
    lane = lax.broadcasted_iota(jnp.int32, (1, LANES), 1)
    m_ki = (lane < _L_KPE).astype(F32)
    m_kpe = ((lane >= _L_KPE) & (lane < _L_WI)).astype(F32)
    m_wi = ((lane >= _L_WI) & (lane < _L_WI + IDX_HEADS)).astype(F32)
    small = (zs * (c64 * m_ki + c32 * m_kpe + (IDX_HEADS ** -0.5) * m_wi)
             + pltpu.roll(zs, 32, 1) * (sa64 * m_ki) + pltpu.roll(zs, 96, 1) * (sb64 * m_ki)
             + pltpu.roll(zs, 16, 1) * (sa32 * m_kpe) + pltpu.roll(zs, 112, 1) * (sb32 * m_kpe))
    small_o[0] = small
    if not sample:
        ki = small * m_ki
        ki2_o[0] = (ki + pltpu.roll(ki, 64, 1)).astype(BF16)
        kp = pltpu.roll(small * m_kpe, 64, 1)
        kpe4 = (kp + pltpu.roll(kp, 32, 1) + pltpu.roll(kp, 64, 1)
                + pltpu.roll(kp, 96, 1)).astype(BF16)
        for pair in range(MLA_HEADS // 2):
            kcat_o[0, :, (2 * pair + 1) * LANES:(2 * pair + 2) * LANES] = kpe4

    q = _dot(cq, wuq_ref[...])
    qn = (q[:, :HD] * MLA_QSCALE).astype(BF16)
    for blk in range(2):
        qpe = q[:, HD + blk * LANES:HD + (blk + 1) * LANES]
        qp_o[0, :, blk * LANES:(blk + 1) * LANES] = (
            _rope_blk(qpe, c32, sa32, sb32, 16) * MLA_QSCALE).astype(BF16)

    if sample:
        qlat_o[0] = _dot(qn, wkv_ref[...]).astype(BF16)
    else:
        qn_o[0] = qn
        kv = _dot(ckv.astype(BF16), wkv_ref[...])
        for pair in range(MLA_HEADS // 2):
            kcat_o[0, :, 2 * pair * LANES:(2 * pair + 1) * LANES] = (
                kv[:, pair * LANES:(pair + 1) * LANES].astype(BF16))
        for blk in range(HD // LANES):
            vm_o[0, blk * LANES:(blk + 1) * LANES, :] = (
                kv[:, HD + blk * LANES:HD + (blk + 1) * LANES].T.astype(BF16))


def _const_spec(shape):
    nd = len(shape)
    return pl.BlockSpec(shape, lambda *_: (0,) * nd, pipeline_mode=pl.Buffered(1))


def _proj_call(x, tables, g1, gq, gkv, win, wuq, wkv, tm, sample, l_out=None):
    nb, lp, d = x.shape
    l_out = lp if l_out is None else l_out
    grid = (nb, lp // tm)
    row = lambda w: pl.BlockSpec((1, tm, w), lambda b, j: (b, j, 0))
    tab = pl.BlockSpec((tm, LANES), lambda b, j: (j, 0))
    in_specs = ([row(d)] + [tab] * 6
                + [_const_spec(a.shape) for a in (g1, gq, gkv, win, wuq, wkv)])
    f32_w = [MLA_KV_LORA, HD, HD, LANES]
    if sample:
        bf_w = [MLA_HEADS * MLA_KV_LORA, 2 * LANES, HD, HD, 2048]
    else:
        bf_w = [HD, 2 * LANES, HD, HD, 2048, 2 * HD, HD, HD, HD, LANES]
    out_shape = ([jax.ShapeDtypeStruct((nb, lp, w), F32) for w in f32_w]
                 + [jax.ShapeDtypeStruct((nb, lp, w), BF16) for w in bf_w])
    out_specs = [row(w) for w in f32_w + bf_w]
    out_shape[0] = jax.ShapeDtypeStruct((nb, l_out, MLA_KV_LORA), F32)
    if not sample:
        n_f32 = len(f32_w)
        for i, dt, rows in ((1, F32, l_out), (2, F32, l_out), (n_f32 + 6, BF16, lp),
                            (n_f32 + 8, BF16, lp)):
            out_shape[i] = jax.ShapeDtypeStruct((nb, HD, rows), dt)
            out_specs[i] = pl.BlockSpec((1, HD, tm), lambda b, j: (b, 0, j))
    return pl.pallas_call(
        functools.partial(_proj_kernel, sample),
        grid=grid, in_specs=in_specs, out_specs=out_specs, out_shape=out_shape,
        compiler_params=pltpu.CompilerParams(
            dimension_semantics=("arbitrary", "arbitrary"), vmem_limit_bytes=VMEM_LIMIT),
        name="proj_sample" if sample else "proj_prompt",
    )(x, *tables, g1, gq, gkv, win, wuq, wkv)


_MAX_BISECT = 320
_BISECT_UNROLL = 4


def _chunk_ds(c, tk):
    start = c * tk
    return pl.ds(start if isinstance(start, int) else pl.multiple_of(start, tk), tk)


def _key_chunk(ref, c, tk, axis):
    ds = _chunk_ds(c, tk)
    return ref[ds, :] if axis == 0 else ref[:, ds]


def _key_fold(v, op, axis):
    if axis == 0:
        parts = [v[i * SUBLANES:(i + 1) * SUBLANES] for i in range(v.shape[0] // SUBLANES)]
    else:
        parts = [v[:, i * LANES:(i + 1) * LANES] for i in range(v.shape[1] // LANES)]
    while len(parts) > 1:
        parts = [op(a, b) for a, b in zip(parts[::2], parts[1::2])] + (
            [parts[-1]] if len(parts) % 2 else [])
    return parts[0]


def _key_index(c, tk, axis):
    shape = (tk, 1) if axis == 0 else (1, tk)
    return (lax.convert_element_type(c * tk, F32)
            + lax.broadcasted_iota(jnp.int32, shape, axis).astype(F32))


def _count(sc_ref, nck, tk, axis, pred):
    nq = sc_ref.shape[1 - axis]
    part = (SUBLANES, nq) if axis == 0 else (nq, LANES)

    def body(c, acc):
        v = jnp.where(pred(_key_chunk(sc_ref, c, tk, axis), c), 1.0, 0.0)
        return acc + _key_fold(v, jnp.add, axis)

    acc = lax.fori_loop(0, nck, body, jnp.zeros(part, F32))
    return jnp.sum(acc, axis=axis, keepdims=True)


def _stats_init(part):
    return (jnp.full(part, -NEG_INF, F32), jnp.full(part, NEG_INF, F32),
            jnp.zeros(part, F32), jnp.zeros(part, F32))


def _stats_update(stats, x, axis, has_masked):
    mn, mx, gt0, ge0 = stats
    xv = jnp.where(x > 0.5 * NEG_INF, x, -NEG_INF) if has_masked else x
    return (jnp.minimum(mn, _key_fold(xv, jnp.minimum, axis)),
            jnp.maximum(mx, _key_fold(x, jnp.maximum, axis)),
            gt0 + _key_fold(jnp.where(x > 0.0, 1.0, 0.0), jnp.add, axis),
            ge0 + _key_fold(jnp.where(x >= 0.0, 1.0, 0.0), jnp.add, axis))


def _select_threshold(sc_ref, nck, tk, k_eff, n_valid, axis, stats=None):
    nq = sc_ref.shape[1 - axis]
    part = (SUBLANES, nq) if axis == 0 else (nq, LANES)

    if stats is None:
        stats = lax.fori_loop(
            0, nck, lambda c, st: _stats_update(st, _key_chunk(sc_ref, c, tk, axis), axis, True),
            _stats_init(part))
    mn, mx, gt0, ge0 = stats
    lo0 = jnp.min(mn, axis=axis, keepdims=True)
    hi0 = jnp.max(mx, axis=axis, keepdims=True)
    hi0 = hi0 + jnp.maximum(jnp.abs(hi0), 1.0) * 1e-6

    def count_ge(thr):
        return _count(sc_ref, nck, tk, axis, lambda x, c: x >= thr)

    def all_rows(fin):
        return jnp.min(jnp.where(fin, 1.0, 0.0)) > 0.5

    c_gt0 = jnp.sum(gt0, axis=axis, keepdims=True)
    c_ge0 = jnp.sum(ge0, axis=axis, keepdims=True)
    at_zero = (c_gt0 < k_eff) & (c_ge0 >= k_eff)
    above = c_gt0 >= k_eff
    inside = (lo0 < 0.0) & (hi0 > 0.0)
    lo_at_zero = at_zero | (above & inside)
    c_lo0 = jnp.where(lo_at_zero, c_ge0, n_valid)
    lo0, hi0 = (jnp.where(lo_at_zero, 0.0, lo0),
                jnp.where(at_zero | (~above & inside), 0.0, hi0))

    def cond(st):
        it, _, _, _, done = st
        return jnp.logical_and(it < _MAX_BISECT, jnp.logical_not(done))

    def body(st):
        it, lo, hi, c_lo, _ = st
        for _ in range(_BISECT_UNROLL):
            mid = 0.5 * lo + 0.5 * hi
            c = count_ge(mid)
            ge = c >= k_eff
            fin = (mid <= lo) | (mid >= hi)
            lo, hi, c_lo = jnp.where(ge, mid, lo), jnp.where(ge, hi, mid), jnp.where(ge, c, c_lo)
            fin = fin | (c_lo == k_eff)
        return it + _BISECT_UNROLL, lo, hi, c_lo, all_rows(fin)

    fin0 = (c_lo0 == k_eff) | (lo0 >= hi0)
    _, t, _, c_t, _ = lax.while_loop(
        cond, body, (jnp.int32(0), lo0, hi0, c_lo0, all_rows(fin0)))

    tie_rows = c_t > k_eff

    @pl.when(jnp.max(jnp.where(tie_rows, 1.0, 0.0)) > 0.5)
    def _():
        n_keys = sc_ref.shape[axis]
        need = k_eff - _count(sc_ref, nck, tk, axis, lambda x, c: x > t)

        def tie_count(m):
            return _count(sc_ref, nck, tk, axis,
                          lambda x, c: (x == t) & (_key_index(c, tk, axis) <= m))

        def ibody(_, st):
            lo_i, hi_i = st
            mid = jnp.floor(0.5 * (lo_i + hi_i))
            ok = tie_count(mid) >= need
            return jnp.where(ok, lo_i, mid), jnp.where(ok, mid, hi_i)

        steps = int(np.ceil(np.log2(n_keys))) + 1
        _, m_idx = lax.fori_loop(0, steps, ibody,
                                 (jnp.full(t.shape, -1.0, F32),
                                  jnp.full(t.shape, float(n_keys - 1), F32)))

        def fix(c, carry):
            x = _key_chunk(sc_ref, c, tk, axis)
            drop = (x == t) & (_key_index(c, tk, axis) > m_idx) & tie_rows
            ds = _chunk_ds(c, tk)
            if axis == 0:
                sc_ref[ds, :] = jnp.where(drop, NEG_INF, x)
            else:
                sc_ref[:, ds] = jnp.where(drop, NEG_INF, x)
            return carry

        lax.fori_loop(0, nck, fix, 0)

    return t


def _lane_blocks(s):
    return [s[:, i * LANES:(i + 1) * LANES] for i in range(s.shape[1] // LANES)]


def _scale_cols(alpha, x):
    return jnp.concatenate([alpha * b for b in _lane_blocks(x)], axis=1)


def _online_update(m_ref, l_ref, acc_ref, idx, blocks, pv):
    m = m_ref[idx]
    bm = blocks[0]
    for b in blocks[1:]:
        bm = jnp.maximum(bm, b)
    m_new = jnp.maximum(m, jnp.max(bm, axis=1, keepdims=True))
    alpha = jnp.exp2(m - m_new)
    ps = [jnp.exp2(b - m_new) for b in blocks]
    rs = ps[0]
    for p in ps[1:]:
        rs = rs + p
    l_ref[idx] = alpha * l_ref[idx] + jnp.sum(rs, axis=1, keepdims=True)
    acc_ref[idx] = _scale_cols(alpha, acc_ref[idx]) + pv([p.astype(BF16) for p in ps])
    m_ref[idx] = m_new


def _reset_state(m_ref, l_ref, acc_ref):
    m_ref[...] = jnp.full_like(m_ref, NEG_INF)
    l_ref[...] = jnp.zeros_like(l_ref)
    acc_ref[...] = jnp.zeros_like(acc_ref)


def _softmax_keys_major(m_ref, l_ref, idx, s):
    m = m_ref[idx]
    m_new = jnp.maximum(m, jnp.max(_key_fold(s, jnp.maximum, 0), axis=0, keepdims=True))
    alpha = jnp.exp2(m - m_new)
    p = jnp.exp2(s - m_new)
    l_ref[idx] = alpha * l_ref[idx] + jnp.sum(_key_fold(p, jnp.add, 0), axis=0, keepdims=True)
    m_ref[idx] = m_new
    return alpha, p.astype(BF16)


def _attn_kernel(tq, tk, j0, n_keep, topk, n_alias,
                 qn_ref, qp_ref, qb_ref, qi_ref, small_ref,
                 kcat_ref, vm_ref, kb_ref, vb_ref, ki2_ref, *rest):
    (mla_ref, dsa_ref, sc_ref, qc_ref, qd_ref, qx_ref, w_ref, st_ref, m_ref, l_ref,
     acc_ref) = rest[n_alias:]
    j = pl.program_id(1) + j0
    nh = MLA_HEADS
    n_pairs = nh // 2

    @pl.when(j >= n_keep)
    def _():
        mla_ref[...] = jnp.zeros_like(mla_ref)
        dsa_ref[...] = jnp.zeros_like(dsa_ref)

    @pl.when(j < n_keep)
    def _():
        nck = (j * tq + tq - 1) // tk + 1
        last = nck - 1
        lane = lax.broadcasted_iota(jnp.int32, (1, LANES), 1)
        q_row = j * tq + lax.broadcasted_iota(jnp.int32, (1, tq), 1)
        k_row = lax.broadcasted_iota(jnp.int32, (tk, 1), 0)
        zero_b = jnp.zeros((tq, LANES), BF16)

        def ksl(c):
            return _chunk_ds(c, tk)

        def pair_sl(pair):
            return slice(pair * LANES, (pair + 1) * LANES)

        def two(x):
            return jnp.concatenate([x, x], axis=1)

        def write_heads(out_ref):
            half = LANES // 2
            for pair in range(n_pairs):
                o = acc_ref[pair] / l_ref[pair]
                o = jnp.concatenate([o[:half, :tq], o[half:, tq:]], axis=0)
                out_ref[0, :, pair_sl(pair)] = o.T.astype(BF16)

        small = small_ref[0]
        for h in range(nh):
            pair, sub = divmod(h, 2)
            grp, gsub = divmod(h, 4)
            rows = slice(sub * tq, (sub + 1) * tq)
            own = (lane >= 64) == bool(sub)
            qc_ref[pair, rows, :LANES] = jnp.where(own, qn_ref[0, :, pair_sl(pair)], zero_b)
            qc_ref[pair, rows, LANES:] = jnp.where((lane // 32) == gsub,
                                                   qp_ref[0, :, pair_sl(grp)], zero_b)
            qd_ref[pair, rows, :] = jnp.where(own, qb_ref[0, :, pair_sl(pair)], zero_b)
            qx_ref[h * tq:(h + 1) * tq, :] = jnp.where(own, qi_ref[0, :, pair_sl(pair)], zero_b)
        w_ref[...] = small.T[_L_WI:_L_WI + IDX_HEADS]

        def attend(c, score_fn, fix_fn, vt_ref):
            ks = ksl(c)
            s = {p: score_fn(p, ks) for p in range(min(2, n_pairs))}
            for p in range(n_pairs):
                alpha, pr = _softmax_keys_major(m_ref, l_ref, p, fix_fn(s.pop(p)))
                if p + 2 < n_pairs:
                    s[p + 2] = score_fn(p + 2, ks)
                acc_ref[p] = alpha * acc_ref[p] + _dot(vt_ref[0, pair_sl(p), ks], pr)

        _reset_state(m_ref, l_ref, acc_ref)
        for i, part in enumerate(_stats_init((SUBLANES, tq))):
            st_ref[i] = part

        def mla_scores(p, ks):
            return _dot_t(kcat_ref[0, ks, 2 * p * LANES:(2 * p + 2) * LANES], qc_ref[p])

        def mla_idx_chunk(c, masked):
            ks = ksl(c)
            r = _dot_t(ki2_ref[0, ks, :], qx_ref[...])
            sc = None
            for h in range(IDX_HEADS):
                term = jnp.maximum(r[:, h * tq:(h + 1) * tq], 0.0) * w_ref[h:h + 1, :]
                sc = term if sc is None else sc + term
            if masked:
                sc = jnp.where((c * tk + k_row) <= q_row, sc, NEG_INF)
                vis2 = (c * tk + k_row) <= two(q_row)
                fix = lambda s: jnp.where(vis2, s, NEG_INF)
            else:
                fix = lambda s: s
            sc_ref[ks, :] = sc
            for i, part in enumerate(_stats_update(tuple(st_ref[i] for i in range(4)), sc, 0,
                                                   masked)):
                st_ref[i] = part
            attend(c, mla_scores, fix, vm_ref)

        def mla_idx_step(i, carry):
            mla_idx_chunk(2 * i, False)
            mla_idx_chunk(2 * i + 1, False)
            return carry

        lax.fori_loop(0, last // 2, mla_idx_step, 0)

        @pl.when(last % 2 == 1)
        def _():
            mla_idx_chunk(last - 1, False)

        mla_idx_chunk(last, True)
        write_heads(mla_ref)

        n_valid = (q_row + 1).astype(F32)
        k_eff = jnp.minimum(n_valid, float(topk))
        t2 = two(_select_threshold(sc_ref, nck, tk, k_eff, n_valid, 0,
                                   tuple(st_ref[i] for i in range(4))))

        _reset_state(m_ref, l_ref, acc_ref)

        def dsa_scores(p, ks):
            return _dot_t(kb_ref[0, ks, pair_sl(p)], qd_ref[p])

        def dsa_chunk(c):
            sel = two(sc_ref[ksl(c), :]) >= t2
            attend(c, dsa_scores, lambda s: jnp.where(sel, s, NEG_INF), vb_ref)

        def dsa_step(i, carry):
            dsa_chunk(2 * i)
            dsa_chunk(2 * i + 1)
            return carry

        lax.fori_loop(0, nck // 2, dsa_step, 0)

        @pl.when(nck % 2 == 1)
        def _():
            dsa_chunk(nck - 1)

        write_heads(dsa_ref)


_TAIL_TQ = 64


def _attn_call(qn, qp, qb, qi, small, kcat, vm, kbb, vbb, ki2, l_valid, tq, tk,
               j0=0, n_tiles=None, n_keep=None, into=()):
    nb, lp, _ = qn.shape
    n_tiles = lp // tq if n_tiles is None else n_tiles
    n_keep = -(-l_valid // tq) if n_keep is None else n_keep
    topk = min(TOPK_MAX, l_valid // 4)
    n_pairs = MLA_HEADS // 2
    qrow = lambda w: pl.BlockSpec((1, tq, w), lambda b, j: (b, j + j0, 0))
    krow = lambda w: pl.BlockSpec((1, lp, w), lambda b, j: (b, 0, 0))
    vcol = pl.BlockSpec((1, HD, lp), lambda b, j: (b, 0, 0))
    in_specs = [qrow(HD), qrow(2 * LANES), qrow(HD), qrow(HD), qrow(LANES),
                krow(2 * HD), vcol, krow(HD), vcol, krow(LANES)] + [qrow(HD)] * len(into)
    n_in = 10
    return pl.pallas_call(
        functools.partial(_attn_kernel, tq, tk, j0, n_keep, topk, len(into)),
        grid=(nb, n_tiles), in_specs=in_specs,
        out_specs=[qrow(HD), qrow(HD)],
        out_shape=[jax.ShapeDtypeStruct((nb, lp, HD), BF16)] * 2,
        input_output_aliases={n_in + i: i for i in range(len(into))},
        scratch_shapes=[pltpu.VMEM((lp, tq), F32),
                        pltpu.VMEM((n_pairs, 2 * tq, 2 * LANES), BF16),
                        pltpu.VMEM((n_pairs, 2 * tq, LANES), BF16),
                        pltpu.VMEM((IDX_HEADS * tq, LANES), BF16),
                        pltpu.VMEM((IDX_HEADS, tq), F32),
                        pltpu.VMEM((4, SUBLANES, tq), F32),
                        pltpu.VMEM((n_pairs, 1, 2 * tq), F32),
                        pltpu.VMEM((n_pairs, 1, 2 * tq), F32),
                        pltpu.VMEM((n_pairs, LANES, 2 * tq), F32)],
        compiler_params=pltpu.CompilerParams(
            dimension_semantics=("arbitrary", "arbitrary"), vmem_limit_bytes=VMEM_LIMIT),
        name="attn_prompt" if not into else "attn_prompt_tail",
    )(qn, qp, qb, qi, small, kcat, vm, kbb, vbb, ki2, *into)


def _merge_kernel(x_ref, mla_ref, dsa_ref, gate_ref, wa_ref, wb_ref, wo_ref, g2_ref,
                  x2_ref, h2_ref):
    d = x_ref.shape[-1]
    a = _dot(mla_ref[0], wa_ref[...])
    b = _dot(dsa_ref[0], wb_ref[...])
    g = gate_ref[0]
    o = g[:, :d].astype(F32) * a + g[:, d:].astype(F32) * b
    x2 = x_ref[0] + _dot(o.astype(BF16), wo_ref[...])
    x2_ref[0] = x2
    h2_ref[0] = _rms(x2, g2_ref[...]).astype(BF16)


def _merge_call(x, mla, dsa, gate, wa, wb, wo, g2, tm, name):
    nb, lp, d = x.shape
    row = lambda w: pl.BlockSpec((1, tm, w), lambda b, j: (b, j, 0))
    return pl.pallas_call(
        _merge_kernel, grid=(nb, lp // tm),
        in_specs=[row(d), row(HD), row(HD), row(2 * d)]
        + [_const_spec(a.shape) for a in (wa, wb, wo, g2)],
        out_specs=[row(d), row(d)],
        out_shape=[jax.ShapeDtypeStruct((nb, lp, d), F32), jax.ShapeDtypeStruct((nb, lp, d), BF16)],
        compiler_params=pltpu.CompilerParams(
            dimension_semantics=("arbitrary", "arbitrary"), vmem_limit_bytes=VMEM_LIMIT),
        name=name,
    )(x, mla, dsa, gate, wa, wb, wo, g2)


_FF_CHUNK = 1024
_FF_AHEAD = 1


def _ffn_kernel(seq_rows, tail_tile, tail_off, h_ref, x_ref, s1_ref, s2_ref,
                wg_ref, wu_ref, wd_ref, cw_ref, cb_ref, gf_ref, y_ref, tail_ref, prev_ref):
    j = pl.program_id(1)
    tm = h_ref.shape[1]
    d_ff = wg_ref.shape[1]
    h = h_ref[0]
    row = lax.broadcasted_iota(jnp.int32, (tm, 1), 0)
    if seq_rows is None:
        @pl.when(j == 0)
        def _():
            prev_ref[...] = jnp.zeros_like(prev_ref)
        first1, first2 = row < 1, row < 2
    else:
        first1, first2 = (row % seq_rows) < 1, (row % seq_rows) < 2

    acc = jnp.zeros((tm, x_ref.shape[-1]), F32)
    chunks = [slice(c0, min(c0 + _FF_CHUNK, d_ff)) for c0 in range(0, d_ff, _FF_CHUNK)]

    def up(sl):
        return _dot(h, wg_ref[:, sl]), _dot(h, wu_ref[:, sl])

    ahead = [up(sl) for sl in chunks[:_FF_AHEAD]]
    for i, sl in enumerate(chunks):
        g, u = ahead.pop(0)
        if i + _FF_AHEAD < len(chunks):
            ahead.append(up(chunks[i + _FF_AHEAD]))
        if seq_rows is None:
            p = prev_ref[:, sl]
            hist1 = jnp.broadcast_to(p[SUBLANES - 1:SUBLANES], g.shape)
            hist2 = jnp.where(row < 1, jnp.broadcast_to(p[SUBLANES - 2:SUBLANES - 1], g.shape),
                              hist1)
            prev_ref[:, sl] = g[tm - SUBLANES:]
        else:
            hist1, hist2 = s1_ref[0, :, sl], s2_ref[0, :, sl]
        g1 = jnp.where(first1, hist1, pltpu.roll(g, 1, 0))
        g2 = jnp.where(first2, hist2, pltpu.roll(g, 2, 0))
        cw = cw_ref[:, sl]
        gc = cb_ref[:, sl] + cw[0:1] * g2 + cw[1:2] * g1 + cw[2:3] * g
        act = (gc * jax.nn.sigmoid(gc) * u).astype(BF16)
        acc = acc + _dot(act, wd_ref[sl, :])
        if seq_rows is None:
            @pl.when(j == tail_tile)
            def _(g=g, sl=sl):
                tail_ref[0, :, sl] = g[tail_off:tail_off + SUBLANES]
        else:
            tail_ref[0, :, sl] = g
    y_ref[0] = _rms(x_ref[0] + acc, gf_ref[...])


def _ffn_call(h2, x2, s1, s2, wg, wu, wd, cw, cb, gf, tm, l_valid, sample, name):
    nb, lp, d = x2.shape
    d_ff = wg.shape[1]
    row = lambda w: pl.BlockSpec((1, tm, w), lambda b, j: (b, j, 0))
    if sample:
        seq_rows, tail_tile, tail_off = l_valid, 0, 0
        tail_shape, tail_spec = (nb, lp, d_ff), row(d_ff)
        s_spec = row(d_ff)
    else:
        seq_rows = None
        tail_tile, tail_off = divmod(l_valid - SUBLANES, tm)
        tail_shape = (nb, SUBLANES, d_ff)
        tail_spec = pl.BlockSpec((1, SUBLANES, d_ff), lambda b, j: (b, 0, 0))
        s_spec = pl.BlockSpec((1, SUBLANES, d_ff), lambda b, j: (0, 0, 0))
    return pl.pallas_call(
        functools.partial(_ffn_kernel, seq_rows, tail_tile, tail_off),
        grid=(nb, lp // tm),
        in_specs=[row(d), row(d), s_spec, s_spec]
        + [_const_spec(a.shape) for a in (wg, wu, wd, cw, cb, gf)],
        out_specs=[row(d), tail_spec],
        out_shape=[jax.ShapeDtypeStruct((nb, lp, d), F32), jax.ShapeDtypeStruct(tail_shape, F32)],
        scratch_shapes=[pltpu.VMEM((SUBLANES, d_ff), F32)],
        compiler_params=pltpu.CompilerParams(
            dimension_semantics=("arbitrary", "arbitrary"), vmem_limit_bytes=VMEM_LIMIT),
        name=name,
    )(h2, x2, s1, s2, wg, wu, wd, cw, cb, gf)


def _head_rows_mask(rows_per_head, n_heads, width_per_head):
    r = lax.broadcasted_iota(jnp.int32, (n_heads * rows_per_head, 1), 0) // rows_per_head
    c = lax.broadcasted_iota(jnp.int32, (1, n_heads * width_per_head), 1) // width_per_head
    return r == c


def _diag_heads(full, t, n_heads, width):
    col_head = lax.broadcasted_iota(jnp.int32, (1, n_heads * width), 1) // width
    out = jnp.zeros((t, n_heads * width), F32)
    for h in range(n_heads):
        out = jnp.where(col_head == h, full[h * t:(h + 1) * t], out)
    return out


def _smla_kernel(npp, t_new, pt_ref, qabs_ref, qpe_ref, qi_ref, wrow_ref,
                 ckvn_ref, kpen_ref, ikn_ref, wuv_ref, *rest):
    ckv_pages = rest[:npp]
    kpe_pages = rest[npp:2 * npp]
    ik_pages = rest[2 * npp:3 * npp]
    mla_ref, scp_ref, scn_ref, m_ref, l_ref, acc_ref, ck_ref, kpe_ref, ik_ref = rest[3 * npp:]
    c = pl.program_id(1)
    rows = qabs_ref.shape[1]
    qabs, qpe, qi, wrow = qabs_ref[0], qpe_ref[0], qi_ref[0], wrow_ref[0]

    @pl.when(c == 0)
    def _():
        _reset_state(m_ref, l_ref, acc_ref)

    def update(s, vals):
        _online_update(m_ref, l_ref, acc_ref, 0, _lane_blocks(s),
                       lambda ps: _dot(jnp.concatenate(ps, axis=1), vals))

    def idx_score(s):
        s = _scale_cols(wrow, jnp.maximum(s, 0.0))
        out = s[:t_new]
        for h in range(1, IDX_HEADS):
            out = out + s[h * t_new:(h + 1) * t_new]
        return out

    for i in range(npp):
        ck_ref[i * PAGE_SIZE:(i + 1) * PAGE_SIZE, :] = ckv_pages[i][...].astype(BF16)
        kpe_ref[:, i * PAGE_SIZE:(i + 1) * PAGE_SIZE] = kpe_pages[i][...].astype(BF16)
        ik_ref[:, i * PAGE_SIZE:(i + 1) * PAGE_SIZE] = ik_pages[i][...].astype(BF16)
    ck_all = ck_ref[...]
    update(_dot_t(qabs, ck_all) + _dot(qpe, kpe_ref[...]), ck_all)
    scp_ref[0] = idx_score(_dot(qi, ik_ref[...]))

    @pl.when(c == pl.num_programs(1) - 1)
    def _():
        ck = ckvn_ref[0]
        tok = lax.broadcasted_iota(jnp.int32, (rows, 1), 0) % t_new
        key = lax.broadcasted_iota(jnp.int32, (1, PAGE_SIZE), 1)
        vis = key <= tok
        s = _dot_t(qabs, ck) + _dot_t(qpe, kpen_ref[0])
        update(jnp.where(vis, s, NEG_INF), ck)
        scn_ref[0] = jnp.where(vis[:t_new], idx_score(_dot_t(qi, ikn_ref[0])), NEG_INF)
        o_lat = _scale_cols(1.0 / l_ref[0], acc_ref[0]).astype(BF16)
        mla_ref[0] = _diag_heads(_dot(o_lat, wuv_ref[...]), t_new, MLA_HEADS, MLA_V).astype(BF16)


def _sdsa_kernel(npp, t_new, topk, past_len, sel_tk, pt_ref, qbd_ref, scp_ref, scn_ref, kbn_ref,
                 vbn_ref, *rest):
    k_pages = rest[:npp]
    v_pages = rest[npp:2 * npp]
    dsa_ref, sc_ref, t_ref, m_ref, l_ref, acc_ref, kt_ref, vt_ref = rest[2 * npp:]
    c = pl.program_id(1)
    rows = qbd_ref.shape[1]
    qbd = qbd_ref[0]
    step_keys = npp * PAGE_SIZE

    @pl.when(c == 0)
    def _():
        _reset_state(m_ref, l_ref, acc_ref)
        sc_ref[:, :past_len] = scp_ref[0]
        sc_ref[:, past_len:past_len + PAGE_SIZE] = scn_ref[0]
        if sel_tk > PAGE_SIZE:
            sc_ref[:, past_len + PAGE_SIZE:] = jnp.full((t_new, sel_tk - PAGE_SIZE), NEG_INF, F32)
        tok = lax.broadcasted_iota(jnp.int32, (t_new, 1), 0)
        n_valid = (past_len + 1 + tok).astype(F32)
        k_eff = jnp.minimum(n_valid, float(topk))
        t_ref[...] = _select_threshold(sc_ref, (past_len + sel_tk) // sel_tk, sel_tk,
                                       k_eff, n_valid, 1)

    t = t_ref[...]

    def masked(s, sc):
        pen = jnp.where(sc >= t, 0.0, NEG_INF)
        sel = jnp.concatenate([pen] * DSA_HEADS, axis=0) == 0.0
        return jnp.where(sel, s, NEG_INF)

    def update(s, pv_dot, vals):
        _online_update(m_ref, l_ref, acc_ref, 0, _lane_blocks(s),
                       lambda ps: pv_dot(jnp.concatenate(ps, axis=1), vals))

    for i in range(npp):
        kt_ref[:, i * PAGE_SIZE:(i + 1) * PAGE_SIZE] = k_pages[i][...].astype(BF16)
        vt_ref[:, i * PAGE_SIZE:(i + 1) * PAGE_SIZE] = v_pages[i][...].astype(BF16)
    start = c * step_keys
    start = start if isinstance(start, int) else pl.multiple_of(start, step_keys)
    update(masked(_dot(qbd, kt_ref[...]), sc_ref[:, pl.ds(start, step_keys)]), _dot_t, vt_ref[...])

    @pl.when(c == pl.num_programs(1) - 1)
    def _():
        s = masked(_dot_t(qbd, kbn_ref[0]), sc_ref[:, past_len:past_len + PAGE_SIZE])
        update(s, _dot, vbn_ref[0])
        o = _scale_cols(1.0 / l_ref[0], acc_ref[0])
        dsa_ref[0] = _diag_heads(o, t_new, DSA_HEADS, DSA_HEAD_DIM).astype(BF16)


def _pages_per_step(n_pages, want):
    p = min(want, n_pages)
    while n_pages % p:
        p -= 1
    return p


def _page_specs(npp, rows, width):
    return [pl.BlockSpec((None, rows, width),
                         functools.partial(lambda i, b, c, pt: (pt[b, c * npp + i], 0, 0), i))
            for i in range(npp)]


def _smla_call(page_table, qabs, qpe, qi, wrow, ckvn, kpen, ikn, wuv, pool_ckv, pool_kpe, pool_ik,
               t_new):
    nb, n_pages = page_table.shape
    npp = _pages_per_step(n_pages, 64)
    rows = qabs.shape[1]
    past_len = n_pages * PAGE_SIZE
    per_b = lambda shape: pl.BlockSpec((1,) + shape, lambda b, c, pt: (b,) + (0,) * len(shape))
    in_specs = ([per_b((rows, MLA_KV_LORA)), per_b((rows, MLA_ROPE)), per_b((rows, IDX_DIM)),
                 per_b((rows, LANES)), per_b((PAGE_SIZE, MLA_KV_LORA)),
                 per_b((PAGE_SIZE, MLA_ROPE)), per_b((PAGE_SIZE, IDX_DIM)),
                 pl.BlockSpec(wuv.shape, lambda b, c, pt: (0, 0))]
                + _page_specs(npp, PAGE_SIZE, MLA_KV_LORA) + _page_specs(npp, MLA_ROPE, PAGE_SIZE)
                + _page_specs(npp, IDX_DIM, PAGE_SIZE))
    out_specs = [per_b((t_new, HD)),
                 pl.BlockSpec((1, t_new, npp * PAGE_SIZE), lambda b, c, pt: (b, 0, c)),
                 per_b((t_new, PAGE_SIZE))]
    out_shape = [jax.ShapeDtypeStruct((nb, t_new, HD), BF16),
                 jax.ShapeDtypeStruct((nb, t_new, past_len), F32),
                 jax.ShapeDtypeStruct((nb, t_new, PAGE_SIZE), F32)]
    grid_spec = pltpu.PrefetchScalarGridSpec(
        num_scalar_prefetch=1, grid=(nb, n_pages // npp), in_specs=in_specs, out_specs=out_specs,
        scratch_shapes=[pltpu.VMEM((1, rows, LANES), F32), pltpu.VMEM((1, rows, LANES), F32),
                        pltpu.VMEM((1, rows, MLA_KV_LORA), F32),
                        pltpu.VMEM((npp * PAGE_SIZE, MLA_KV_LORA), BF16),
                        pltpu.VMEM((MLA_ROPE, npp * PAGE_SIZE), BF16),
                        pltpu.VMEM((IDX_DIM, npp * PAGE_SIZE), BF16)])
    return pl.pallas_call(
        functools.partial(_smla_kernel, npp, t_new), grid_spec=grid_spec, out_shape=out_shape,
        compiler_params=pltpu.CompilerParams(
            dimension_semantics=("arbitrary", "arbitrary"), vmem_limit_bytes=VMEM_LIMIT),
        name="sample_mla",
    )(page_table, qabs, qpe, qi, wrow, ckvn, kpen, ikn, wuv,
      *([pool_ckv] * npp), *([pool_kpe] * npp), *([pool_ik] * npp))


def _sdsa_call(page_table, qbd, scp, scn, kbn, vbn, pool_k, pool_v, t_new):
    nb, n_pages = page_table.shape
    npp = _pages_per_step(n_pages, 32)
    rows = qbd.shape[1]
    past_len = n_pages * PAGE_SIZE
    topk = min(TOPK_MAX, (past_len + t_new) // 4)
    sel_tk = PAGE_SIZE
    while sel_tk < 2048 and past_len % (2 * sel_tk) == 0:
        sel_tk *= 2
    per_b = lambda shape: pl.BlockSpec((1,) + shape, lambda b, c, pt: (b,) + (0,) * len(shape))
    in_specs = ([per_b((rows, HD)), per_b((t_new, past_len)), per_b((t_new, PAGE_SIZE)),
                 per_b((PAGE_SIZE, HD)), per_b((PAGE_SIZE, HD))]
                + _page_specs(npp, HD, PAGE_SIZE) + _page_specs(npp, HD, PAGE_SIZE))
    grid_spec = pltpu.PrefetchScalarGridSpec(
        num_scalar_prefetch=1, grid=(nb, n_pages // npp), in_specs=in_specs,
        out_specs=[per_b((t_new, HD))],
        scratch_shapes=[pltpu.VMEM((t_new, past_len + sel_tk), F32),
                        pltpu.VMEM((t_new, 1), F32),
                        pltpu.VMEM((1, rows, LANES), F32), pltpu.VMEM((1, rows, LANES), F32),
                        pltpu.VMEM((1, rows, HD), F32),
                        pltpu.VMEM((HD, npp * PAGE_SIZE), BF16),
                        pltpu.VMEM((HD, npp * PAGE_SIZE), BF16)])
    return pl.pallas_call(
        functools.partial(_sdsa_kernel, npp, t_new, topk, past_len, sel_tk), grid_spec=grid_spec,
        out_shape=[jax.ShapeDtypeStruct((nb, t_new, HD), BF16)],
        compiler_params=pltpu.CompilerParams(
            dimension_semantics=("arbitrary", "arbitrary"), vmem_limit_bytes=VMEM_LIMIT),
        name="sample_dsa",
    )(page_table, qbd, scp, scn, kbn, vbn, *([pool_k] * npp), *([pool_v] * npp))[0]


def _pick_tile(n, candidates):
    for c in candidates:
        if n % c == 0:
            return c
    return n


def _pad_rows(a, rows):
    return jnp.pad(a, ((0, 0), (0, rows - a.shape[1]), (0, 0)))


def _head_major(a, nb, t, heads):
    w = a.shape[-1] // heads
    return a.reshape(nb, t, heads, w).transpose(0, 2, 1, 3).reshape(nb, heads * t, w)


def kernel(x_prompt, x_sample, cache_mla_ckv, cache_mla_kpe, cache_dsa_k, cache_dsa_v, cache_idx_k,
           state_ffn_conv, page_table, meta_tokens, norm1_g, w_in, g_q, g_kv, w_uq, w_uk, w_uv,
           w_br_a, w_br_b, w_o, norm2_g, w_ffn_g, w_ffn_u, ffn_conv_w, ffn_conv_b, w_ffn_d, final_g):
    depth = w_in.shape[0]
    assert depth == 1
    nb, seq, d = x_prompt.shape
    nsb, t_new, _ = x_sample.shape
    n_pages = page_table.shape[1]
    past_len = n_pages * PAGE_SIZE
    l_valid = N_META + seq
    tq, tk = 256, 256
    lp = -(-l_valid // tk) * tk
    assert l_valid % SUBLANES == 0 and t_new % SUBLANES == 0 and t_new <= PAGE_SIZE
    d_ff = w_ffn_g.shape[-1]
    l = 0

    wi = w_in[l]
    cuts = np.cumsum([MLA_Q_LORA, MLA_KV_LORA, MLA_ROPE, HD, HD, HD, HD, IDX_DIM, IDX_HEADS, d, d])
    c_q, c_kv, k_pe, q_b, k_b, v_b, q_i, k_i, w_i, g_a, g_b = jnp.split(wi, cuts[:-1], axis=1)
    pad = jnp.zeros((d, LANES - IDX_DIM - MLA_ROPE - IDX_HEADS), wi.dtype)
    win = jnp.concatenate([c_q, c_kv, q_b, k_b, v_b, q_i, g_a, g_b, k_i, k_pe, w_i, pad],
                          axis=1).astype(BF16)
    wuq = jnp.concatenate([w_uq[l][:, :, :MLA_NOPE].reshape(MLA_Q_LORA, -1),
                           w_uq[l][:, :, MLA_NOPE:].reshape(MLA_Q_LORA, -1)], axis=1).astype(BF16)
    wuk2 = w_uk[l].reshape(MLA_KV_LORA, HD)
    wuv2 = w_uv[l].reshape(MLA_KV_LORA, HD)
    wkv_p = jnp.concatenate([wuk2, wuv2], axis=1).astype(BF16)
    ukt = w_uk[l].transpose(1, 2, 0)
    eye = jnp.eye(MLA_HEADS, dtype=ukt.dtype)
    wuk_bd = (ukt[:, :, None, :] * eye[:, None, :, None]).reshape(HD, MLA_HEADS * MLA_KV_LORA)
    wuk_bd = wuk_bd.astype(BF16)
    g1 = norm1_g[l][None]
    gq = g_q[l][None]
    gkv = g_kv[l][None]
    g2 = norm2_g[l][None]
    gf = final_g[None]
    wa, wb, wo = w_br_a[l].astype(BF16), w_br_b[l].astype(BF16), w_o[l].astype(BF16)
    wg, wu, wd = w_ffn_g[l].astype(BF16), w_ffn_u[l].astype(BF16), w_ffn_d[l].astype(BF16)
    cw, cb = ffn_conv_w[l], ffn_conv_b[l][None]

    meta = jnp.broadcast_to(meta_tokens[None].astype(x_prompt.dtype), (nb, N_META, d))
    xp = jnp.concatenate([meta, x_prompt, jnp.zeros((nb, lp - l_valid, d), x_prompt.dtype)], axis=1)
    tm = _pick_tile(lp, (384, 256, 128))
    tabs_p = _rope_tables(np.arange(lp))
    (ckv_p, kb_p, vb_p, small_p, qn, qp, qb, qi, gate_p, kcat, vm, kbb, vbb, ki2) = _proj_call(
        xp, tabs_p, g1, gq, gkv, win, wuq, wkv_p, tm, sample=False, l_out=l_valid)
    attn_args = (qn, qp, qb, qi, small_p, kcat, vm, kbb, vbb, ki2, l_valid)
    full_tiles, rest_rows = divmod(l_valid, tq)
    if 0 < rest_rows <= _TAIL_TQ and full_tiles:
        mla_p, dsa_p = _attn_call(*attn_args, tq, tk, n_keep=full_tiles)
        mla_p, dsa_p = _attn_call(*attn_args, _TAIL_TQ, tk, j0=full_tiles * tq // _TAIL_TQ,
                                  n_tiles=1, into=(mla_p, dsa_p))
    else:
        mla_p, dsa_p = _attn_call(*attn_args, tq, tk)
    tm_ffn = _pick_tile(lp, (768, 384, 256, 128))
    x2_p, h2_p = _merge_call(xp, mla_p, dsa_p, gate_p, wa, wb, wo, g2, tm_ffn, "merge_prompt")
    zstate = jnp.zeros((1, SUBLANES, d_ff), F32)
    y_p, tail_p = _ffn_call(h2_p, x2_p, zstate, zstate, wg, wu, wd, cw, cb, gf, tm_ffn, l_valid,
                            False, "ffn_prompt")

    ns = nsb * t_new
    xs = x_sample.reshape(1, ns, d)
    tabs_s = _rope_tables(past_len + (np.arange(ns) % t_new))
    (ckv_s, kb_s, vb_s, small_s, qlat, qp_s, qb_s, qi_s, gate_s) = _proj_call(
        xs, tabs_s, g1, gq, gkv, win, wuq, wuk_bd, ns, sample=True)
    qabs = _head_major(qlat, nsb, t_new, MLA_HEADS)
    qpe_r = _head_major(qp_s, nsb, t_new, MLA_HEADS)
    qi_r = _head_major(qi_s, nsb, t_new, IDX_HEADS)
    w_rows = small_s[0, :, _L_WI:_L_WI + IDX_HEADS].reshape(nsb, t_new, IDX_HEADS)
    w_rows = jnp.broadcast_to(w_rows.transpose(0, 2, 1).reshape(nsb, IDX_HEADS * t_new, 1),
                              (nsb, IDX_HEADS * t_new, LANES))
    qb_r = _head_major(qb_s, nsb, t_new, DSA_HEADS)
    qbd = jnp.where(_np_head_mask(t_new), jnp.tile(qb_r, (1, 1, DSA_HEADS)), jnp.zeros((), BF16))
    new_rows = lambda a: _pad_rows(a.reshape(nsb, t_new, -1), PAGE_SIZE).astype(BF16)
    ckvn = new_rows(ckv_s)
    kpen = new_rows(small_s[..., _L_KPE:_L_KPE + MLA_ROPE])
    ikn = new_rows(small_s[..., _L_KI:_L_KI + IDX_DIM])
    kbn, vbn = new_rows(kb_s), new_rows(vb_s)
    n_pool = cache_dsa_k.shape[1]
    keys_minor = lambda pool: jnp.moveaxis(pool[l], 1, -1).reshape(n_pool, -1, PAGE_SIZE)
    mla_s, scp, scn = _smla_call(page_table, qabs, qpe_r, qi_r, w_rows, ckvn, kpen, ikn,
                                 wuv2.astype(BF16), cache_mla_ckv[l], keys_minor(cache_mla_kpe),
                                 keys_minor(cache_idx_k), t_new)
    dsa_s = _sdsa_call(page_table, qbd, scp, scn, kbn, vbn,
                       keys_minor(cache_dsa_k), keys_minor(cache_dsa_v), t_new)
    x2_s, h2_s = _merge_call(xs, mla_s.reshape(1, ns, HD), dsa_s.reshape(1, ns, HD), gate_s,
                             wa, wb, wo, g2, ns, "merge_sample")
    st = state_ffn_conv[l]
    zrow = jnp.zeros((nsb, 1, d_ff), st.dtype)
    s1 = jnp.concatenate([st[:, 1:2]] + [zrow] * (t_new - 1), axis=1).reshape(1, ns, d_ff)
    s2 = jnp.concatenate([st[:, 0:1], st[:, 1:2]] + [zrow] * (t_new - 2), axis=1).reshape(1, ns, d_ff)
    y_s, tail_s = _ffn_call(h2_s, x2_s, s1, s2, wg, wu, wd, cw, cb, gf, ns, t_new, True,
                            "ffn_sample")

    y_prompt = y_p[:, N_META:l_valid]
    y_sample = y_s.reshape(nsb, t_new, d)
    cut = lambda a: a[:, :l_valid]
    new_ckv_p = ckv_p[None]
    new_kpe_p = cut(small_p)[..., _L_KPE:_L_KPE + MLA_ROPE][None]
    pos_last = lambda a: jnp.moveaxis(
        a.reshape(nb, DSA_HEADS, DSA_HEAD_DIM, l_valid), -1, 1)[None]
    new_k_p, new_v_p = pos_last(kb_p), pos_last(vb_p)
    new_ik_p = cut(small_p)[..., _L_KI:_L_KI + IDX_DIM][None]
    new_conv_p = tail_p[:, SUBLANES - (CONV_W - 1):][None]
    per_s = lambda a: a.reshape(nsb, t_new, -1)
    new_ckv_s = per_s(ckv_s)[None]
    new_kpe_s = per_s(small_s)[..., _L_KPE:_L_KPE + MLA_ROPE][None]
    new_k_s = per_s(kb_s).reshape(1, nsb, t_new, DSA_HEADS, DSA_HEAD_DIM)
    new_v_s = per_s(vb_s).reshape(1, nsb, t_new, DSA_HEADS, DSA_HEAD_DIM)
    new_ik_s = per_s(small_s)[..., _L_KI:_L_KI + IDX_DIM][None]
    new_conv_s = per_s(tail_s)[:, t_new - (CONV_W - 1):][None]
    return (y_prompt, y_sample, new_ckv_p, new_kpe_p, new_k_p, new_v_p, new_ik_p, new_conv_p,
            new_ckv_s, new_kpe_s, new_k_s, new_v_s, new_ik_s, new_conv_s)


def _np_head_mask(t_new):
    r = np.arange(DSA_HEADS * t_new)[:, None] // t_new
    c = np.arange(HD)[None, :] // DSA_HEAD_DIM
    return jnp.asarray(r == c)[None]
```

```python
import functools

import numpy as np
import jax
import jax.numpy as jnp
from jax import lax
from jax.experimental import pallas as pl
from jax.experimental.pallas import tpu as pltpu

N_META = 16
MLA_HEADS = 8
MLA_NOPE = 64
MLA_ROPE = 32
MLA_V = 64
MLA_KV_LORA = 256
MLA_Q_LORA = 768
MLA_SCALE = (MLA_NOPE + MLA_ROPE) ** -0.5
DSA_HEADS = 8
DSA_HEAD_DIM = 64
DSA_SCALE = DSA_HEAD_DIM ** -0.5
IDX_HEADS = 8
IDX_DIM = 64
IDX_SCALE = IDX_DIM ** -0.5
LOG2E = 1.4426950408889634
MLA_QSCALE = MLA_SCALE * LOG2E
DSA_QSCALE = DSA_SCALE * LOG2E
TOPK_MAX = 256
CONV_W = 3
ROPE_THETA = 10000.0
EPS = 1e-6
NEG_INF = -1e30
PAGE_SIZE = 128

LANES = 128
SUBLANES = 8
HD = 512
VMEM_LIMIT = 56 * 1024 * 1024

F32 = jnp.float32
BF16 = jnp.bfloat16

_C_Q, _C_KV, _Q_B, _K_B, _V_B, _Q_I, _G_A, _G_B, _SMALL, _D_IN_P = (
    0, 768, 1024, 1536, 2048, 2560, 3072, 4096, 5120, 5248)
_L_KI, _L_KPE, _L_WI = 0, 64, 96


def _dot(a, b):
    return jnp.dot(a, b, preferred_element_type=F32)


def _dot_t(a, b):
    return lax.dot_general(a, b, (((1,), (1,)), ((), ())), preferred_element_type=F32)


def _rms(x, g):
    ms = jnp.mean(x * x, axis=-1, keepdims=True)
    return x * lax.rsqrt(ms + EPS) * g


def _rope_tables(pos):
    pos = jnp.asarray(pos).astype(F32)[:, None]
    lane = np.arange(LANES)
    out = []
    for width in (64, 32):
        half = width // 2
        m = lane % width
        inv_freq = 1.0 / (ROPE_THETA ** (jnp.arange(half, dtype=F32) / half))
        ang = pos * inv_freq[m % half][None, :]
        c, s = jnp.cos(ang), jnp.sin(ang)
        out += [c, jnp.where(m >= half, s, 0.0), jnp.where(m < half, -s, 0.0)]
    return out


def _rope_blk(x, c, sa, sb, half):
    return x * c + pltpu.roll(x, half, 1) * sa + pltpu.roll(x, LANES - half, 1) * sb


def _proj_kernel(sample, x_ref, c64_ref, sa64_ref, sb64_ref, c32_ref, sa32_ref, sb32_ref,
                 g1_ref, gq_ref, gkv_ref, win_ref, wuq_ref, wkv_ref, *outs):
    if sample:
        (ckv_o, kb_o, vb_o, small_o, qlat_o, qp_o, qb_o, qi_o, gate_o) = outs
    else:
        (ckv_o, kb_o, vb_o, small_o, qn_o, qp_o, qb_o, qi_o, gate_o,
         kcat_o, vm_o, kbb_o, vbb_o, ki2_o) = outs
    hb = _rms(x_ref[0], g1_ref[...]).astype(BF16)
    c64, sa64, sb64 = c64_ref[...], sa64_ref[...], sb64_ref[...]
    c32, sa32, sb32 = c32_ref[...], sa32_ref[...], sb32_ref[...]

    def proj(a, b):
        return _dot(hb, win_ref[:, a:b])

    cq = _rms(proj(_C_Q, _C_KV), gq_ref[...]).astype(BF16)
    ckv = _rms(proj(_C_KV, _Q_B), gkv_ref[...])
    ckv_o[0] = ckv

    zq = proj(_Q_B, _K_B)
    zk = proj(_K_B, _V_B)
    zi = proj(_Q_I, _G_A)
    for blk in range(HD // LANES):
        sl = slice(blk * LANES, (blk + 1) * LANES)
        qb_o[0, :, sl] = (_rope_blk(zq[:, sl], c64, sa64, sb64, 32) * DSA_QSCALE).astype(BF16)
        qi_o[0, :, sl] = (_rope_blk(zi[:, sl], c64, sa64, sb64, 32) * IDX_SCALE).astype(BF16)
        kr = _rope_blk(zk[:, sl], c64, sa64, sb64, 32)
        if sample:
            kb_o[0, :, sl] = kr
        else:
            kb_o[0, sl, :] = kr.T
            kbb_o[0, :, sl] = kr.astype(BF16)
    zv = proj(_V_B, _Q_I)
    if sample:
        vb_o[0] = zv
    else:
        for blk in range(HD // LANES):
            sl = slice(blk * LANES, (blk + 1) * LANES)
            zvt = zv[:, sl].T
            vb_o[0, sl, :] = zvt
            vbb_o[0, sl, :] = zvt.astype(BF16)

    gate_o[0] = jax.nn.sigmoid(proj(_G_A, _SMALL)).astype(BF16)

    zs = proj(_SMALL, _D_IN_P)
    lane = lax.broadcasted_iota(jnp.int32, (1, LANES), 1)
    m_ki = (lane < _L_KPE).astype(F32)
    m_kpe = ((lane >= _L_KPE) & (lane < _L_WI)).astype(F32)
    m_wi = ((lane >= _L_WI) & (lane < _L_WI + IDX_HEADS)).astype(F32)
    small = (zs * (c64 * m_ki + c32 * m_kpe + (IDX_HEADS ** -0.5) * m_wi)
             + pltpu.roll(zs, 32, 1) * (sa64 * m_ki) + pltpu.roll(zs, 96, 1) * (sb64 * m_ki)
             + pltpu.roll(zs, 16, 1) * (sa32 * m_kpe) + pltpu.roll(zs, 112, 1) * (sb32 * m_kpe))
    small_o[0] = small
    if not sample:
        ki = small * m_ki
        ki2_o[0] = (ki + pltpu.roll(ki, 64, 1)).astype(BF16)
        kp = pltpu.roll(small * m_kpe, 64, 1)
        kpe4 = (kp + pltpu.roll(kp, 32, 1) + pltpu.roll(kp, 64, 1)
                + pltpu.roll(kp, 96, 1)).astype(BF16)
        for pair in range(MLA_HEADS // 2):
            kcat_o[0, :, (2 * pair + 1) * LANES:(2 * pair + 2) * LANES] = kpe4

    q = _dot(cq, wuq_ref[...])
    qn = (q[:, :HD] * MLA_QSCALE).astype(BF16)
    for blk in range(2):
        qpe = q[:, HD + blk * LANES:HD + (blk + 1) * LANES]
        qp_o[0, :, blk * LANES:(blk + 1) * LANES] = (
            _rope_blk(qpe, c32, sa32, sb32, 16) * MLA_QSCALE).astype(BF16)

    if sample:
        qlat_o[0] = _dot(qn, wkv_ref[...]).astype(BF16)
    else:
        qn_o[0] = qn
        kv = _dot(ckv.astype(BF16), wkv_ref[...])
        for pair in range(MLA_HEADS // 2):
            kcat_o[0, :, 2 * pair * LANES:(2 * pair + 1) * LANES] = (
                kv[:, pair * LANES:(pair + 1) * LANES].astype(BF16))
        for blk in range(HD // LANES):
            vm_o[0, blk * LANES:(blk + 1) * LANES, :] = (
                kv[:, HD + blk * LANES:HD + (blk + 1) * LANES].T.astype(BF16))


def _const_spec(shape):
    nd = len(shape)
    return pl.BlockSpec(shape, lambda *_: (0,) * nd, pipeline_mode=pl.Buffered(1))


def _proj_call(x, tables, g1, gq, gkv, win, wuq, wkv, tm, sample, l_out=None):
    nb, lp, d = x.shape
    l_out = lp if l_out is None else l_out
    grid = (nb, lp // tm)
    row = lambda w: pl.BlockSpec((1, tm, w), lambda b, j: (b, j, 0))
    tab = pl.BlockSpec((tm, LANES), lambda b, j: (j, 0))
    in_specs = ([row(d)] + [tab] * 6
                + [_const_spec(a.shape) for a in (g1, gq, gkv, win, wuq, wkv)])
    f32_w = [MLA_KV_LORA, HD, HD, LANES]
    if sample:
        bf_w = [MLA_HEADS * MLA_KV_LORA, 2 * LANES, HD, HD, 2048]
    else:
        bf_w = [HD, 2 * LANES, HD, HD, 2048, 2 * HD, HD, HD, HD, LANES]
    out_shape = ([jax.ShapeDtypeStruct((nb, lp, w), F32) for w in f32_w]
                 + [jax.ShapeDtypeStruct((nb, lp, w), BF16) for w in bf_w])
    out_specs = [row(w) for w in f32_w + bf_w]
    out_shape[0] = jax.ShapeDtypeStruct((nb, l_out, MLA_KV_LORA), F32)
    if not sample:
        n_f32 = len(f32_w)
        for i, dt, rows in ((1, F32, l_out), (2, F32, l_out), (n_f32 + 6, BF16, lp),
                            (n_f32 + 8, BF16, lp)):
            out_shape[i] = jax.ShapeDtypeStruct((nb, HD, rows), dt)
            out_specs[i] = pl.BlockSpec((1, HD, tm), lambda b, j: (b, 0, j))
    return pl.pallas_call(
        functools.partial(_proj_kernel, sample),
        grid=grid, in_specs=in_specs, out_specs=out_specs, out_shape=out_shape,
        compiler_params=pltpu.CompilerParams(
            dimension_semantics=("arbitrary", "arbitrary"), vmem_limit_bytes=VMEM_LIMIT),
        name="proj_sample" if sample else "proj_prompt",
    )(x, *tables, g1, gq, gkv, win, wuq, wkv)


_MAX_BISECT = 320
_BISECT_UNROLL = 4


def _chunk_ds(c, tk):
    start = c * tk
    return pl.ds(start if isinstance(start, int) else pl.multiple_of(start, tk), tk)


def _key_chunk(ref, c, tk, axis):
    ds = _chunk_ds(c, tk)
    return ref[ds, :] if axis == 0 else ref[:, ds]


def _key_fold(v, op, axis):
    if axis == 0:
        parts = [v[i * SUBLANES:(i + 1) * SUBLANES] for i in range(v.shape[0] // SUBLANES)]
    else:
        parts = [v[:, i * LANES:(i + 1) * LANES] for i in range(v.shape[1] // LANES)]
    while len(parts) > 1:
        parts = [op(a, b) for a, b in zip(parts[::2], parts[1::2])] + (
            [parts[-1]] if len(parts) % 2 else [])
    return parts[0]


def _key_index(c, tk, axis):
    shape = (tk, 1) if axis == 0 else (1, tk)
    return (lax.convert_element_type(c * tk, F32)
            + lax.broadcasted_iota(jnp.int32, shape, axis).astype(F32))


def _count(sc_ref, nck, tk, axis, pred):
    nq = sc_ref.shape[1 - axis]
    part = (SUBLANES, nq) if axis == 0 else (nq, LANES)

    def body(c, acc):
        v = jnp.where(pred(_key_chunk(sc_ref, c, tk, axis), c), 1.0, 0.0)
        return acc + _key_fold(v, jnp.add, axis)

    acc = lax.fori_loop(0, nck, body, jnp.zeros(part, F32))
    return jnp.sum(acc, axis=axis, keepdims=True)


def _stats_init(part):
    return (jnp.full(part, -NEG_INF, F32), jnp.full(part, NEG_INF, F32),
            jnp.zeros(part, F32), jnp.zeros(part, F32))


def _stats_update(stats, x, axis, has_masked):
    mn, mx, gt0, ge0 = stats
    xv = jnp.where(x > 0.5 * NEG_INF, x, -NEG_INF) if has_masked else x
    return (jnp.minimum(mn, _key_fold(xv, jnp.minimum, axis)),
            jnp.maximum(mx, _key_fold(x, jnp.maximum, axis)),
            gt0 + _key_fold(jnp.where(x > 0.0, 1.0, 0.0), jnp.add, axis),
            ge0 + _key_fold(jnp.where(x >= 0.0, 1.0, 0.0), jnp.add, axis))


def _select_threshold(sc_ref, nck, tk, k_eff, n_valid, axis, stats=None):
    nq = sc_ref.shape[1 - axis]
    part = (SUBLANES, nq) if axis == 0 else (nq, LANES)

    if stats is None:
        stats = lax.fori_loop(
            0, nck, lambda c, st: _stats_update(st, _key_chunk(sc_ref, c, tk, axis), axis, True),
            _stats_init(part))
    mn, mx, gt0, ge0 = stats
    lo0 = jnp.min(mn, axis=axis, keepdims=True)
    hi0 = jnp.max(mx, axis=axis, keepdims=True)
    hi0 = hi0 + jnp.maximum(jnp.abs(hi0), 1.0) * 1e-6

    def count_ge(thr):
        return _count(sc_ref, nck, tk, axis, lambda x, c: x >= thr)

    def all_rows(fin):
        return jnp.min(jnp.where(fin, 1.0, 0.0)) > 0.5

    c_gt0 = jnp.sum(gt0, axis=axis, keepdims=True)
    c_ge0 = jnp.sum(ge0, axis=axis, keepdims=True)
    at_zero = (c_gt0 < k_eff) & (c_ge0 >= k_eff)
    above = c_gt0 >= k_eff
    inside = (lo0 < 0.0) & (hi0 > 0.0)
    lo_at_zero = at_zero | (above & inside)
    c_lo0 = jnp.where(lo_at_zero, c_ge0, n_valid)
    lo0, hi0 = (jnp.where(lo_at_zero, 0.0, lo0),
                jnp.where(at_zero | (~above & inside), 0.0, hi0))

    def cond(st):
        it, _, _, _, done = st
        return jnp.logical_and(it < _MAX_BISECT, jnp.logical_not(done))

    def body(st):
        it, lo, hi, c_lo, _ = st
        for _ in range(_BISECT_UNROLL):
            mid = 0.5 * lo + 0.5 * hi
            c = count_ge(mid)
            ge = c >= k_eff
            fin = (mid <= lo) | (mid >= hi)
            lo, hi, c_lo = jnp.where(ge, mid, lo), jnp.where(ge, hi, mid), jnp.where(ge, c, c_lo)
            fin = fin | (c_lo == k_eff)
        return it + _BISECT_UNROLL, lo, hi, c_lo, all_rows(fin)

    fin0 = (c_lo0 == k_eff) | (lo0 >= hi0)
    _, t, _, c_t, _ = lax.while_loop(
        cond, body, (jnp.int32(0), lo0, hi0, c_lo0, all_rows(fin0)))

    tie_rows = c_t > k_eff

    @pl.when(jnp.max(jnp.where(tie_rows, 1.0, 0.0)) > 0.5)
    def _():
        n_keys = sc_ref.shape[axis]
        need = k_eff - _count(sc_ref, nck, tk, axis, lambda x, c: x > t)

        def tie_count(m):
            return _count(sc_ref, nck, tk, axis,
                          lambda x, c: (x == t) & (_key_index(c, tk, axis) <= m))

        def ibody(_, st):
            lo_i, hi_i = st
            mid = jnp.floor(0.5 * (lo_i + hi_i))
            ok = tie_count(mid) >= need
            return jnp.where(ok, lo_i, mid), jnp.where(ok, mid, hi_i)

        steps = int(np.ceil(np.log2(n_keys))) + 1
        _, m_idx = lax.fori_loop(0, steps, ibody,
                                 (jnp.full(t.shape, -1.0, F32),
                                  jnp.full(t.shape, float(n_keys - 1), F32)))

        def fix(c, carry):
            x = _key_chunk(sc_ref, c, tk, axis)
            drop = (x == t) & (_key_index(c, tk, axis) > m_idx) & tie_rows
            ds = _chunk_ds(c, tk)
            if axis == 0:
                sc_ref[ds, :] = jnp.where(drop, NEG_INF, x)
            else:
                sc_ref[:, ds] = jnp.where(drop, NEG_INF, x)
            return carry

        lax.fori_loop(0, nck, fix, 0)

    return t


def _lane_blocks(s):
    return [s[:, i * LANES:(i + 1) * LANES] for i in range(s.shape[1] // LANES)]


def _scale_cols(alpha, x):
    return jnp.concatenate([alpha * b for b in _lane_blocks(x)], axis=1)


def _online_update(m_ref, l_ref, acc_ref, idx, blocks, pv):
    m = m_ref[idx]
    bm = blocks[0]
    for b in blocks[1:]:
        bm = jnp.maximum(bm, b)
    m_new = jnp.maximum(m, jnp.max(bm, axis=1, keepdims=True))
    alpha = jnp.exp2(m - m_new)
    ps = [jnp.exp2(b - m_new) for b in blocks]
    rs = ps[0]
    for p in ps[1:]:
        rs = rs + p
    l_ref[idx] = alpha * l_ref[idx] + jnp.sum(rs, axis=1, keepdims=True)
    acc_ref[idx] = _scale_cols(alpha, acc_ref[idx]) + pv([p.astype(BF16) for p in ps])
    m_ref[idx] = m_new


def _reset_state(m_ref, l_ref, acc_ref):
    m_ref[...] = jnp.full_like(m_ref, NEG_INF)
    l_ref[...] = jnp.zeros_like(l_ref)
    acc_ref[...] = jnp.zeros_like(acc_ref)


def _softmax_keys_major(m_ref, l_ref, idx, s):
    m = m_ref[idx]
    m_new = jnp.maximum(m, jnp.max(_key_fold(s, jnp.maximum, 0), axis=0, keepdims=True))
    alpha = jnp.exp2(m - m_new)
    p = jnp.exp2(s - m_new)
    l_ref[idx] = alpha * l_ref[idx] + jnp.sum(_key_fold(p, jnp.add, 0), axis=0, keepdims=True)
    m_ref[idx] = m_new
    return alpha, p.astype(BF16)


def _attn_kernel(tq, tk, j0, n_keep, topk, n_alias,
                 qn_ref, qp_ref, qb_ref, qi_ref, small_ref,
                 kcat_ref, vm_ref, kb_ref, vb_ref, ki2_ref, *rest):
    (mla_ref, dsa_ref, sc_ref, qc_ref, qd_ref, qx_ref, w_ref, st_ref, m_ref, l_ref,
     acc_ref) = rest[n_alias:]
    j = pl.program_id(1) + j0
    nh = MLA_HEADS
    n_pairs = nh // 2

    @pl.when(j >= n_keep)
    def _():
        mla_ref[...] = jnp.zeros_like(mla_ref)
        dsa_ref[...] = jnp.zeros_like(dsa_ref)

    @pl.when(j < n_keep)
    def _():
        nck = (j * tq + tq - 1) // tk + 1
        last = nck - 1
        lane = lax.broadcasted_iota(jnp.int32, (1, LANES), 1)
        q_row = j * tq + lax.broadcasted_iota(jnp.int32, (1, tq), 1)
        k_row = lax.broadcasted_iota(jnp.int32, (tk, 1), 0)
        zero_b = jnp.zeros((tq, LANES), BF16)

        def ksl(c):
            return _chunk_ds(c, tk)

        def pair_sl(pair):
            return slice(pair * LANES, (pair + 1) * LANES)

        def two(x):
            return jnp.concatenate([x, x], axis=1)

        def write_heads(out_ref):
            half = LANES // 2
            for pair in range(n_pairs):
                o = acc_ref[pair] / l_ref[pair]
                o = jnp.concatenate([o[:half, :tq], o[half:, tq:]], axis=0)
                out_ref[0, :, pair_sl(pair)] = o.T.astype(BF16)

        small = small_ref[0]
        for h in range(nh):
            pair, sub = divmod(h, 2)
            grp, gsub = divmod(h, 4)
            rows = slice(sub * tq, (sub + 1) * tq)
            own = (lane >= 64) == bool(sub)
            qc_ref[pair, rows, :LANES] = jnp.where(own, qn_ref[0, :, pair_sl(pair)], zero_b)
            qc_ref[pair, rows, LANES:] = jnp.where((lane // 32) == gsub,
                                                   qp_ref[0, :, pair_sl(grp)], zero_b)
            qd_ref[pair, rows, :] = jnp.where(own, qb_ref[0, :, pair_sl(pair)], zero_b)
            qx_ref[h * tq:(h + 1) * tq, :] = jnp.where(own, qi_ref[0, :, pair_sl(pair)], zero_b)
        w_ref[...] = small.T[_L_WI:_L_WI + IDX_HEADS]

        def attend(c, score_fn, fix_fn, vt_ref):
            ks = ksl(c)
            s = {p: score_fn(p, ks) for p in range(min(2, n_pairs))}
            for p in range(n_pairs):
                alpha, pr = _softmax_keys_major(m_ref, l_ref, p, fix_fn(s.pop(p)))
                if p + 2 < n_pairs:
                    s[p + 2] = score_fn(p + 2, ks)
                acc_ref[p] = alpha * acc_ref[p] + _dot(vt_ref[0, pair_sl(p), ks], pr)

        _reset_state(m_ref, l_ref, acc_ref)
        for i, part in enumerate(_stats_init((SUBLANES, tq))):
            st_ref[i] = part

        def mla_scores(p, ks):
            return _dot_t(kcat_ref[0, ks, 2 * p * LANES:(2 * p + 2) * LANES], qc_ref[p])

        def mla_idx_chunk(c, masked):
            ks = ksl(c)
            r = _dot_t(ki2_ref[0, ks, :], qx_ref[...])
            sc = None
            for h in range(IDX_HEADS):
                term = jnp.maximum(r[:, h * tq:(h + 1) * tq], 0.0) * w_ref[h:h + 1, :]
                sc = term if sc is None else sc + term
            if masked:
                sc = jnp.where((c * tk + k_row) <= q_row, sc, NEG_INF)
                vis2 = (c * tk + k_row) <= two(q_row)
                fix = lambda s: jnp.where(vis2, s, NEG_INF)
            else:
                fix = lambda s: s
            sc_ref[ks, :] = sc
            for i, part in enumerate(_stats_update(tuple(st_ref[i] for i in range(4)), sc, 0,
                                                   masked)):
                st_ref[i] = part
            attend(c, mla_scores, fix, vm_ref)

        def mla_idx_step(i, carry):
            mla_idx_chunk(2 * i, False)
            mla_idx_chunk(2 * i + 1, False)
            return carry

        lax.fori_loop(0, last // 2, mla_idx_step, 0)

        @pl.when(last % 2 == 1)
        def _():
            mla_idx_chunk(last - 1, False)

        mla_idx_chunk(last, True)
        write_heads(mla_ref)

        n_valid = (q_row + 1).astype(F32)
        k_eff = jnp.minimum(n_valid, float(topk))
        t2 = two(_select_threshold(sc_ref, nck, tk, k_eff, n_valid, 0,
                                   tuple(st_ref[i] for i in range(4))))

        _reset_state(m_ref, l_ref, acc_ref)

        def dsa_scores(p, ks):
            return _dot_t(kb_ref[0, ks, pair_sl(p)], qd_ref[p])

        def dsa_chunk(c):
            sel = two(sc_ref[ksl(c), :]) >= t2
            attend(c, dsa_scores, lambda s: jnp.where(sel, s, NEG_INF), vb_ref)

        def dsa_step(i, carry):
            dsa_chunk(2 * i)
            dsa_chunk(2 * i + 1)
            return carry

        lax.fori_loop(0, nck // 2, dsa_step, 0)

        @pl.when(nck % 2 == 1)
        def _():
            dsa_chunk(nck - 1)

        write_heads(dsa_ref)


_TAIL_TQ = 64


def _attn_call(qn, qp, qb, qi, small, kcat, vm, kbb, vbb, ki2, l_valid, tq, tk,
               j0=0, n_tiles=None, n_keep=None, into=()):
    nb, lp, _ = qn.shape
    n_tiles = lp // tq if n_tiles is None else n_tiles
    n_keep = -(-l_valid // tq) if n_keep is None else n_keep
    topk = min(TOPK_MAX, l_valid // 4)
    n_pairs = MLA_HEADS // 2
    qrow = lambda w: pl.BlockSpec((1, tq, w), lambda b, j: (b, j + j0, 0))
    krow = lambda w: pl.BlockSpec((1, lp, w), lambda b, j: (b, 0, 0))
    vcol = pl.BlockSpec((1, HD, lp), lambda b, j: (b, 0, 0))
    in_specs = [qrow(HD), qrow(2 * LANES), qrow(HD), qrow(HD), qrow(LANES),
                krow(2 * HD), vcol, krow(HD), vcol, krow(LANES)] + [qrow(HD)] * len(into)
    n_in = 10
    return pl.pallas_call(
        functools.partial(_attn_kernel, tq, tk, j0, n_keep, topk, len(into)),
        grid=(nb, n_tiles), in_specs=in_specs,
        out_specs=[qrow(HD), qrow(HD)],
        out_shape=[jax.ShapeDtypeStruct((nb, lp, HD), BF16)] * 2,
        input_output_aliases={n_in + i: i for i in range(len(into))},
        scratch_shapes=[pltpu.VMEM((lp, tq), F32),
                        pltpu.VMEM((n_pairs, 2 * tq, 2 * LANES), BF16),
                        pltpu.VMEM((n_pairs, 2 * tq, LANES), BF16),
                        pltpu.VMEM((IDX_HEADS * tq, LANES), BF16),
                        pltpu.VMEM((IDX_HEADS, tq), F32),
                        pltpu.VMEM((4, SUBLANES, tq), F32),
                        pltpu.VMEM((n_pairs, 1, 2 * tq), F32),
                        pltpu.VMEM((n_pairs, 1, 2 * tq), F32),
                        pltpu.VMEM((n_pairs, LANES, 2 * tq), F32)],
        compiler_params=pltpu.CompilerParams(
            dimension_semantics=("arbitrary", "arbitrary"), vmem_limit_bytes=VMEM_LIMIT),
        name="attn_prompt" if not into else "attn_prompt_tail",
    )(qn, qp, qb, qi, small, kcat, vm, kbb, vbb, ki2, *into)


def _merge_kernel(x_ref, mla_ref, dsa_ref, gate_ref, wa_ref, wb_ref, wo_ref, g2_ref,
                  x2_ref, h2_ref):
    d = x_ref.shape[-1]
    a = _dot(mla_ref[0], wa_ref[...])
    b = _dot(dsa_ref[0], wb_ref[...])
    g = gate_ref[0]
    o = g[:, :d].astype(F32) * a + g[:, d:].astype(F32) * b
    x2 = x_ref[0] + _dot(o.astype(BF16), wo_ref[...])
    x2_ref[0] = x2
    h2_ref[0] = _rms(x2, g2_ref[...]).astype(BF16)


def _merge_call(x, mla, dsa, gate, wa, wb, wo, g2, tm, name):
    nb, lp, d = x.shape
    row = lambda w: pl.BlockSpec((1, tm, w), lambda b, j: (b, j, 0))
    return pl.pallas_call(
        _merge_kernel, grid=(nb, lp // tm),
        in_specs=[row(d), row(HD), row(HD), row(2 * d)]
        + [_const_spec(a.shape) for a in (wa, wb, wo, g2)],
        out_specs=[row(d), row(d)],
        out_shape=[jax.ShapeDtypeStruct((nb, lp, d), F32), jax.ShapeDtypeStruct((nb, lp, d), BF16)],
        compiler_params=pltpu.CompilerParams(
            dimension_semantics=("arbitrary", "arbitrary"), vmem_limit_bytes=VMEM_LIMIT),
        name=name,
    )(x, mla, dsa, gate, wa, wb, wo, g2)


_FF_CHUNK = 1024
_FF_AHEAD = 1


def _ffn_kernel(seq_rows, tail_tile, tail_off, h_ref, x_ref, s1_ref, s2_ref,
                wg_ref, wu_ref, wd_ref, cw_ref, cb_ref, gf_ref, y_ref, tail_ref, prev_ref):
    j = pl.program_id(1)
    tm = h_ref.shape[1]
    d_ff = wg_ref.shape[1]
    h = h_ref[0]
    row = lax.broadcasted_iota(jnp.int32, (tm, 1), 0)
    if seq_rows is None:
        @pl.when(j == 0)
        def _():
            prev_ref[...] = jnp.zeros_like(prev_ref)
        first1, first2 = row < 1, row < 2
    else:
        first1, first2 = (row % seq_rows) < 1, (row % seq_rows) < 2

    acc = jnp.zeros((tm, x_ref.shape[-1]), F32)
    chunks = [slice(c0, min(c0 + _FF_CHUNK, d_ff)) for c0 in range(0, d_ff, _FF_CHUNK)]

    def up(sl):
        return _dot(h, wg_ref[:, sl]), _dot(h, wu_ref[:, sl])

    ahead = [up(sl) for sl in chunks[:_FF_AHEAD]]
    for i, sl in enumerate(chunks):
        g, u = ahead.pop(0)
        if i + _FF_AHEAD < len(chunks):
            ahead.append(up(chunks[i + _FF_AHEAD]))
        if seq_rows is None:
            p = prev_ref[:, sl]
            hist1 = jnp.broadcast_to(p[SUBLANES - 1:SUBLANES], g.shape)
            hist2 = jnp.where(row < 1, jnp.broadcast_to(p[SUBLANES - 2:SUBLANES - 1], g.shape),
                              hist1)
            prev_ref[:, sl] = g[tm - SUBLANES:]
        else:
            hist1, hist2 = s1_ref[0, :, sl], s2_ref[0, :, sl]
        g1 = jnp.where(first1, hist1, pltpu.roll(g, 1, 0))
        g2 = jnp.where(first2, hist2, pltpu.roll(g, 2, 0))
        cw = cw_ref[:, sl]
        gc = cb_ref[:, sl] + cw[0:1] * g2 + cw[1:2] * g1 + cw[2:3] * g
        act = (gc * jax.nn.sigmoid(gc) * u).astype(BF16)
        acc = acc + _dot(act, wd_ref[sl, :])
        if seq_rows is None:
            @pl.when(j == tail_tile)
            def _(g=g, sl=sl):
                tail_ref[0, :, sl] = g[tail_off:tail_off + SUBLANES]
        else:
            tail_ref[0, :, sl] = g
    y_ref[0] = _rms(x_ref[0] + acc, gf_ref[...])


def _ffn_call(h2, x2, s1, s2, wg, wu, wd, cw, cb, gf, tm, l_valid, sample, name):
    nb, lp, d = x2.shape
    d_ff = wg.shape[1]
    row = lambda w: pl.BlockSpec((1, tm, w), lambda b, j: (b, j, 0))
    if sample:
        seq_rows, tail_tile, tail_off = l_valid, 0, 0
        tail_shape, tail_spec = (nb, lp, d_ff), row(d_ff)
        s_spec = row(d_ff)
    else:
        seq_rows = None
        tail_tile, tail_off = divmod(l_valid - SUBLANES, tm)
        tail_shape = (nb, SUBLANES, d_ff)
        tail_spec = pl.BlockSpec((1, SUBLANES, d_ff), lambda b, j: (b, 0, 0))
        s_spec = pl.BlockSpec((1, SUBLANES, d_ff), lambda b, j: (0, 0, 0))
    return pl.pallas_call(
        functools.partial(_ffn_kernel, seq_rows, tail_tile, tail_off),
        grid=(nb, lp // tm),
        in_specs=[row(d), row(d), s_spec, s_spec]
        + [_const_spec(a.shape) for a in (wg, wu, wd, cw, cb, gf)],
        out_specs=[row(d), tail_spec],
        out_shape=[jax.ShapeDtypeStruct((nb, lp, d), F32), jax.ShapeDtypeStruct(tail_shape, F32)],
        scratch_shapes=[pltpu.VMEM((SUBLANES, d_ff), F32)],
        compiler_params=pltpu.CompilerParams(
            dimension_semantics=("arbitrary", "arbitrary"), vmem_limit_bytes=VMEM_LIMIT),
        name=name,
    )(h2, x2, s1, s2, wg, wu, wd, cw, cb, gf)


def _diag_heads(full, t, n_heads, width):
    col_head = lax.broadcasted_iota(jnp.int32, (1, n_heads * width), 1) // width
    out = jnp.zeros((t, n_heads * width), F32)
    for h in range(n_heads):
        out = jnp.where(col_head == h, full[h * t:(h + 1) * t], out)
    return out


def _smla_kernel(npp, t_new, pt_ref, qabs_ref, qpe_ref, qi_ref, wrow_ref,
                 ckvn_ref, kpen_ref, ikn_ref, wuv_ref, *rest):
    ckv_pages = rest[:npp]
    kpe_pages = rest[npp:2 * npp]
    ik_pages = rest[2 * npp:3 * npp]
    mla_ref, scp_ref, scn_ref, m_ref, l_ref, acc_ref, ck_ref, kpe_ref, ik_ref = rest[3 * npp:]
    c = pl.program_id(1)
    rows = qabs_ref.shape[1]
    qabs, qpe, qi, wrow = qabs_ref[0], qpe_ref[0], qi_ref[0], wrow_ref[0]

    @pl.when(c == 0)
    def _():
        _reset_state(m_ref, l_ref, acc_ref)

    def update(s, vals):
        _online_update(m_ref, l_ref, acc_ref, 0, _lane_blocks(s),
                       lambda ps: _dot(jnp.concatenate(ps, axis=1), vals))

    def idx_score(s):
        s = _scale_cols(wrow, jnp.maximum(s, 0.0))
        out = s[:t_new]
        for h in range(1, IDX_HEADS):
            out = out + s[h * t_new:(h + 1) * t_new]
        return out

    for i in range(npp):
        ck_ref[i * PAGE_SIZE:(i + 1) * PAGE_SIZE, :] = ckv_pages[i][...].astype(BF16)
        kpe_ref[:, i * PAGE_SIZE:(i + 1) * PAGE_SIZE] = kpe_pages[i][...].astype(BF16)
        ik_ref[:, i * PAGE_SIZE:(i + 1) * PAGE_SIZE] = ik_pages[i][...].astype(BF16)
    ck_all = ck_ref[...]
    update(_dot_t(qabs, ck_all) + _dot(qpe, kpe_ref[...]), ck_all)
    scp_ref[0] = idx_score(_dot(qi, ik_ref[...]))

    @pl.when(c == pl.num_programs(1) - 1)
    def _():
        ck = ckvn_ref[0]
        tok = lax.broadcasted_iota(jnp.int32, (rows, 1), 0) % t_new
        key = lax.broadcasted_iota(jnp.int32, (1, PAGE_SIZE), 1)
        vis = key <= tok
        s = _dot_t(qabs, ck) + _dot_t(qpe, kpen_ref[0])
        update(jnp.where(vis, s, NEG_INF), ck)
        scn_ref[0] = jnp.where(vis[:t_new], idx_score(_dot_t(qi, ikn_ref[0])), NEG_INF)
        o_lat = _scale_cols(1.0 / l_ref[0], acc_ref[0]).astype(BF16)
        mla_ref[0] = _diag_heads(_dot(o_lat, wuv_ref[...]), t_new, MLA_HEADS, MLA_V).astype(BF16)


def _sdsa_kernel(npp, t_new, topk, past_len, sel_tk, pt_ref, qbd_ref, scp_ref, scn_ref, kbn_ref,
                 vbn_ref, *rest):
    k_pages = rest[:npp]
    v_pages = rest[npp:2 * npp]
    dsa_ref, sc_ref, t_ref, m_ref, l_ref, acc_ref, kt_ref, vt_ref = rest[2 * npp:]
    c = pl.program_id(1)
    rows = qbd_ref.shape[1]
    qbd = qbd_ref[0]
    step_keys = npp * PAGE_SIZE

    @pl.when(c == 0)
    def _():
        _reset_state(m_ref, l_ref, acc_ref)
        sc_ref[:, :past_len] = scp_ref[0]
        sc_ref[:, past_len:past_len + PAGE_SIZE] = scn_ref[0]
        if sel_tk > PAGE_SIZE:
            sc_ref[:, past_len + PAGE_SIZE:] = jnp.full((t_new, sel_tk - PAGE_SIZE), NEG_INF, F32)
        tok = lax.broadcasted_iota(jnp.int32, (t_new, 1), 0)
        n_valid = (past_len + 1 + tok).astype(F32)
        k_eff = jnp.minimum(n_valid, float(topk))
        t_ref[...] = _select_threshold(sc_ref, (past_len + sel_tk) // sel_tk, sel_tk,
                                       k_eff, n_valid, 1)

    t = t_ref[...]

    def masked(s, sc):
        pen = jnp.where(sc >= t, 0.0, NEG_INF)
        sel = jnp.concatenate([pen] * DSA_HEADS, axis=0) == 0.0
        return jnp.where(sel, s, NEG_INF)

    def update(s, pv_dot, vals):
        _online_update(m_ref, l_ref, acc_ref, 0, _lane_blocks(s),
                       lambda ps: pv_dot(jnp.concatenate(ps, axis=1), vals))

    for i in range(npp):
        kt_ref[:, i * PAGE_SIZE:(i + 1) * PAGE_SIZE] = k_pages[i][...].astype(BF16)
        vt_ref[:, i * PAGE_SIZE:(i + 1) * PAGE_SIZE] = v_pages[i][...].astype(BF16)
    start = c * step_keys
    start = start if isinstance(start, int) else pl.multiple_of(start, step_keys)
    update(masked(_dot(qbd, kt_ref[...]), sc_ref[:, pl.ds(start, step_keys)]), _dot_t, vt_ref[...])

    @pl.when(c == pl.num_programs(1) - 1)
    def _():
        s = masked(_dot_t(qbd, kbn_ref[0]), sc_ref[:, past_len:past_len + PAGE_SIZE])
        update(s, _dot, vbn_ref[0])
        o = _scale_cols(1.0 / l_ref[0], acc_ref[0])
        dsa_ref[0] = _diag_heads(o, t_new, DSA_HEADS, DSA_HEAD_DIM).astype(BF16)


def _pages_per_step(n_pages, want):
    p = min(want, n_pages)
    while n_pages % p:
        p -= 1
    return p


def _page_specs(npp, rows, width):
    return [pl.BlockSpec((None, rows, width),
                         functools.partial(lambda i, b, c, pt: (pt[b, c * npp + i], 0, 0), i))
            for i in range(npp)]


def _smla_call(page_table, qabs, qpe, qi, wrow, ckvn, kpen, ikn, wuv, pool_ckv, pool_kpe, pool_ik,
               t_new):
    nb, n_pages = page_table.shape
    npp = _pages_per_step(n_pages, 64)
    rows = qabs.shape[1]
    past_len = n_pages * PAGE_SIZE
    per_b = lambda shape: pl.BlockSpec((1,) + shape, lambda b, c, pt: (b,) + (0,) * len(shape))
    in_specs = ([per_b((rows, MLA_KV_LORA)), per_b((rows, MLA_ROPE)), per_b((rows, IDX_DIM)),
                 per_b((rows, LANES)), per_b((PAGE_SIZE, MLA_KV_LORA)),
                 per_b((PAGE_SIZE, MLA_ROPE)), per_b((PAGE_SIZE, IDX_DIM)),
                 pl.BlockSpec(wuv.shape, lambda b, c, pt: (0, 0))]
                + _page_specs(npp, PAGE_SIZE, MLA_KV_LORA) + _page_specs(npp, MLA_ROPE, PAGE_SIZE)
                + _page_specs(npp, IDX_DIM, PAGE_SIZE))
    out_specs = [per_b((t_new, HD)),
                 pl.BlockSpec((1, t_new, npp * PAGE_SIZE), lambda b, c, pt: (b, 0, c)),
                 per_b((t_new, PAGE_SIZE))]
    out_shape = [jax.ShapeDtypeStruct((nb, t_new, HD), BF16),
                 jax.ShapeDtypeStruct((nb, t_new, past_len), F32),
                 jax.ShapeDtypeStruct((nb, t_new, PAGE_SIZE), F32)]
    grid_spec = pltpu.PrefetchScalarGridSpec(
        num_scalar_prefetch=1, grid=(nb, n_pages // npp), in_specs=in_specs, out_specs=out_specs,
        scratch_shapes=[pltpu.VMEM((1, rows, LANES), F32), pltpu.VMEM((1, rows, LANES), F32),
                        pltpu.VMEM((1, rows, MLA_KV_LORA), F32),
                        pltpu.VMEM((npp * PAGE_SIZE, MLA_KV_LORA), BF16),
                        pltpu.VMEM((MLA_ROPE, npp * PAGE_SIZE), BF16),
                        pltpu.VMEM((IDX_DIM, npp * PAGE_SIZE), BF16)])
    return pl.pallas_call(
        functools.partial(_smla_kernel, npp, t_new), grid_spec=grid_spec, out_shape=out_shape,
        compiler_params=pltpu.CompilerParams(
            dimension_semantics=("arbitrary", "arbitrary"), vmem_limit_bytes=VMEM_LIMIT),
        name="sample_mla",
    )(page_table, qabs, qpe, qi, wrow, ckvn, kpen, ikn, wuv,
      *([pool_ckv] * npp), *([pool_kpe] * npp), *([pool_ik] * npp))


def _sdsa_call(page_table, qbd, scp, scn, kbn, vbn, pool_k, pool_v, t_new):
    nb, n_pages = page_table.shape
    npp = _pages_per_step(n_pages, 32)
    rows = qbd.shape[1]
    past_len = n_pages * PAGE_SIZE
    topk = min(TOPK_MAX, (past_len + t_new) // 4)
    sel_tk = PAGE_SIZE
    while sel_tk < 2048 and past_len % (2 * sel_tk) == 0:
        sel_tk *= 2
    per_b = lambda shape: pl.BlockSpec((1,) + shape, lambda b, c, pt: (b,) + (0,) * len(shape))
    in_specs = ([per_b((rows, HD)), per_b((t_new, past_len)), per_b((t_new, PAGE_SIZE)),
                 per_b((PAGE_SIZE, HD)), per_b((PAGE_SIZE, HD))]
                + _page_specs(npp, HD, PAGE_SIZE) + _page_specs(npp, HD, PAGE_SIZE))
    grid_spec = pltpu.PrefetchScalarGridSpec(
        num_scalar_prefetch=1, grid=(nb, n_pages // npp), in_specs=in_specs,
        out_specs=[per_b((t_new, HD))],
        scratch_shapes=[pltpu.VMEM((t_new, past_len + sel_tk), F32),
                        pltpu.VMEM((t_new, 1), F32),
                        pltpu.VMEM((1, rows, LANES), F32), pltpu.VMEM((1, rows, LANES), F32),
                        pltpu.VMEM((1, rows, HD), F32),
                        pltpu.VMEM((HD, npp * PAGE_SIZE), BF16),
                        pltpu.VMEM((HD, npp * PAGE_SIZE), BF16)])
    return pl.pallas_call(
        functools.partial(_sdsa_kernel, npp, t_new, topk, past_len, sel_tk), grid_spec=grid_spec,
        out_shape=[jax.ShapeDtypeStruct((nb, t_new, HD), BF16)],
        compiler_params=pltpu.CompilerParams(
            dimension_semantics=("arbitrary", "arbitrary"), vmem_limit_bytes=VMEM_LIMIT),
        name="sample_dsa",
    )(page_table, qbd, scp, scn, kbn, vbn, *([pool_k] * npp), *([pool_v] * npp))[0]


def _pick_tile(n, candidates):
    for c in candidates:
        if n % c == 0:
            return c
    return n


def _pad_rows(a, rows):
    return jnp.pad(a, ((0, 0), (0, rows - a.shape[1]), (0, 0)))


def _head_major(a, nb, t, heads):
    w = a.shape[-1] // heads
    return a.reshape(nb, t, heads, w).transpose(0, 2, 1, 3).reshape(nb, heads * t, w)


def kernel(x_prompt, x_sample, cache_mla_ckv, cache_mla_kpe, cache_dsa_k, cache_dsa_v, cache_idx_k,
           state_ffn_conv, page_table, meta_tokens, norm1_g, w_in, g_q, g_kv, w_uq, w_uk, w_uv,
           w_br_a, w_br_b, w_o, norm2_g, w_ffn_g, w_ffn_u, ffn_conv_w, ffn_conv_b, w_ffn_d, final_g):
    depth = w_in.shape[0]
    assert depth == 1
    nb, seq, d = x_prompt.shape
    nsb, t_new, _ = x_sample.shape
    n_pages = page_table.shape[1]
    past_len = n_pages * PAGE_SIZE
    l_valid = N_META + seq
    tq, tk = 256, 256
    lp = -(-l_valid // tk) * tk
    assert l_valid % SUBLANES == 0 and t_new % SUBLANES == 0 and t_new <= PAGE_SIZE
    d_ff = w_ffn_g.shape[-1]
    l = 0

    wi = w_in[l]
    cuts = np.cumsum([MLA_Q_LORA, MLA_KV_LORA, MLA_ROPE, HD, HD, HD, HD, IDX_DIM, IDX_HEADS, d, d])
    c_q, c_kv, k_pe, q_b, k_b, v_b, q_i, k_i, w_i, g_a, g_b = jnp.split(wi, cuts[:-1], axis=1)
    pad = jnp.zeros((d, LANES - IDX_DIM - MLA_ROPE - IDX_HEADS), wi.dtype)
    win = jnp.concatenate([c_q, c_kv, q_b, k_b, v_b, q_i, g_a, g_b, k_i, k_pe, w_i, pad],
                          axis=1).astype(BF16)
    wuq = jnp.concatenate([w_uq[l][:, :, :MLA_NOPE].reshape(MLA_Q_LORA, -1),
                           w_uq[l][:, :, MLA_NOPE:].reshape(MLA_Q_LORA, -1)], axis=1).astype(BF16)
    wuk2 = w_uk[l].reshape(MLA_KV_LORA, HD)
    wuv2 = w_uv[l].reshape(MLA_KV_LORA, HD)
    wkv_p = jnp.concatenate([wuk2, wuv2], axis=1).astype(BF16)
    ukt = w_uk[l].transpose(1, 2, 0)
    eye = jnp.eye(MLA_HEADS, dtype=ukt.dtype)
    wuk_bd = (ukt[:, :, None, :] * eye[:, None, :, None]).reshape(HD, MLA_HEADS * MLA_KV_LORA)
    wuk_bd = wuk_bd.astype(BF16)
    g1 = norm1_g[l][None]
    gq = g_q[l][None]
    gkv = g_kv[l][None]
    g2 = norm2_g[l][None]
    gf = final_g[None]
    wa, wb, wo = w_br_a[l].astype(BF16), w_br_b[l].astype(BF16), w_o[l].astype(BF16)
    wg, wu, wd = w_ffn_g[l].astype(BF16), w_ffn_u[l].astype(BF16), w_ffn_d[l].astype(BF16)
    cw, cb = ffn_conv_w[l], ffn_conv_b[l][None]

    meta = jnp.broadcast_to(meta_tokens[None].astype(x_prompt.dtype), (nb, N_META, d))
    xp = jnp.concatenate([meta, x_prompt, jnp.zeros((nb, lp - l_valid, d), x_prompt.dtype)], axis=1)
    tm = _pick_tile(lp, (384, 256, 128))
    tabs_p = _rope_tables(np.arange(lp))
    (ckv_p, kb_p, vb_p, small_p, qn, qp, qb, qi, gate_p, kcat, vm, kbb, vbb, ki2) = _proj_call(
        xp, tabs_p, g1, gq, gkv, win, wuq, wkv_p, tm, sample=False, l_out=l_valid)
    attn_args = (qn, qp, qb, qi, small_p, kcat, vm, kbb, vbb, ki2, l_valid)
    full_tiles, rest_rows = divmod(l_valid, tq)
    if 0 < rest_rows <= _TAIL_TQ and full_tiles:
        mla_p, dsa_p = _attn_call(*attn_args, tq, tk, n_keep=full_tiles)
        tk_tail = _pick_tile(lp, (3 * tk, tk))
        mla_p, dsa_p = _attn_call(*attn_args, _TAIL_TQ, tk_tail, j0=full_tiles * tq // _TAIL_TQ,
                                  n_tiles=1, into=(mla_p, dsa_p))
    else:
        mla_p, dsa_p = _attn_call(*attn_args, tq, tk)
    tm_ffn = _pick_tile(lp, (768, 384, 256, 128))
    x2_p, h2_p = _merge_call(xp, mla_p, dsa_p, gate_p, wa, wb, wo, g2, tm_ffn, "merge_prompt")
    zstate = jnp.zeros((1, SUBLANES, d_ff), F32)
    y_p, tail_p = _ffn_call(h2_p, x2_p, zstate, zstate, wg, wu, wd, cw, cb, gf, tm_ffn, l_valid,
                            False, "ffn_prompt")

    ns = nsb * t_new
    xs = x_sample.reshape(1, ns, d)
    tabs_s = _rope_tables(past_len + (np.arange(ns) % t_new))
    (ckv_s, kb_s, vb_s, small_s, qlat, qp_s, qb_s, qi_s, gate_s) = _proj_call(
        xs, tabs_s, g1, gq, gkv, win, wuq, wuk_bd, ns, sample=True)
    qabs = _head_major(qlat, nsb, t_new, MLA_HEADS)
    qpe_r = _head_major(qp_s, nsb, t_new, MLA_HEADS)
    qi_r = _head_major(qi_s, nsb, t_new, IDX_HEADS)
    w_rows = small_s[0, :, _L_WI:_L_WI + IDX_HEADS].reshape(nsb, t_new, IDX_HEADS)
    w_rows = jnp.broadcast_to(w_rows.transpose(0, 2, 1).reshape(nsb, IDX_HEADS * t_new, 1),
                              (nsb, IDX_HEADS * t_new, LANES))
    qb_r = _head_major(qb_s, nsb, t_new, DSA_HEADS)
    qbd = jnp.where(_np_head_mask(t_new), jnp.tile(qb_r, (1, 1, DSA_HEADS)), jnp.zeros((), BF16))
    new_rows = lambda a: _pad_rows(a.reshape(nsb, t_new, -1), PAGE_SIZE).astype(BF16)
    ckvn = new_rows(ckv_s)
    kpen = new_rows(small_s[..., _L_KPE:_L_KPE + MLA_ROPE])
    ikn = new_rows(small_s[..., _L_KI:_L_KI + IDX_DIM])
    kbn, vbn = new_rows(kb_s), new_rows(vb_s)
    n_pool = cache_dsa_k.shape[1]
    keys_minor = lambda pool: jnp.moveaxis(pool[l], 1, -1).reshape(n_pool, -1, PAGE_SIZE)
    mla_s, scp, scn = _smla_call(page_table, qabs, qpe_r, qi_r, w_rows, ckvn, kpen, ikn,
                                 wuv2.astype(BF16), cache_mla_ckv[l], keys_minor(cache_mla_kpe),
                                 keys_minor(cache_idx_k), t_new)
    dsa_s = _sdsa_call(page_table, qbd, scp, scn, kbn, vbn,
                       keys_minor(cache_dsa_k), keys_minor(cache_dsa_v), t_new)
    x2_s, h2_s = _merge_call(xs, mla_s.reshape(1, ns, HD), dsa_s.reshape(1, ns, HD), gate_s,
                             wa, wb, wo, g2, ns, "merge_sample")
    st = state_ffn_conv[l]
    zrow = jnp.zeros((nsb, 1, d_ff), st.dtype)
    s1 = jnp.concatenate([st[:, 1:2]] + [zrow] * (t_new - 1), axis=1).reshape(1, ns, d_ff)
    s2 = jnp.concatenate([st[:, 0:1], st[:, 1:2]] + [zrow] * (t_new - 2), axis=1).reshape(1, ns, d_ff)
    y_s, tail_s = _ffn_call(h2_s, x2_s, s1, s2, wg, wu, wd, cw, cb, gf, ns, t_new, True,
                            "ffn_sample")

    y_prompt = y_p[:, N_META:l_valid]
    y_sample = y_s.reshape(nsb, t_new, d)
    cut = lambda a: a[:, :l_valid]
    new_ckv_p = ckv_p[None]
    new_kpe_p = cut(small_p)[..., _L_KPE:_L_KPE + MLA_ROPE][None]
    pos_last = lambda a: jnp.moveaxis(
        a.reshape(nb, DSA_HEADS, DSA_HEAD_DIM, l_valid), -1, 1)[None]
    new_k_p, new_v_p = pos_last(kb_p), pos_last(vb_p)
    new_ik_p = cut(small_p)[..., _L_KI:_L_KI + IDX_DIM][None]
    new_conv_p = tail_p[:, SUBLANES - (CONV_W - 1):][None]
    per_s = lambda a: a.reshape(nsb, t_new, -1)
    new_ckv_s = per_s(ckv_s)[None]
    new_kpe_s = per_s(small_s)[..., _L_KPE:_L_KPE + MLA_ROPE][None]
    new_k_s = per_s(kb_s).reshape(1, nsb, t_new, DSA_HEADS, DSA_HEAD_DIM)
    new_v_s = per_s(vb_s).reshape(1, nsb, t_new, DSA_HEADS, DSA_HEAD_DIM)
    new_ik_s = per_s(small_s)[..., _L_KI:_L_KI + IDX_DIM][None]
    new_conv_s = per_s(tail_s)[:, t_new - (CONV_W - 1):][None]
    return (y_prompt, y_sample, new_ckv_p, new_kpe_p, new_k_p, new_v_p, new_ik_p, new_conv_p,
            new_ckv_s, new_kpe_s, new_k_s, new_v_s, new_ik_s, new_conv_s)


def _np_head_mask(t_new):
    r = np.arange(DSA_HEADS * t_new)[:, None] // t_new
    c = np.arange(HD)[None, :] // DSA_HEAD_DIM
    return jnp.asarray(r == c)[None]
```

```python
import functools

import numpy as np
import jax
import jax.numpy as jnp
from jax import lax
from jax.experimental import pallas as pl
from jax.experimental.pallas import tpu as pltpu

N_META = 16
MLA_HEADS = 8
MLA_NOPE = 64
MLA_ROPE = 32
MLA_V = 64
MLA_KV_LORA = 256
MLA_Q_LORA = 768
MLA_SCALE = (MLA_NOPE + MLA_ROPE) ** -0.5
DSA_HEADS = 8
DSA_HEAD_DIM = 64
DSA_SCALE = DSA_HEAD_DIM ** -0.5
IDX_HEADS = 8
IDX_DIM = 64
IDX_SCALE = IDX_DIM ** -0.5
LOG2E = 1.4426950408889634
MLA_QSCALE = MLA_SCALE * LOG2E
DSA_QSCALE = DSA_SCALE * LOG2E
TOPK_MAX = 256
CONV_W = 3
ROPE_THETA = 10000.0
EPS = 1e-6
NEG_INF = -1e30
PAGE_SIZE = 128

LANES = 128
SUBLANES = 8
HD = 512
VMEM_LIMIT = 56 * 1024 * 1024

F32 = jnp.float32
BF16 = jnp.bfloat16

_C_Q, _C_KV, _Q_B, _K_B, _V_B, _Q_I, _G_A, _G_B, _SMALL, _D_IN_P = (
    0, 768, 1024, 1536, 2048, 2560, 3072, 4096, 5120, 5248)
_L_KI, _L_KPE, _L_WI = 0, 64, 96


def _dot(a, b):
    return jnp.dot(a, b, preferred_element_type=F32)


def _dot_t(a, b):
    return lax.dot_general(a, b, (((1,), (1,)), ((), ())), preferred_element_type=F32)


def _rms(x, g):
    ms = jnp.mean(x * x, axis=-1, keepdims=True)
    return x * lax.rsqrt(ms + EPS) * g


def _rope_tables(pos):
    pos = jnp.asarray(pos).astype(F32)[:, None]
    lane = np.arange(LANES)
    out = []
    for width in (64, 32):
        half = width // 2
        m = lane % width
        inv_freq = 1.0 / (ROPE_THETA ** (jnp.arange(half, dtype=F32) / half))
        ang = pos * inv_freq[m % half][None, :]
        c, s = jnp.cos(ang), jnp.sin(ang)
        out += [c, jnp.where(m >= half, s, 0.0), jnp.where(m < half, -s, 0.0)]
    return out


def _rope_blk(x, c, sa, sb, half):
    return x * c + pltpu.roll(x, half, 1) * sa + pltpu.roll(x, LANES - half, 1) * sb


def _proj_kernel(sample, x_ref, c64_ref, sa64_ref, sb64_ref, c32_ref, sa32_ref, sb32_ref,
                 g1_ref, gq_ref, gkv_ref, win_ref, wuq_ref, wkv_ref, *outs):
    if sample:
        (ckv_o, kb_o, vb_o, small_o, qlat_o, qp_o, qb_o, qi_o, gate_o) = outs
    else:
        (ckv_o, kb_o, vb_o, small_o, qn_o, qp_o, qb_o, qi_o, gate_o,
         kcat_o, vm_o, kbb_o, vbb_o, ki2_o) = outs
    hb = _rms(x_ref[0], g1_ref[...]).astype(BF16)
    c64, sa64, sb64 = c64_ref[...], sa64_ref[...], sb64_ref[...]
    c32, sa32, sb32 = c32_ref[...], sa32_ref[...], sb32_ref[...]

    def proj(a, b):
        return _dot(hb, win_ref[:, a:b])

    cq = _rms(proj(_C_Q, _C_KV), gq_ref[...]).astype(BF16)
    ckv = _rms(proj(_C_KV, _Q_B), gkv_ref[...])
    ckv_o[0] = ckv

    zq = proj(_Q_B, _K_B)
    zk = proj(_K_B, _V_B)
    zi = proj(_Q_I, _G_A)
    for blk in range(HD // LANES):
        sl = slice(blk * LANES, (blk + 1) * LANES)
        qb_o[0, :, sl] = (_rope_blk(zq[:, sl], c64, sa64, sb64, 32) * DSA_QSCALE).astype(BF16)
        qi_o[0, :, sl] = (_rope_blk(zi[:, sl], c64, sa64, sb64, 32) * IDX_SCALE).astype(BF16)
        kr = _rope_blk(zk[:, sl], c64, sa64, sb64, 32)
        if sample:
            kb_o[0, :, sl] = kr
        else:
            kb_o[0, sl, :] = kr.T
            kbb_o[0, :, sl] = kr.astype(BF16)
    zv = proj(_V_B, _Q_I)
    if sample:
        vb_o[0] = zv
    else:
        for blk in range(HD // LANES):
            sl = slice(blk * LANES, (blk + 1) * LANES)
            zvt = zv[:, sl].T
            vb_o[0, sl, :] = zvt
            vbb_o[0, sl, :] = zvt.astype(BF16)

    gate_o[0] = jax.nn.sigmoid(proj(_G_A, _SMALL)).astype(BF16)

    zs = proj(_SMALL, _D_IN_P)
    lane = lax.broadcasted_iota(jnp.int32, (1, LANES), 1)
    m_ki = (lane < _L_KPE).astype(F32)
    m_kpe = ((lane >= _L_KPE) & (lane < _L_WI)).astype(F32)
    m_wi = ((lane >= _L_WI) & (lane < _L_WI + IDX_HEADS)).astype(F32)
    small = (zs * (c64 * m_ki + c32 * m_kpe + (IDX_HEADS ** -0.5) * m_wi)
             + pltpu.roll(zs, 32, 1) * (sa64 * m_ki) + pltpu.roll(zs, 96, 1) * (sb64 * m_ki)
             + pltpu.roll(zs, 16, 1) * (sa32 * m_kpe) + pltpu.roll(zs, 112, 1) * (sb32 * m_kpe))
    small_o[0] = small
    if not sample:
        ki = small * m_ki
        ki2_o[0] = (ki + pltpu.roll(ki, 64, 1)).astype(BF16)
        kp = pltpu.roll(small * m_kpe, 64, 1)
        kpe4 = (kp + pltpu.roll(kp, 32, 1) + pltpu.roll(kp, 64, 1)
                + pltpu.roll(kp, 96, 1)).astype(BF16)
        for pair in range(MLA_HEADS // 2):
            kcat_o[0, :, (2 * pair + 1) * LANES:(2 * pair + 2) * LANES] = kpe4

    q = _dot(cq, wuq_ref[...])
    qn = (q[:, :HD] * MLA_QSCALE).astype(BF16)
    for blk in range(2):
        qpe = q[:, HD + blk * LANES:HD + (blk + 1) * LANES]
        qp_o[0, :, blk * LANES:(blk + 1) * LANES] = (
            _rope_blk(qpe, c32, sa32, sb32, 16) * MLA_QSCALE).astype(BF16)

    if sample:
        qlat_o[0] = _dot(qn, wkv_ref[...]).astype(BF16)
    else:
        qn_o[0] = qn
        kv = _dot(ckv.astype(BF16), wkv_ref[...])
        for pair in range(MLA_HEADS // 2):
            kcat_o[0, :, 2 * pair * LANES:(2 * pair + 1) * LANES] = (
                kv[:, pair * LANES:(pair + 1) * LANES].astype(BF16))
        for blk in range(HD // LANES):
            vm_o[0, blk * LANES:(blk + 1) * LANES, :] = (
                kv[:, HD + blk * LANES:HD + (blk + 1) * LANES].T.astype(BF16))


def _const_spec(shape):
    nd = len(shape)
    return pl.BlockSpec(shape, lambda *_: (0,) * nd, pipeline_mode=pl.Buffered(1))


def _proj_call(x, tables, g1, gq, gkv, win, wuq, wkv, tm, sample, l_out=None):
    nb, lp, d = x.shape
    l_out = lp if l_out is None else l_out
    grid = (nb, lp // tm)
    row = lambda w: pl.BlockSpec((1, tm, w), lambda b, j: (b, j, 0))
    tab = pl.BlockSpec((tm, LANES), lambda b, j: (j, 0))
    in_specs = ([row(d)] + [tab] * 6
                + [_const_spec(a.shape) for a in (g1, gq, gkv, win, wuq, wkv)])
    f32_w = [MLA_KV_LORA, HD, HD, LANES]
    if sample:
        bf_w = [MLA_HEADS * MLA_KV_LORA, 2 * LANES, HD, HD, 2048]
    else:
        bf_w = [HD, 2 * LANES, HD, HD, 2048, 2 * HD, HD, HD, HD, LANES]
    out_shape = ([jax.ShapeDtypeStruct((nb, lp, w), F32) for w in f32_w]
                 + [jax.ShapeDtypeStruct((nb, lp, w), BF16) for w in bf_w])
    out_specs = [row(w) for w in f32_w + bf_w]
    out_shape[0] = jax.ShapeDtypeStruct((nb, l_out, MLA_KV_LORA), F32)
    if not sample:
        n_f32 = len(f32_w)
        for i, dt, rows in ((1, F32, l_out), (2, F32, l_out), (n_f32 + 6, BF16, lp),
                            (n_f32 + 8, BF16, lp)):
            out_shape[i] = jax.ShapeDtypeStruct((nb, HD, rows), dt)
            out_specs[i] = pl.BlockSpec((1, HD, tm), lambda b, j: (b, 0, j))
    return pl.pallas_call(
        functools.partial(_proj_kernel, sample),
        grid=grid, in_specs=in_specs, out_specs=out_specs, out_shape=out_shape,
        compiler_params=pltpu.CompilerParams(
            dimension_semantics=("arbitrary", "arbitrary"), vmem_limit_bytes=VMEM_LIMIT),
        name="proj_sample" if sample else "proj_prompt",
    )(x, *tables, g1, gq, gkv, win, wuq, wkv)


_MAX_BISECT = 320
_BISECT_UNROLL = 4


def _chunk_ds(c, tk):
    start = c * tk
    return pl.ds(start if isinstance(start, int) else pl.multiple_of(start, tk), tk)


def _key_chunk(ref, c, tk, axis):
    ds = _chunk_ds(c, tk)
    return ref[ds, :] if axis == 0 else ref[:, ds]


def _key_fold(v, op, axis):
    if axis == 0:
        parts = [v[i * SUBLANES:(i + 1) * SUBLANES] for i in range(v.shape[0] // SUBLANES)]
    else:
        parts = [v[:, i * LANES:(i + 1) * LANES] for i in range(v.shape[1] // LANES)]
    while len(parts) > 1:
        parts = [op(a, b) for a, b in zip(parts[::2], parts[1::2])] + (
            [parts[-1]] if len(parts) % 2 else [])
    return parts[0]


def _key_index(c, tk, axis):
    shape = (tk, 1) if axis == 0 else (1, tk)
    return (lax.convert_element_type(c * tk, F32)
            + lax.broadcasted_iota(jnp.int32, shape, axis).astype(F32))


def _count(sc_ref, nck, tk, axis, pred):
    nq = sc_ref.shape[1 - axis]
    part = (SUBLANES, nq) if axis == 0 else (nq, LANES)

    def body(c, acc):
        v = jnp.where(pred(_key_chunk(sc_ref, c, tk, axis), c), 1.0, 0.0)
        return acc + _key_fold(v, jnp.add, axis)

    acc = lax.fori_loop(0, nck, body, jnp.zeros(part, F32))
    return jnp.sum(acc, axis=axis, keepdims=True)


def _stats_init(part):
    return (jnp.full(part, -NEG_INF, F32), jnp.full(part, NEG_INF, F32),
            jnp.zeros(part, F32), jnp.zeros(part, F32))


def _stats_update(stats, x, axis, has_masked):
    mn, mx, gt0, ge0 = stats
    xv = jnp.where(x > 0.5 * NEG_INF, x, -NEG_INF) if has_masked else x
    return (jnp.minimum(mn, _key_fold(xv, jnp.minimum, axis)),
            jnp.maximum(mx, _key_fold(x, jnp.maximum, axis)),
            gt0 + _key_fold(jnp.where(x > 0.0, 1.0, 0.0), jnp.add, axis),
            ge0 + _key_fold(jnp.where(x >= 0.0, 1.0, 0.0), jnp.add, axis))


def _select_threshold(sc_ref, nck, tk, k_eff, n_valid, axis, stats=None):
    nq = sc_ref.shape[1 - axis]
    part = (SUBLANES, nq) if axis == 0 else (nq, LANES)

    if stats is None:
        stats = lax.fori_loop(
            0, nck, lambda c, st: _stats_update(st, _key_chunk(sc_ref, c, tk, axis), axis, True),
            _stats_init(part))
    mn, mx, gt0, ge0 = stats
    lo0 = jnp.min(mn, axis=axis, keepdims=True)
    hi0 = jnp.max(mx, axis=axis, keepdims=True)
    hi0 = hi0 + jnp.maximum(jnp.abs(hi0), 1.0) * 1e-6

    def count_ge(thr):
        return _count(sc_ref, nck, tk, axis, lambda x, c: x >= thr)

    def all_rows(fin):
        return jnp.min(jnp.where(fin, 1.0, 0.0)) > 0.5

    c_gt0 = jnp.sum(gt0, axis=axis, keepdims=True)
    c_ge0 = jnp.sum(ge0, axis=axis, keepdims=True)
    at_zero = (c_gt0 < k_eff) & (c_ge0 >= k_eff)
    above = c_gt0 >= k_eff
    inside = (lo0 < 0.0) & (hi0 > 0.0)
    lo_at_zero = at_zero | (above & inside)
    c_lo0 = jnp.where(lo_at_zero, c_ge0, n_valid)
    lo0, hi0 = (jnp.where(lo_at_zero, 0.0, lo0),
                jnp.where(at_zero | (~above & inside), 0.0, hi0))

    def cond(st):
        it, _, _, _, done = st
        return jnp.logical_and(it < _MAX_BISECT, jnp.logical_not(done))

    def body(st):
        it, lo, hi, c_lo, _ = st
        for _ in range(_BISECT_UNROLL):
            mid = 0.5 * lo + 0.5 * hi
            c = count_ge(mid)
            ge = c >= k_eff
            fin = (mid <= lo) | (mid >= hi)
            lo, hi, c_lo = jnp.where(ge, mid, lo), jnp.where(ge, hi, mid), jnp.where(ge, c, c_lo)
            fin = fin | (c_lo == k_eff)
        return it + _BISECT_UNROLL, lo, hi, c_lo, all_rows(fin)

    fin0 = (c_lo0 == k_eff) | (lo0 >= hi0)
    _, t, _, c_t, _ = lax.while_loop(
        cond, body, (jnp.int32(0), lo0, hi0, c_lo0, all_rows(fin0)))

    tie_rows = c_t > k_eff

    @pl.when(jnp.max(jnp.where(tie_rows, 1.0, 0.0)) > 0.5)
    def _():
        n_keys = sc_ref.shape[axis]
        need = k_eff - _count(sc_ref, nck, tk, axis, lambda x, c: x > t)

        def tie_count(m):
            return _count(sc_ref, nck, tk, axis,
                          lambda x, c: (x == t) & (_key_index(c, tk, axis) <= m))

        def ibody(_, st):
            lo_i, hi_i = st
            mid = jnp.floor(0.5 * (lo_i + hi_i))
            ok = tie_count(mid) >= need
            return jnp.where(ok, lo_i, mid), jnp.where(ok, mid, hi_i)

        steps = int(np.ceil(np.log2(n_keys))) + 1
        _, m_idx = lax.fori_loop(0, steps, ibody,
                                 (jnp.full(t.shape, -1.0, F32),
                                  jnp.full(t.shape, float(n_keys - 1), F32)))

        def fix(c, carry):
            x = _key_chunk(sc_ref, c, tk, axis)
            drop = (x == t) & (_key_index(c, tk, axis) > m_idx) & tie_rows
            ds = _chunk_ds(c, tk)
            if axis == 0:
                sc_ref[ds, :] = jnp.where(drop, NEG_INF, x)
            else:
                sc_ref[:, ds] = jnp.where(drop, NEG_INF, x)
            return carry

        lax.fori_loop(0, nck, fix, 0)

    return t


def _lane_blocks(s):
    return [s[:, i * LANES:(i + 1) * LANES] for i in range(s.shape[1] // LANES)]


def _scale_cols(alpha, x):
    return jnp.concatenate([alpha * b for b in _lane_blocks(x)], axis=1)


def _online_update(m_ref, l_ref, acc_ref, idx, blocks, pv):
    m = m_ref[idx]
    bm = blocks[0]
    for b in blocks[1:]:
        bm = jnp.maximum(bm, b)
    m_new = jnp.maximum(m, jnp.max(bm, axis=1, keepdims=True))
    alpha = jnp.exp2(m - m_new)
    ps = [jnp.exp2(b - m_new) for b in blocks]
    rs = ps[0]
    for p in ps[1:]:
        rs = rs + p
    l_ref[idx] = alpha * l_ref[idx] + jnp.sum(rs, axis=1, keepdims=True)
    acc_ref[idx] = _scale_cols(alpha, acc_ref[idx]) + pv([p.astype(BF16) for p in ps])
    m_ref[idx] = m_new


def _reset_state(m_ref, l_ref, acc_ref):
    m_ref[...] = jnp.full_like(m_ref, NEG_INF)
    l_ref[...] = jnp.zeros_like(l_ref)
    acc_ref[...] = jnp.zeros_like(acc_ref)


def _softmax_keys_major(m_ref, l_ref, idx, s):
    m = m_ref[idx]
    m_new = jnp.maximum(m, jnp.max(_key_fold(s, jnp.maximum, 0), axis=0, keepdims=True))
    alpha = jnp.exp2(m - m_new)
    p = jnp.exp2(s - m_new)
    l_ref[idx] = alpha * l_ref[idx] + jnp.sum(_key_fold(p, jnp.add, 0), axis=0, keepdims=True)
    m_ref[idx] = m_new
    return alpha, p.astype(BF16)


def _attn_kernel(tq, tk, j0, n_keep, topk, n_alias,
                 qn_ref, qp_ref, qb_ref, qi_ref, small_ref,
                 kcat_ref, vm_ref, kb_ref, vb_ref, ki2_ref, *rest):
    (mla_ref, dsa_ref, sc_ref, qc_ref, qd_ref, qx_ref, w_ref, st_ref, m_ref, l_ref,
     acc_ref) = rest[n_alias:]
    j = pl.program_id(1) + j0
    nh = MLA_HEADS
    n_pairs = nh // 2

    @pl.when(j >= n_keep)
    def _():
        mla_ref[...] = jnp.zeros_like(mla_ref)
        dsa_ref[...] = jnp.zeros_like(dsa_ref)

    @pl.when(j < n_keep)
    def _():
        nck = (j * tq + tq - 1) // tk + 1
        last = nck - 1
        lane = lax.broadcasted_iota(jnp.int32, (1, LANES), 1)
        q_row = j * tq + lax.broadcasted_iota(jnp.int32, (1, tq), 1)
        k_row = lax.broadcasted_iota(jnp.int32, (tk, 1), 0)
        zero_b = jnp.zeros((tq, LANES), BF16)

        def ksl(c):
            return _chunk_ds(c, tk)

        def pair_sl(pair):
            return slice(pair * LANES, (pair + 1) * LANES)

        def two(x):
            return jnp.concatenate([x, x], axis=1)

        def write_heads(out_ref):
            half = LANES // 2
            for pair in range(n_pairs):
                o = acc_ref[pair] / l_ref[pair]
                o = jnp.concatenate([o[:half, :tq], o[half:, tq:]], axis=0)
                out_ref[0, :, pair_sl(pair)] = o.T.astype(BF16)

        small = small_ref[0]
        for h in range(nh):
            pair, sub = divmod(h, 2)
            grp, gsub = divmod(h, 4)
            rows = slice(sub * tq, (sub + 1) * tq)
            own = (lane >= 64) == bool(sub)
            qc_ref[pair, rows, :LANES] = jnp.where(own, qn_ref[0, :, pair_sl(pair)], zero_b)
            qc_ref[pair, rows, LANES:] = jnp.where((lane // 32) == gsub,
                                                   qp_ref[0, :, pair_sl(grp)], zero_b)
            qd_ref[pair, rows, :] = jnp.where(own, qb_ref[0, :, pair_sl(pair)], zero_b)
            qx_ref[h * tq:(h + 1) * tq, :] = jnp.where(own, qi_ref[0, :, pair_sl(pair)], zero_b)
        w_ref[...] = small.T[_L_WI:_L_WI + IDX_HEADS]

        def attend(c, score_fn, fix_fn, vt_ref):
            ks = ksl(c)
            s = {p: score_fn(p, ks) for p in range(min(_SCORE_AHEAD, n_pairs))}
            for p in range(n_pairs):
                alpha, pr = _softmax_keys_major(m_ref, l_ref, p, fix_fn(s.pop(p)))
                if p + _SCORE_AHEAD < n_pairs:
                    s[p + _SCORE_AHEAD] = score_fn(p + _SCORE_AHEAD, ks)
                acc_ref[p] = alpha * acc_ref[p] + _dot(vt_ref[0, pair_sl(p), ks], pr)

        _reset_state(m_ref, l_ref, acc_ref)
        for i, part in enumerate(_stats_init((SUBLANES, tq))):
            st_ref[i] = part

        def mla_scores(p, ks):
            return _dot_t(kcat_ref[0, ks, 2 * p * LANES:(2 * p + 2) * LANES], qc_ref[p])

        def mla_idx_chunk(c, masked):
            ks = ksl(c)
            r = _dot_t(ki2_ref[0, ks, :], qx_ref[...])
            sc = None
            for h in range(IDX_HEADS):
                term = jnp.maximum(r[:, h * tq:(h + 1) * tq], 0.0) * w_ref[h:h + 1, :]
                sc = term if sc is None else sc + term
            if masked:
                sc = jnp.where((c * tk + k_row) <= q_row, sc, NEG_INF)
                vis2 = (c * tk + k_row) <= two(q_row)
                fix = lambda s: jnp.where(vis2, s, NEG_INF)
            else:
                fix = lambda s: s
            sc_ref[ks, :] = sc
            for i, part in enumerate(_stats_update(tuple(st_ref[i] for i in range(4)), sc, 0,
                                                   masked)):
                st_ref[i] = part
            attend(c, mla_scores, fix, vm_ref)

        def mla_idx_step(i, carry):
            mla_idx_chunk(2 * i, False)
            mla_idx_chunk(2 * i + 1, False)
            return carry

        lax.fori_loop(0, last // 2, mla_idx_step, 0)

        @pl.when(last % 2 == 1)
        def _():
            mla_idx_chunk(last - 1, False)

        mla_idx_chunk(last, True)
        write_heads(mla_ref)

        n_valid = (q_row + 1).astype(F32)
        k_eff = jnp.minimum(n_valid, float(topk))
        t2 = two(_select_threshold(sc_ref, nck, tk, k_eff, n_valid, 0,
                                   tuple(st_ref[i] for i in range(4))))

        _reset_state(m_ref, l_ref, acc_ref)

        def dsa_scores(p, ks):
            return _dot_t(kb_ref[0, ks, pair_sl(p)], qd_ref[p])

        def dsa_chunk(c):
            sel = two(sc_ref[ksl(c), :]) >= t2
            attend(c, dsa_scores, lambda s: jnp.where(sel, s, NEG_INF), vb_ref)

        def dsa_step(i, carry):
            dsa_chunk(2 * i)
            dsa_chunk(2 * i + 1)
            return carry

        lax.fori_loop(0, nck // 2, dsa_step, 0)

        @pl.when(nck % 2 == 1)
        def _():
            dsa_chunk(nck - 1)

        write_heads(dsa_ref)


_TAIL_TQ = 64
_SCORE_AHEAD = 4


def _attn_call(qn, qp, qb, qi, small, kcat, vm, kbb, vbb, ki2, l_valid, tq, tk,
               j0=0, n_tiles=None, n_keep=None, into=()):
    nb, lp, _ = qn.shape
    n_tiles = lp // tq if n_tiles is None else n_tiles
    n_keep = -(-l_valid // tq) if n_keep is None else n_keep
    topk = min(TOPK_MAX, l_valid // 4)
    n_pairs = MLA_HEADS // 2
    qrow = lambda w: pl.BlockSpec((1, tq, w), lambda b, j: (b, j + j0, 0))
    krow = lambda w: pl.BlockSpec((1, lp, w), lambda b, j: (b, 0, 0))
    vcol = pl.BlockSpec((1, HD, lp), lambda b, j: (b, 0, 0))
    in_specs = [qrow(HD), qrow(2 * LANES), qrow(HD), qrow(HD), qrow(LANES),
                krow(2 * HD), vcol, krow(HD), vcol, krow(LANES)] + [qrow(HD)] * len(into)
    n_in = 10
    return pl.pallas_call(
        functools.partial(_attn_kernel, tq, tk, j0, n_keep, topk, len(into)),
        grid=(nb, n_tiles), in_specs=in_specs,
        out_specs=[qrow(HD), qrow(HD)],
        out_shape=[jax.ShapeDtypeStruct((nb, lp, HD), BF16)] * 2,
        input_output_aliases={n_in + i: i for i in range(len(into))},
        scratch_shapes=[pltpu.VMEM((lp, tq), F32),
                        pltpu.VMEM((n_pairs, 2 * tq, 2 * LANES), BF16),
                        pltpu.VMEM((n_pairs, 2 * tq, LANES), BF16),
                        pltpu.VMEM((IDX_HEADS * tq, LANES), BF16),
                        pltpu.VMEM((IDX_HEADS, tq), F32),
                        pltpu.VMEM((4, SUBLANES, tq), F32),
                        pltpu.VMEM((n_pairs, 1, 2 * tq), F32),
                        pltpu.VMEM((n_pairs, 1, 2 * tq), F32),
                        pltpu.VMEM((n_pairs, LANES, 2 * tq), F32)],
        compiler_params=pltpu.CompilerParams(
            dimension_semantics=("arbitrary", "arbitrary"), vmem_limit_bytes=VMEM_LIMIT),
        name="attn_prompt" if not into else "attn_prompt_tail",
    )(qn, qp, qb, qi, small, kcat, vm, kbb, vbb, ki2, *into)


def _merge_kernel(x_ref, mla_ref, dsa_ref, gate_ref, wa_ref, wb_ref, wo_ref, g2_ref,
                  x2_ref, h2_ref):
    d = x_ref.shape[-1]
    a = _dot(mla_ref[0], wa_ref[...])
    b = _dot(dsa_ref[0], wb_ref[...])
    g = gate_ref[0]
    o = g[:, :d].astype(F32) * a + g[:, d:].astype(F32) * b
    x2 = x_ref[0] + _dot(o.astype(BF16), wo_ref[...])
    x2_ref[0] = x2
    h2_ref[0] = _rms(x2, g2_ref[...]).astype(BF16)


def _merge_call(x, mla, dsa, gate, wa, wb, wo, g2, tm, name):
    nb, lp, d = x.shape
    row = lambda w: pl.BlockSpec((1, tm, w), lambda b, j: (b, j, 0))
    return pl.pallas_call(
        _merge_kernel, grid=(nb, lp // tm),
        in_specs=[row(d), row(HD), row(HD), row(2 * d)]
        + [_const_spec(a.shape) for a in (wa, wb, wo, g2)],
        out_specs=[row(d), row(d)],
        out_shape=[jax.ShapeDtypeStruct((nb, lp, d), F32), jax.ShapeDtypeStruct((nb, lp, d), BF16)],
        compiler_params=pltpu.CompilerParams(
            dimension_semantics=("arbitrary", "arbitrary"), vmem_limit_bytes=VMEM_LIMIT),
        name=name,
    )(x, mla, dsa, gate, wa, wb, wo, g2)


_FF_CHUNK = 1024
_FF_AHEAD = 1


def _ffn_kernel(seq_rows, tail_tile, tail_off, h_ref, x_ref, s1_ref, s2_ref,
                wg_ref, wu_ref, wd_ref, cw_ref, cb_ref, gf_ref, y_ref, tail_ref, prev_ref):
    j = pl.program_id(1)
    tm = h_ref.shape[1]
    d_ff = wg_ref.shape[1]
    h = h_ref[0]
    row = lax.broadcasted_iota(jnp.int32, (tm, 1), 0)
    if seq_rows is None:
        @pl.when(j == 0)
        def _():
            prev_ref[...] = jnp.zeros_like(prev_ref)
        first1, first2 = row < 1, row < 2
    else:
        first1, first2 = (row % seq_rows) < 1, (row % seq_rows) < 2

    acc = jnp.zeros((tm, x_ref.shape[-1]), F32)
    chunks = [slice(c0, min(c0 + _FF_CHUNK, d_ff)) for c0 in range(0, d_ff, _FF_CHUNK)]

    def up(sl):
        return _dot(h, wg_ref[:, sl]), _dot(h, wu_ref[:, sl])

    ahead = [up(sl) for sl in chunks[:_FF_AHEAD]]
    for i, sl in enumerate(chunks):
        g, u = ahead.pop(0)
        if i + _FF_AHEAD < len(chunks):
            ahead.append(up(chunks[i + _FF_AHEAD]))
        if seq_rows is None:
            p = prev_ref[:, sl]
            hist1 = jnp.broadcast_to(p[SUBLANES - 1:SUBLANES], g.shape)
            hist2 = jnp.where(row < 1, jnp.broadcast_to(p[SUBLANES - 2:SUBLANES - 1], g.shape),
                              hist1)
            prev_ref[:, sl] = g[tm - SUBLANES:]
        else:
            hist1, hist2 = s1_ref[0, :, sl], s2_ref[0, :, sl]
        g1 = jnp.where(first1, hist1, pltpu.roll(g, 1, 0))
        g2 = jnp.where(first2, hist2, pltpu.roll(g, 2, 0))
        cw = cw_ref[:, sl]
        gc = cb_ref[:, sl] + cw[0:1] * g2 + cw[1:2] * g1 + cw[2:3] * g
        act = (gc * jax.nn.sigmoid(gc) * u).astype(BF16)
        acc = acc + _dot(act, wd_ref[sl, :])
        if seq_rows is None:
            @pl.when(j == tail_tile)
            def _(g=g, sl=sl):
                tail_ref[0, :, sl] = g[tail_off:tail_off + SUBLANES]
        else:
            tail_ref[0, :, sl] = g
    y_ref[0] = _rms(x_ref[0] + acc, gf_ref[...])


def _ffn_call(h2, x2, s1, s2, wg, wu, wd, cw, cb, gf, tm, l_valid, sample, name):
    nb, lp, d = x2.shape
    d_ff = wg.shape[1]
    row = lambda w: pl.BlockSpec((1, tm, w), lambda b, j: (b, j, 0))
    if sample:
        seq_rows, tail_tile, tail_off = l_valid, 0, 0
        tail_shape, tail_spec = (nb, lp, d_ff), row(d_ff)
        s_spec = row(d_ff)
    else:
        seq_rows = None
        tail_tile, tail_off = divmod(l_valid - SUBLANES, tm)
        tail_shape = (nb, SUBLANES, d_ff)
        tail_spec = pl.BlockSpec((1, SUBLANES, d_ff), lambda b, j: (b, 0, 0))
        s_spec = pl.BlockSpec((1, SUBLANES, d_ff), lambda b, j: (0, 0, 0))
    return pl.pallas_call(
        functools.partial(_ffn_kernel, seq_rows, tail_tile, tail_off),
        grid=(nb, lp // tm),
        in_specs=[row(d), row(d), s_spec, s_spec]
        + [_const_spec(a.shape) for a in (wg, wu, wd, cw, cb, gf)],
        out_specs=[row(d), tail_spec],
        out_shape=[jax.ShapeDtypeStruct((nb, lp, d), F32), jax.ShapeDtypeStruct(tail_shape, F32)],
        scratch_shapes=[pltpu.VMEM((SUBLANES, d_ff), F32)],
        compiler_params=pltpu.CompilerParams(
            dimension_semantics=("arbitrary", "arbitrary"), vmem_limit_bytes=VMEM_LIMIT),
        name=name,
    )(h2, x2, s1, s2, wg, wu, wd, cw, cb, gf)


def _diag_heads(full, t, n_heads, width):
    col_head = lax.broadcasted_iota(jnp.int32, (1, n_heads * width), 1) // width
    out = jnp.zeros((t, n_heads * width), F32)
    for h in range(n_heads):
        out = jnp.where(col_head == h, full[h * t:(h + 1) * t], out)
    return out


def _smla_kernel(npp, t_new, pt_ref, qabs_ref, qpe_ref, qi_ref, wrow_ref,
                 ckvn_ref, kpen_ref, ikn_ref, wuv_ref, *rest):
    ckv_pages = rest[:npp]
    kpe_pages = rest[npp:2 * npp]
    ik_pages = rest[2 * npp:3 * npp]
    mla_ref, scp_ref, scn_ref, m_ref, l_ref, acc_ref, ck_ref, kpe_ref, ik_ref = rest[3 * npp:]
    c = pl.program_id(1)
    rows = qabs_ref.shape[1]
    qabs, qpe, qi, wrow = qabs_ref[0], qpe_ref[0], qi_ref[0], wrow_ref[0]

    @pl.when(c == 0)
    def _():
        _reset_state(m_ref, l_ref, acc_ref)

    def update(s, vals):
        _online_update(m_ref, l_ref, acc_ref, 0, _lane_blocks(s),
                       lambda ps: _dot(jnp.concatenate(ps, axis=1), vals))

    def idx_score(s):
        s = _scale_cols(wrow, jnp.maximum(s, 0.0))
        out = s[:t_new]
        for h in range(1, IDX_HEADS):
            out = out + s[h * t_new:(h + 1) * t_new]
        return out

    for i in range(npp):
        ck_ref[i * PAGE_SIZE:(i + 1) * PAGE_SIZE, :] = ckv_pages[i][...].astype(BF16)
        kpe_ref[:, i * PAGE_SIZE:(i + 1) * PAGE_SIZE] = kpe_pages[i][...].astype(BF16)
        ik_ref[:, i * PAGE_SIZE:(i + 1) * PAGE_SIZE] = ik_pages[i][...].astype(BF16)
    ck_all = ck_ref[...]
    update(_dot_t(qabs, ck_all) + _dot(qpe, kpe_ref[...]), ck_all)
    scp_ref[0] = idx_score(_dot(qi, ik_ref[...]))

    @pl.when(c == pl.num_programs(1) - 1)
    def _():
        ck = ckvn_ref[0]
        tok = lax.broadcasted_iota(jnp.int32, (rows, 1), 0) % t_new
        key = lax.broadcasted_iota(jnp.int32, (1, PAGE_SIZE), 1)
        vis = key <= tok
        s = _dot_t(qabs, ck) + _dot_t(qpe, kpen_ref[0])
        update(jnp.where(vis, s, NEG_INF), ck)
        scn_ref[0] = jnp.where(vis[:t_new], idx_score(_dot_t(qi, ikn_ref[0])), NEG_INF)
        o_lat = _scale_cols(1.0 / l_ref[0], acc_ref[0]).astype(BF16)
        mla_ref[0] = _diag_heads(_dot(o_lat, wuv_ref[...]), t_new, MLA_HEADS, MLA_V).astype(BF16)


def _sdsa_kernel(npp, t_new, topk, past_len, sel_tk, pt_ref, qbd_ref, scp_ref, scn_ref, kbn_ref,
                 vbn_ref, *rest):
    k_pages = rest[:npp]
    v_pages = rest[npp:2 * npp]
    dsa_ref, sc_ref, t_ref, m_ref, l_ref, acc_ref, kt_ref, vt_ref = rest[2 * npp:]
    c = pl.program_id(1)
    rows = qbd_ref.shape[1]
    qbd = qbd_ref[0]
    step_keys = npp * PAGE_SIZE

    @pl.when(c == 0)
    def _():
        _reset_state(m_ref, l_ref, acc_ref)
        sc_ref[:, :past_len] = scp_ref[0]
        sc_ref[:, past_len:past_len + PAGE_SIZE] = scn_ref[0]
        if sel_tk > PAGE_SIZE:
            sc_ref[:, past_len + PAGE_SIZE:] = jnp.full((t_new, sel_tk - PAGE_SIZE), NEG_INF, F32)
        tok = lax.broadcasted_iota(jnp.int32, (t_new, 1), 0)
        n_valid = (past_len + 1 + tok).astype(F32)
        k_eff = jnp.minimum(n_valid, float(topk))
        t_ref[...] = _select_threshold(sc_ref, (past_len + sel_tk) // sel_tk, sel_tk,
                                       k_eff, n_valid, 1)

    t = t_ref[...]

    def masked(s, sc):
        pen = jnp.where(sc >= t, 0.0, NEG_INF)
        sel = jnp.concatenate([pen] * DSA_HEADS, axis=0) == 0.0
        return jnp.where(sel, s, NEG_INF)

    def update(s, pv_dot, vals):
        _online_update(m_ref, l_ref, acc_ref, 0, _lane_blocks(s),
                       lambda ps: pv_dot(jnp.concatenate(ps, axis=1), vals))

    for i in range(npp):
        kt_ref[:, i * PAGE_SIZE:(i + 1) * PAGE_SIZE] = k_pages[i][...].astype(BF16)
        vt_ref[:, i * PAGE_SIZE:(i + 1) * PAGE_SIZE] = v_pages[i][...].astype(BF16)
    start = c * step_keys
    start = start if isinstance(start, int) else pl.multiple_of(start, step_keys)
    update(masked(_dot(qbd, kt_ref[...]), sc_ref[:, pl.ds(start, step_keys)]), _dot_t, vt_ref[...])

    @pl.when(c == pl.num_programs(1) - 1)
    def _():
        s = masked(_dot_t(qbd, kbn_ref[0]), sc_ref[:, past_len:past_len + PAGE_SIZE])
        update(s, _dot, vbn_ref[0])
        o = _scale_cols(1.0 / l_ref[0], acc_ref[0])
        dsa_ref[0] = _diag_heads(o, t_new, DSA_HEADS, DSA_HEAD_DIM).astype(BF16)


def _pages_per_step(n_pages, want):
    p = min(want, n_pages)
    while n_pages % p:
        p -= 1
    return p


def _page_specs(npp, rows, width):
    return [pl.BlockSpec((None, rows, width),
                         functools.partial(lambda i, b, c, pt: (pt[b, c * npp + i], 0, 0), i))
            for i in range(npp)]


def _smla_call(page_table, qabs, qpe, qi, wrow, ckvn, kpen, ikn, wuv, pool_ckv, pool_kpe, pool_ik,
               t_new):
    nb, n_pages = page_table.shape
    npp = _pages_per_step(n_pages, 64)
    rows = qabs.shape[1]
    past_len = n_pages * PAGE_SIZE
    per_b = lambda shape: pl.BlockSpec((1,) + shape, lambda b, c, pt: (b,) + (0,) * len(shape))
    in_specs = ([per_b((rows, MLA_KV_LORA)), per_b((rows, MLA_ROPE)), per_b((rows, IDX_DIM)),
                 per_b((rows, LANES)), per_b((PAGE_SIZE, MLA_KV_LORA)),
                 per_b((PAGE_SIZE, MLA_ROPE)), per_b((PAGE_SIZE, IDX_DIM)),
                 pl.BlockSpec(wuv.shape, lambda b, c, pt: (0, 0))]
                + _page_specs(npp, PAGE_SIZE, MLA_KV_LORA) + _page_specs(npp, MLA_ROPE, PAGE_SIZE)
                + _page_specs(npp, IDX_DIM, PAGE_SIZE))
    out_specs = [per_b((t_new, HD)),
                 pl.BlockSpec((1, t_new, npp * PAGE_SIZE), lambda b, c, pt: (b, 0, c)),
                 per_b((t_new, PAGE_SIZE))]
    out_shape = [jax.ShapeDtypeStruct((nb, t_new, HD), BF16),
                 jax.ShapeDtypeStruct((nb, t_new, past_len), F32),
                 jax.ShapeDtypeStruct((nb, t_new, PAGE_SIZE), F32)]
    grid_spec = pltpu.PrefetchScalarGridSpec(
        num_scalar_prefetch=1, grid=(nb, n_pages // npp), in_specs=in_specs, out_specs=out_specs,
        scratch_shapes=[pltpu.VMEM((1, rows, LANES), F32), pltpu.VMEM((1, rows, LANES), F32),
                        pltpu.VMEM((1, rows, MLA_KV_LORA), F32),
                        pltpu.VMEM((npp * PAGE_SIZE, MLA_KV_LORA), BF16),
                        pltpu.VMEM((MLA_ROPE, npp * PAGE_SIZE), BF16),
                        pltpu.VMEM((IDX_DIM, npp * PAGE_SIZE), BF16)])
    return pl.pallas_call(
        functools.partial(_smla_kernel, npp, t_new), grid_spec=grid_spec, out_shape=out_shape,
        compiler_params=pltpu.CompilerParams(
            dimension_semantics=("arbitrary", "arbitrary"), vmem_limit_bytes=VMEM_LIMIT),
        name="sample_mla",
    )(page_table, qabs, qpe, qi, wrow, ckvn, kpen, ikn, wuv,
      *([pool_ckv] * npp), *([pool_kpe] * npp), *([pool_ik] * npp))


def _sdsa_call(page_table, qbd, scp, scn, kbn, vbn, pool_k, pool_v, t_new):
    nb, n_pages = page_table.shape
    npp = _pages_per_step(n_pages, 32)
    rows = qbd.shape[1]
    past_len = n_pages * PAGE_SIZE
    topk = min(TOPK_MAX, (past_len + t_new) // 4)
    sel_tk = PAGE_SIZE
    while sel_tk < 2048 and past_len % (2 * sel_tk) == 0:
        sel_tk *= 2
    per_b = lambda shape: pl.BlockSpec((1,) + shape, lambda b, c, pt: (b,) + (0,) * len(shape))
    in_specs = ([per_b((rows, HD)), per_b((t_new, past_len)), per_b((t_new, PAGE_SIZE)),
                 per_b((PAGE_SIZE, HD)), per_b((PAGE_SIZE, HD))]
                + _page_specs(npp, HD, PAGE_SIZE) + _page_specs(npp, HD, PAGE_SIZE))
    grid_spec = pltpu.PrefetchScalarGridSpec(
        num_scalar_prefetch=1, grid=(nb, n_pages // npp), in_specs=in_specs,
        out_specs=[per_b((t_new, HD))],
        scratch_shapes=[pltpu.VMEM((t_new, past_len + sel_tk), F32),
                        pltpu.VMEM((t_new, 1), F32),
                        pltpu.VMEM((1, rows, LANES), F32), pltpu.VMEM((1, rows, LANES), F32),
                        pltpu.VMEM((1, rows, HD), F32),
                        pltpu.VMEM((HD, npp * PAGE_SIZE), BF16),
                        pltpu.VMEM((HD, npp * PAGE_SIZE), BF16)])
    return pl.pallas_call(
        functools.partial(_sdsa_kernel, npp, t_new, topk, past_len, sel_tk), grid_spec=grid_spec,
        out_shape=[jax.ShapeDtypeStruct((nb, t_new, HD), BF16)],
        compiler_params=pltpu.CompilerParams(
            dimension_semantics=("arbitrary", "arbitrary"), vmem_limit_bytes=VMEM_LIMIT),
        name="sample_dsa",
    )(page_table, qbd, scp, scn, kbn, vbn, *([pool_k] * npp), *([pool_v] * npp))[0]


def _pick_tile(n, candidates):
    for c in candidates:
        if n % c == 0:
            return c
    return n


def _pad_rows(a, rows):
    return jnp.pad(a, ((0, 0), (0, rows - a.shape[1]), (0, 0)))


def _head_major(a, nb, t, heads):
    w = a.shape[-1] // heads
    return a.reshape(nb, t, heads, w).transpose(0, 2, 1, 3).reshape(nb, heads * t, w)


def kernel(x_prompt, x_sample, cache_mla_ckv, cache_mla_kpe, cache_dsa_k, cache_dsa_v, cache_idx_k,
           state_ffn_conv, page_table, meta_tokens, norm1_g, w_in, g_q, g_kv, w_uq, w_uk, w_uv,
           w_br_a, w_br_b, w_o, norm2_g, w_ffn_g, w_ffn_u, ffn_conv_w, ffn_conv_b, w_ffn_d, final_g):
    depth = w_in.shape[0]
    assert depth == 1
    nb, seq, d = x_prompt.shape
    nsb, t_new, _ = x_sample.shape
    n_pages = page_table.shape[1]
    past_len = n_pages * PAGE_SIZE
    l_valid = N_META + seq
    tq, tk = 256, 256
    lp = -(-l_valid // tk) * tk
    assert l_valid % SUBLANES == 0 and t_new % SUBLANES == 0 and t_new <= PAGE_SIZE
    d_ff = w_ffn_g.shape[-1]
    l = 0

    wi = w_in[l]
    cuts = np.cumsum([MLA_Q_LORA, MLA_KV_LORA, MLA_ROPE, HD, HD, HD, HD, IDX_DIM, IDX_HEADS, d, d])
    c_q, c_kv, k_pe, q_b, k_b, v_b, q_i, k_i, w_i, g_a, g_b = jnp.split(wi, cuts[:-1], axis=1)
    pad = jnp.zeros((d, LANES - IDX_DIM - MLA_ROPE - IDX_HEADS), wi.dtype)
    win = jnp.concatenate([c_q, c_kv, q_b, k_b, v_b, q_i, g_a, g_b, k_i, k_pe, w_i, pad],
                          axis=1).astype(BF16)
    wuq = jnp.concatenate([w_uq[l][:, :, :MLA_NOPE].reshape(MLA_Q_LORA, -1),
                           w_uq[l][:, :, MLA_NOPE:].reshape(MLA_Q_LORA, -1)], axis=1).astype(BF16)
    wuk2 = w_uk[l].reshape(MLA_KV_LORA, HD)
    wuv2 = w_uv[l].reshape(MLA_KV_LORA, HD)
    wkv_p = jnp.concatenate([wuk2, wuv2], axis=1).astype(BF16)
    ukt = w_uk[l].transpose(1, 2, 0)
    eye = jnp.eye(MLA_HEADS, dtype=ukt.dtype)
    wuk_bd = (ukt[:, :, None, :] * eye[:, None, :, None]).reshape(HD, MLA_HEADS * MLA_KV_LORA)
    wuk_bd = wuk_bd.astype(BF16)
    g1 = norm1_g[l][None]
    gq = g_q[l][None]
    gkv = g_kv[l][None]
    g2 = norm2_g[l][None]
    gf = final_g[None]
    wa, wb, wo = w_br_a[l].astype(BF16), w_br_b[l].astype(BF16), w_o[l].astype(BF16)
    wg, wu, wd = w_ffn_g[l].astype(BF16), w_ffn_u[l].astype(BF16), w_ffn_d[l].astype(BF16)
    cw, cb = ffn_conv_w[l], ffn_conv_b[l][None]

    meta = jnp.broadcast_to(meta_tokens[None].astype(x_prompt.dtype), (nb, N_META, d))
    xp = jnp.concatenate([meta, x_prompt, jnp.zeros((nb, lp - l_valid, d), x_prompt.dtype)], axis=1)
    tm = _pick_tile(lp, (384, 256, 128))
    tabs_p = _rope_tables(np.arange(lp))
    (ckv_p, kb_p, vb_p, small_p, qn, qp, qb, qi, gate_p, kcat, vm, kbb, vbb, ki2) = _proj_call(
        xp, tabs_p, g1, gq, gkv, win, wuq, wkv_p, tm, sample=False, l_out=l_valid)
    attn_args = (qn, qp, qb, qi, small_p, kcat, vm, kbb, vbb, ki2, l_valid)
    full_tiles, rest_rows = divmod(l_valid, tq)
    if 0 < rest_rows <= _TAIL_TQ and full_tiles:
        mla_p, dsa_p = _attn_call(*attn_args, tq, tk, n_keep=full_tiles)
        tk_tail = _pick_tile(lp, (3 * tk, tk))
        mla_p, dsa_p = _attn_call(*attn_args, _TAIL_TQ, tk_tail, j0=full_tiles * tq // _TAIL_TQ,
                                  n_tiles=1, into=(mla_p, dsa_p))
    else:
        mla_p, dsa_p = _attn_call(*attn_args, tq, tk)
    tm_ffn = _pick_tile(lp, (768, 384, 256, 128))
    x2_p, h2_p = _merge_call(xp, mla_p, dsa_p, gate_p, wa, wb, wo, g2, tm_ffn, "merge_prompt")
    zstate = jnp.zeros((1, SUBLANES, d_ff), F32)
    y_p, tail_p = _ffn_call(h2_p, x2_p, zstate, zstate, wg, wu, wd, cw, cb, gf, tm_ffn, l_valid,
                            False, "ffn_prompt")

    ns = nsb * t_new
    xs = x_sample.reshape(1, ns, d)
    tabs_s = _rope_tables(past_len + (np.arange(ns) % t_new))
    (ckv_s, kb_s, vb_s, small_s, qlat, qp_s, qb_s, qi_s, gate_s) = _proj_call(
        xs, tabs_s, g1, gq, gkv, win, wuq, wuk_bd, ns, sample=True)
    qabs = _head_major(qlat, nsb, t_new, MLA_HEADS)
    qpe_r = _head_major(qp_s, nsb, t_new, MLA_HEADS)
    qi_r = _head_major(qi_s, nsb, t_new, IDX_HEADS)
    w_rows = small_s[0, :, _L_WI:_L_WI + IDX_HEADS].reshape(nsb, t_new, IDX_HEADS)
    w_rows = jnp.broadcast_to(w_rows.transpose(0, 2, 1).reshape(nsb, IDX_HEADS * t_new, 1),
                              (nsb, IDX_HEADS * t_new, LANES))
    qb_r = _head_major(qb_s, nsb, t_new, DSA_HEADS)
    qbd = jnp.where(_np_head_mask(t_new), jnp.tile(qb_r, (1, 1, DSA_HEADS)), jnp.zeros((), BF16))
    new_rows = lambda a: _pad_rows(a.reshape(nsb, t_new, -1), PAGE_SIZE).astype(BF16)
    ckvn = new_rows(ckv_s)
    kpen = new_rows(small_s[..., _L_KPE:_L_KPE + MLA_ROPE])
    ikn = new_rows(small_s[..., _L_KI:_L_KI + IDX_DIM])
    kbn, vbn = new_rows(kb_s), new_rows(vb_s)
    n_pool = cache_dsa_k.shape[1]
    keys_minor = lambda pool: jnp.moveaxis(pool[l], 1, -1).reshape(n_pool, -1, PAGE_SIZE)
    mla_s, scp, scn = _smla_call(page_table, qabs, qpe_r, qi_r, w_rows, ckvn, kpen, ikn,
                                 wuv2.astype(BF16), cache_mla_ckv[l], keys_minor(cache_mla_kpe),
                                 keys_minor(cache_idx_k), t_new)
    dsa_s = _sdsa_call(page_table, qbd, scp, scn, kbn, vbn,
                       keys_minor(cache_dsa_k), keys_minor(cache_dsa_v), t_new)
    x2_s, h2_s = _merge_call(xs, mla_s.reshape(1, ns, HD), dsa_s.reshape(1, ns, HD), gate_s,
                             wa, wb, wo, g2, ns, "merge_sample")
    st = state_ffn_conv[l]
    zrow = jnp.zeros((nsb, 1, d_ff), st.dtype)
    s1 = jnp.concatenate([st[:, 1:2]] + [zrow] * (t_new - 1), axis=1).reshape(1, ns, d_ff)
    s2 = jnp.concatenate([st[:, 0:1], st[:, 1:2]] + [zrow] * (t_new - 2), axis=1).reshape(1, ns, d_ff)
    y_s, tail_s = _ffn_call(h2_s, x2_s, s1, s2, wg, wu, wd, cw, cb, gf, ns, t_new, True,
                            "ffn_sample")

    y_prompt = y_p[:, N_META:l_valid]
    y_sample = y_s.reshape(nsb, t_new, d)
    cut = lambda a: a[:, :l_valid]
    new_ckv_p = ckv_p[None]
    new_kpe_p = cut(small_p)[..., _L_KPE:_L_KPE + MLA_ROPE][None]
    pos_last = lambda a: jnp.moveaxis(
        a.reshape(nb, DSA_HEADS, DSA_HEAD_DIM, l_valid), -1, 1)[None]
    new_k_p, new_v_p = pos_last(kb_p), pos_last(vb_p)
    new_ik_p = cut(small_p)[..., _L_KI:_L_KI + IDX_DIM][None]
    new_conv_p = tail_p[:, SUBLANES - (CONV_W - 1):][None]
    per_s = lambda a: a.reshape(nsb, t_new, -1)
    new_ckv_s = per_s(ckv_s)[None]
    new_kpe_s = per_s(small_s)[..., _L_KPE:_L_KPE + MLA_ROPE][None]
    new_k_s = per_s(kb_s).reshape(1, nsb, t_new, DSA_HEADS, DSA_HEAD_DIM)
    new_v_s = per_s(vb_s).reshape(1, nsb, t_new, DSA_HEADS, DSA_HEAD_DIM)
    new_ik_s = per_s(small_s)[..., _L_KI:_L_KI + IDX_DIM][None]
    new_conv_s = per_s(tail_s)[:, t_new - (CONV_W - 1):][None]
    return (y_prompt, y_sample, new_ckv_p, new_kpe_p, new_k_p, new_v_p, new_ik_p, new_conv_p,
            new_ckv_s, new_kpe_s, new_k_s, new_v_s, new_ik_s, new_conv_s)


def _np_head_mask(t_new):
    r = np.arange(DSA_HEADS * t_new)[:, None] // t_new
    c = np.arange(HD)[None, :] // DSA_HEAD_DIM
    return jnp.asarray(r == c)[None]
```
